```python
import jax, jax.numpy as jnp
from jax import lax
import numpy as np

D_MODEL = 1024
BATCH = 4
SEQ = 8192
DEPTH = 1

MEM_LEN = 256
FOURIER_GROUPS = 4
FOURIER_GROUP_DIM = 128
FOURIER_W = FOURIER_GROUPS * FOURIER_GROUP_DIM
HGRN_HEADS = 4
HGRN_DK = 128
HGRN_DV = 128
HGRN_W = HGRN_HEADS * HGRN_DK
HGRN_CHUNK = 64
XATTN_HEADS = 4
XATTN_HEAD_DIM = 128
XATTN_W = XATTN_HEADS * XATTN_HEAD_DIM
N_BRANCHES = 3
N_IN = FOURIER_W + 5 * HGRN_W + XATTN_W + N_BRANCHES * D_MODEL
N_EXPERTS = 256
TOP_K = 8
N_GROUPS = 8
TOPK_GROUPS = 4
EXPERT_FF = 256
SHARED_FF = 256
ROUTED_SCALE = 2.5
MOE_BLOCK = 128
LN_EPS = 1e-5
DEEPNORM_ALPHA = (2 * DEPTH) ** 0.25
DEEPNORM_BETA = (8 * DEPTH) ** -0.25

kernel_name = "hybrid_fnet_hgrn2_memxattn_moe_deepnorm"


def layer_norm(x, g, b):
    xf = x.astype(jnp.float32)
    mu = jnp.mean(xf, axis=-1, keepdims=True)
    var = jnp.mean(jnp.square(xf - mu), axis=-1, keepdims=True)
    y = (xf - mu) * lax.rsqrt(var + LN_EPS) * g.astype(jnp.float32) + b.astype(jnp.float32)
    return y.astype(x.dtype)


def hgrn_lower_bounds(p):
    sm = jax.nn.softmax(p.astype(jnp.float32), axis=0)
    return jnp.cumsum(sm, axis=0)[:DEPTH]


def _heads(t, n_heads):
    b, s, w = t.shape
    return t.reshape(b, s, n_heads, w // n_heads).transpose(0, 2, 1, 3)


def fourier_mix(u):
    b, s, _ = u.shape
    ug = u.astype(jnp.float32).reshape(b, s, FOURIER_GROUPS, FOURIER_GROUP_DIM)
    z = jnp.fft.fft2(ug, axes=(1, 3), norm="ortho")
    return jnp.real(z).reshape(b, s, FOURIER_W).astype(u.dtype)


def hgrn2_chunkwise(q, k, v, log_f):
    b, h, s, dk = q.shape
    dv = v.shape[-1]
    c = HGRN_CHUNK
    n = s // c
    q, k, log_f = (t.reshape(b, h, n, c, dk) for t in (q, k, log_f))
    v = v.reshape(b, h, n, c, dv)
    a = jnp.cumsum(log_f, axis=3)
    a_end = a[:, :, :, -1:, :]
    q_dec = q * jnp.exp(a)
    k_inv = k * jnp.exp(-a)
    scores = jnp.einsum('bhntd,bhnsd->bhnts', q_dec, k_inv)
    mask = jnp.tril(jnp.ones((c, c), dtype=bool))
    scores = jnp.where(mask, scores, 0.0)
    o_intra = jnp.einsum('bhnts,bhnse->bhnte', scores, v)
    k_end = k * jnp.exp(a_end - a)
    chunk_update = jnp.einsum('bhnsd,bhnse->bhnde', k_end, v)
    chunk_decay = jnp.exp(a_end[:, :, :, 0, :])

    def step(state, inp):
        dec, upd = inp
        return dec[..., None] * state + upd, state

    init = jnp.zeros((b, h, dk, dv), q.dtype)
    _, starts = lax.scan(step, init, (jnp.moveaxis(chunk_decay, 2, 0), jnp.moveaxis(chunk_update, 2, 0)))
    starts = jnp.moveaxis(starts, 0, 2)
    o_inter = jnp.einsum('bhntd,bhnde->bhnte', q_dec, starts)
    return (o_intra + o_inter).reshape(b, h, s, dv)


def hgrn2_bidirectional(q_raw, i_raw, z_fwd, z_bwd, g_raw, lb_fwd, lb_bwd, norm_g):
    f32 = jnp.float32
    q = _heads(jax.nn.silu(q_raw.astype(f32)), HGRN_HEADS)
    v = _heads(i_raw.astype(f32), HGRN_HEADS)

    def forget(z, lb):
        f = lb + (1.0 - lb) * jax.nn.sigmoid(z.astype(f32))
        return _heads(1.0 - f, HGRN_HEADS), _heads(jnp.log(f), HGRN_HEADS)

    k_f, lf_f = forget(z_fwd, lb_fwd)
    k_b, lf_b = forget(z_bwd, lb_bwd)
    o_fwd = hgrn2_chunkwise(q, k_f, v, lf_f)
    rev = lambda t: jnp.flip(t, axis=2)
    o_bwd = rev(hgrn2_chunkwise(rev(q), rev(k_b), rev(v), rev(lf_b)))
    o = o_fwd + o_bwd
    o = o * lax.rsqrt(jnp.mean(o * o, axis=-1, keepdims=True) + LN_EPS) * norm_g.astype(f32).reshape(HGRN_HEADS, 1, HGRN_DV)
    b, h, s, dv = o.shape
    o = o.transpose(0, 2, 1, 3).reshape(b, s, h * dv)
    return (o * jax.nn.silu(g_raw.astype(f32))).astype(q_raw.dtype)


def memory_cross_attention(q_raw, mem_n, w_kv):
    b, s, _ = q_raw.shape
    m = mem_n.shape[1]
    q = q_raw.reshape(b, s, XATTN_HEADS, XATTN_HEAD_DIM)
    k, v = jnp.split(mem_n @ w_kv, 2, axis=-1)
    k = k.reshape(b, m, XATTN_HEADS, XATTN_HEAD_DIM)
    v = v.reshape(b, m, XATTN_HEADS, XATTN_HEAD_DIM)
    scores = jnp.einsum('bshd,bmhd->bhsm', q, k).astype(jnp.float32) * (XATTN_HEAD_DIM ** -0.5)
    p = jax.nn.softmax(scores, axis=-1).astype(v.dtype)
    return jnp.einsum('bhsm,bmhd->bshd', p, v).reshape(b, s, XATTN_W)


def route(x_flat, w_router, router_bias):
    t = x_flat.shape[0]
    affinity = jax.nn.sigmoid((x_flat @ w_router).astype(jnp.float32))
    sel = affinity + router_bias.astype(jnp.float32)
    grp = sel.reshape(t, N_GROUPS, N_EXPERTS // N_GROUPS)
    grp_score = jnp.sum(lax.top_k(grp, 2)[0], axis=-1)
    _, top_groups = lax.top_k(grp_score, TOPK_GROUPS)
    group_mask = jnp.any(top_groups[:, :, None] == jnp.arange(N_GROUPS)[None, None, :], axis=1)
    expert_mask = jnp.repeat(group_mask, N_EXPERTS // N_GROUPS, axis=1)
    sel = jnp.where(expert_mask, sel, -jnp.inf)
    _, expert_idx = lax.top_k(sel, TOP_K)
    w = jnp.take_along_axis(affinity, expert_idx, axis=1)
    w = w / jnp.sum(w, axis=-1, keepdims=True) * ROUTED_SCALE
    return expert_idx, w


def routed_experts(x_flat, expert_idx, gate_w, w_gate, w_up, w_down):
    t, d = x_flat.shape
    n_rows = t * TOP_K
    flat_e = expert_idx.reshape(-1)
    order = jnp.argsort(flat_e)
    sorted_e = flat_e[order]
    sorted_tok = (order // TOP_K).astype(jnp.int32)
    sorted_gate = gate_w.reshape(-1)[order].astype(x_flat.dtype)
    counts = jnp.bincount(flat_e, length=N_EXPERTS)
    padded = (counts + MOE_BLOCK - 1) // MOE_BLOCK * MOE_BLOCK
    start = jnp.cumsum(counts) - counts
    pend = jnp.cumsum(padded)
    pstart = pend - padded
    dest = pstart[sorted_e] + (jnp.arange(n_rows) - start[sorted_e])
    n_blocks = -(-n_rows // MOE_BLOCK) + N_EXPERTS
    r_pad = n_blocks * MOE_BLOCK
    row_tok = jnp.full((r_pad,), t, dtype=jnp.int32).at[dest].set(sorted_tok)
    row_gate = jnp.zeros((r_pad,), x_flat.dtype).at[dest].set(sorted_gate)
    block_start = jnp.arange(n_blocks) * MOE_BLOCK
    block_exp = jnp.minimum(jnp.searchsorted(pend, block_start, side='right'), N_EXPERTS - 1)
    x_pad = jnp.concatenate([x_flat, jnp.zeros((1, d), x_flat.dtype)], axis=0)

    def body(acc, blk):
        tok, g, e = blk
        xb = x_pad[tok]
        hb = jax.nn.silu(xb @ w_gate[e]) * (xb @ w_up[e])
        return acc.at[tok].add((hb @ w_down[e]) * g[:, None]), None

    acc0 = jnp.zeros((t + 1, d), x_flat.dtype)
    acc, _ = lax.scan(body, acc0, (row_tok.reshape(n_blocks, MOE_BLOCK), row_gate.reshape(n_blocks, MOE_BLOCK), block_exp))
    return acc[:t]


def setup_inputs(seed: int = 0) -> dict:
    key = jax.random.key(seed)
    ks = iter(jax.random.split(key, 40))
    L, D = DEPTH, D_MODEL

    def nrm(shape, scale):
        return jax.random.normal(next(ks), shape, jnp.float32) * scale

    return {
        "x": nrm((BATCH, SEQ, D), 1.0),
        "mem": nrm((BATCH, MEM_LEN, D), 1.0),
        "ln_in_g": 1.0 + nrm((D,), 0.02),
        "ln_in_b": nrm((D,), 0.02),
        "ln_mem_g": 1.0 + nrm((D,), 0.02),
        "ln_mem_b": nrm((D,), 0.02),
        "hgrn_lb_fwd": 1.0 + nrm((DEPTH + 1, HGRN_W), 0.1),
        "hgrn_lb_bwd": 1.0 + nrm((DEPTH + 1, HGRN_W), 0.1),
        "w_in": nrm((L, D, N_IN), D ** -0.5),
        "b_gate": nrm((L, N_BRANCHES * D), 0.02),
        "hgrn_norm_g": 1.0 + nrm((L, HGRN_W), 0.02),
        "w_mem_kv": nrm((L, D, 2 * XATTN_W), D ** -0.5),
        "w_fourier_o": nrm((L, FOURIER_W, D), FOURIER_W ** -0.5),
        "b_fourier_o": nrm((L, D), 0.02),
        "w_hgrn_o": nrm((L, HGRN_W, D), HGRN_W ** -0.5),
        "w_xattn_o": nrm((L, XATTN_W, D), XATTN_W ** -0.5),
        "w_out": nrm((L, D, D), DEEPNORM_BETA * D ** -0.5),
        "ln1_g": 1.0 + nrm((L, D), 0.02),
        "ln1_b": nrm((L, D), 0.02),
        "w_router": nrm((L, D, N_EXPERTS), D ** -0.5),
        "router_bias": nrm((L, N_EXPERTS), 0.01),
        "w_exp_gate": nrm((L, N_EXPERTS, D, EXPERT_FF), D ** -0.5),
        "w_exp_up": nrm((L, N_EXPERTS, D, EXPERT_FF), D ** -0.5),
        "w_exp_down": nrm((L, N_EXPERTS, EXPERT_FF, D), DEEPNORM_BETA * EXPERT_FF ** -0.5),
        "w_sh_gate": nrm((L, D, SHARED_FF), D ** -0.5),
        "w_sh_up": nrm((L, D, SHARED_FF), D ** -0.5),
        "w_sh_down": nrm((L, SHARED_FF, D), DEEPNORM_BETA * SHARED_FF ** -0.5),
        "ln2_g": 1.0 + nrm((L, D), 0.02),
        "ln2_b": nrm((L, D), 0.02),
    }


def reference(x, mem, ln_in_g, ln_in_b, ln_mem_g, ln_mem_b, hgrn_lb_fwd, hgrn_lb_bwd,
              w_in, b_gate, hgrn_norm_g, w_mem_kv, w_fourier_o, b_fourier_o, w_hgrn_o, w_xattn_o,
              w_out, ln1_g, ln1_b, w_router, router_bias, w_exp_gate, w_exp_up, w_exp_down,
              w_sh_gate, w_sh_up, w_sh_down, ln2_g, ln2_b):
    b, s, d = x.shape
    h = layer_norm(x, ln_in_g, ln_in_b)
    mem_n = layer_norm(mem, ln_mem_g, ln_mem_b)
    lb_fwd = hgrn_lower_bounds(hgrn_lb_fwd)
    lb_bwd = hgrn_lower_bounds(hgrn_lb_bwd)
    splits = [int(v) for v in np.cumsum([FOURIER_W, HGRN_W, HGRN_W, HGRN_W, HGRN_W, HGRN_W, XATTN_W])]

    for l in range(DEPTH):
        proj = h @ w_in[l]
        u_f, hq, hi, zf, zb, hg, xq, gate_logits = jnp.split(proj, splits, axis=-1)
        y_f = fourier_mix(u_f) @ w_fourier_o[l] + b_fourier_o[l]
        y_h = hgrn2_bidirectional(hq, hi, zf, zb, hg, lb_fwd[l], lb_bwd[l], hgrn_norm_g[l]) @ w_hgrn_o[l]
        y_x = memory_cross_attention(xq, mem_n, w_mem_kv[l]) @ w_xattn_o[l]
        g = jax.nn.sigmoid(gate_logits + b_gate[l]).reshape(b, s, N_BRANCHES, d)
        merged = g[:, :, 0, :] * y_f + g[:, :, 1, :] * y_h + g[:, :, 2, :] * y_x
        h = layer_norm(DEEPNORM_ALPHA * h + merged @ w_out[l], ln1_g[l], ln1_b[l])

        x_flat = h.reshape(b * s, d)
        expert_idx, gate_w = route(x_flat, w_router[l], router_bias[l])
        routed = routed_experts(x_flat, expert_idx, gate_w, w_exp_gate[l], w_exp_up[l], w_exp_down[l])
        shared = (jax.nn.silu(x_flat @ w_sh_gate[l]) * (x_flat @ w_sh_up[l])) @ w_sh_down[l]
        h = layer_norm(DEEPNORM_ALPHA * h + (routed + shared).reshape(b, s, d), ln2_g[l], ln2_b[l])

    return h
```

```python
import functools
import math

import numpy as np
import jax
import jax.numpy as jnp
from jax import lax
from jax.experimental import pallas as pl
from jax.experimental.pallas import tpu as pltpu

F32 = jnp.float32
BF16 = jnp.bfloat16

LN_EPS = 1e-5
DEPTH = 1
DEEPNORM_ALPHA = (2 * DEPTH) ** 0.25
FOURIER_GROUP_DIM = 128
FOURIER_W = 512
HGRN_HEADS = 4
HGRN_HEAD_DIM = 128
HGRN_W = HGRN_HEADS * HGRN_HEAD_DIM
HGRN_CHUNK = 64
XATTN_HEADS = 4
XATTN_HEAD_DIM = 128
XATTN_W = XATTN_HEADS * XATTN_HEAD_DIM
N_BRANCHES = 3
N_EXPERTS = 256
TOP_K = 8
N_GROUPS = 8
TOPK_GROUPS = 4
GROUP_SIZE = N_EXPERTS // N_GROUPS
ROUTED_SCALE = 2.5

V7X_SCOPED_VMEM_CAP_BYTES = 60000 * 1024

INPROJ_TM = 1024
INPROJ_TN = 512
FOURIER_P = 128
HGRN_ROWS = 512
MERGE_TM = 256
ROUTE_TM = 512
MOE_BM = 256
GATHER_ROWS = 256
FINAL_TM = 256


def _params(semantics, vmem_bytes):
    return pltpu.CompilerParams(dimension_semantics=semantics,
                                vmem_limit_bytes=int(min(vmem_bytes, V7X_SCOPED_VMEM_CAP_BYTES)))


def _layer_norm(x, g, b):
    mu = jnp.mean(x, axis=-1, keepdims=True)
    xc = x - mu
    var = jnp.mean(xc * xc, axis=-1, keepdims=True)
    return xc * lax.rsqrt(var + LN_EPS) * g + b


def _silu(x):
    return x * jax.nn.sigmoid(x)


def _dot(a, b):
    return jnp.dot(a, b, preferred_element_type=F32)


def _dot_nt(a, b):
    return lax.dot_general(a, b, (((1,), (1,)), ((), ())), preferred_element_type=F32)


def _dot_tn(a, b):
    return lax.dot_general(a, b, (((0,), (0,)), ((), ())), preferred_element_type=F32)


def _inproj_kernel(x_ref, g_ref, b_ref, w_ref, o_ref, hn_ref):
    @pl.when(pl.program_id(1) == 0)
    def _():
        hn_ref[...] = _layer_norm(x_ref[...], g_ref[...], b_ref[...]).astype(BF16)

    o_ref[...] = _dot(hn_ref[...], w_ref[...])


def _inproj(x2, g, b, w):
    t, d = x2.shape
    n = w.shape[1]
    tm, tn = min(INPROJ_TM, t), INPROJ_TN
    vmem = 2 * (tm * d * 4 + d * tn * 2 + tm * tn * 4) + tm * d * 2 + 3 * tm * d * 4
    return pl.pallas_call(
        _inproj_kernel,
        grid=(t // tm, n // tn),
        in_specs=[
            pl.BlockSpec((tm, d), lambda i, j: (i, 0)),
            pl.BlockSpec((1, d), lambda i, j: (0, 0)),
            pl.BlockSpec((1, d), lambda i, j: (0, 0)),
            pl.BlockSpec((d, tn), lambda i, j: (0, j)),
        ],
        out_specs=pl.BlockSpec((tm, tn), lambda i, j: (i, j)),
        out_shape=jax.ShapeDtypeStruct((t, n), F32),
        scratch_shapes=[pltpu.VMEM((tm, d), BF16)],
        compiler_params=_params(("parallel", "arbitrary"), vmem),
        name="ln_inproj",
    )(x2, g, b, w)


def _mem_kv_kernel(m_ref, g_ref, b_ref, w_ref, o_ref):
    mn = _layer_norm(m_ref[0], g_ref[...], b_ref[...]).astype(BF16)
    o_ref[0] = _dot(mn, w_ref[...]).astype(BF16)


def _mem_kv(mem, g, b, w):
    bsz, m, d = mem.shape
    n = w.shape[1]
    vmem = 2 * (m * d * 4 + d * n * 2 + m * n * 2) + 4 * m * d * 4
    return pl.pallas_call(
        _mem_kv_kernel,
        grid=(bsz,),
        in_specs=[
            pl.BlockSpec((1, m, d), lambda i: (i, 0, 0)),
            pl.BlockSpec((1, d), lambda i: (0, 0)),
            pl.BlockSpec((1, d), lambda i: (0, 0)),
            pl.BlockSpec((d, n), lambda i: (0, 0)),
        ],
        out_specs=pl.BlockSpec((1, m, n), lambda i: (i, 0, 0)),
        out_shape=jax.ShapeDtypeStruct((bsz, m, n), BF16),
        compiler_params=_params(("parallel",), vmem),
        name="mem_kv",
    )(mem, g, b, w)


def _fourier_tables(s, c, p):
    q = s // p
    ki = np.arange(p, dtype=np.int64)
    i = np.arange(p, dtype=np.int64)
    j = np.arange(q, dtype=np.int64)
    ph = (ki[None, :, None] * (q * i[None, None, :] + j[:, None, None])) % s
    ang = 2.0 * np.pi * ph.astype(np.float64) / s
    m1 = np.concatenate([np.cos(ang), -np.sin(ang)], axis=1)
    kj = np.arange(q, dtype=np.int64)
    a2 = 2.0 * np.pi * ((kj[:, None] * j[None, :]) % q).astype(np.float64) / q
    m2 = np.block([[np.cos(a2), np.sin(a2)], [-np.sin(a2), np.cos(a2)]])
    cc = np.arange(c, dtype=np.int64)
    a3 = 2.0 * np.pi * ((cc[:, None] * cc[None, :]) % c).astype(np.float64) / c
    mc = np.concatenate([np.cos(a3), np.sin(a3)], axis=0) / math.sqrt(s * c)
    return (jnp.asarray(m1, dtype=BF16), jnp.asarray(m2, dtype=BF16), jnp.asarray(mc, dtype=BF16))


def _fourier_kernel(x_ref, m1_ref, m2_ref, mc_ref, o_ref, ys_ref, *, p, q):
    def stage1(j, carry):
        xj = x_ref[pl.ds(j, p, stride=q), :].astype(BF16)
        y = _dot(m1_ref[j], xj)
        ys_ref[pl.ds(j, p, stride=2 * q), :] = y[:p]
        ys_ref[pl.ds(q + j, p, stride=2 * q), :] = y[p:]
        return carry

    lax.fori_loop(0, q, stage1, 0)

    def stage2(ki, carry):
        start = pl.multiple_of(ki * (2 * q), 2 * q)
        y = ys_ref[pl.ds(start, 2 * q), :].astype(BF16)
        z = _dot(m2_ref[...], y)
        zc = jnp.concatenate([z[:q], z[q:]], axis=1).astype(BF16)
        o_ref[pl.ds(ki, q, stride=p), :] = _dot(zc, mc_ref[...])
        return carry

    lax.fori_loop(0, p, stage2, 0)


def _fourier(proj, bsz, s, col_block0):
    c = FOURIER_GROUP_DIM
    groups = FOURIER_W // c
    p = min(FOURIER_P, s // 8)
    q = s // p
    m1, m2, mc = _fourier_tables(s, c, p)
    vmem = 2 * (s * c * 4 + m1.size * 2 + m2.size * 2 + mc.size * 2 + s * c * 4) + 2 * s * c * 4 + 8 * p * c * 4
    return pl.pallas_call(
        functools.partial(_fourier_kernel, p=p, q=q),
        grid=(bsz, groups),
        in_specs=[
            pl.BlockSpec((s, c), lambda b, g: (b, col_block0 + g)),
            pl.BlockSpec((q, 2 * p, p), lambda b, g: (0, 0, 0)),
            pl.BlockSpec((2 * q, 2 * q), lambda b, g: (0, 0)),
            pl.BlockSpec((2 * c, c), lambda b, g: (0, 0)),
        ],
        out_specs=pl.BlockSpec((s, c), lambda b, g: (b, g)),
        out_shape=jax.ShapeDtypeStruct((bsz * s, FOURIER_W), F32),
        scratch_shapes=[pltpu.VMEM((2 * s, c), F32)],
        compiler_params=_params(("parallel", "parallel"), vmem),
        name="fourier_mix",
    )(proj, m1, m2, mc)


def _hgrn_kernel(*refs, reverse, finalize, n_chunks):
    if finalize:
        q_ref, i_ref, z_ref, lb_ref, of_ref, g_ref, ng_ref, o_ref, st_ref = refs
    else:
        q_ref, i_ref, z_ref, lb_ref, o_ref, st_ref = refs

    @pl.when(pl.program_id(1) == 0)
    def _():
        st_ref[...] = jnp.zeros_like(st_ref)

    ck = HGRN_CHUNK
    hd = HGRN_HEAD_DIM
    row = lax.broadcasted_iota(jnp.int32, (ck, ck), 0)
    col = lax.broadcasted_iota(jnp.int32, (ck, ck), 1)
    tri = (col >= row) if reverse else (col <= row)
    cum = tri.astype(F32)
    lb = lb_ref[...]

    order = range(n_chunks - 1, -1, -1) if reverse else range(n_chunks)
    for c in order:
        rows = pl.ds(c * ck, ck)
        q = _silu(q_ref[rows, :])
        v = i_ref[rows, :]
        f = lb + (1.0 - lb) * jax.nn.sigmoid(z_ref[rows, :])
        k = 1.0 - f
        a = jnp.dot(cum, jnp.log(f), preferred_element_type=F32, precision=lax.Precision.HIGHEST)
        a_end = a[0:1, :] if reverse else a[ck - 1:ck, :]
        q_dec = (q * jnp.exp(a)).astype(BF16)
        k_inv = (k * jnp.exp(-a)).astype(BF16)
        k_end = (k * jnp.exp(a_end - a)).astype(BF16)
        decay = jnp.exp(a_end)
        vb = v.astype(BF16)
        outs = []
        for h in range(HGRN_HEADS):
            sl = slice(h * hd, (h + 1) * hd)
            sc = jnp.where(tri, _dot_nt(q_dec[:, sl], k_inv[:, sl]), 0.0)
            st = st_ref[h]
            o_h = _dot(sc.astype(BF16), vb[:, sl]) + _dot_nt(q_dec[:, sl], st.astype(BF16))
            st_ref[h] = st * decay[:, sl] + _dot_tn(vb[:, sl], k_end[:, sl])
            if finalize:
                o_h = o_h + of_ref[rows, sl]
                ms = jnp.mean(o_h * o_h, axis=-1, keepdims=True)
                o_h = o_h * lax.rsqrt(ms + LN_EPS) * ng_ref[:, sl]
            outs.append(o_h)
        o = jnp.concatenate(outs, axis=1)
        if finalize:
            o_ref[rows, :] = (o * _silu(g_ref[rows, :])).astype(o_ref.dtype)
        else:
            o_ref[rows, :] = o


def _hgrn_sweep(proj, lb, cols, bsz, s, *, reverse, o_fwd=None, norm_g=None):
    w = HGRN_W
    rb = min(HGRN_ROWS, s)
    nb = s // rb
    finalize = o_fwd is not None

    def rmap(b, n):
        return b * nb + (nb - 1 - n if reverse else n)

    def cmap(cb):
        return lambda b, n: (rmap(b, n), cb)

    in_specs = [pl.BlockSpec((rb, w), cmap(cols[0])), pl.BlockSpec((rb, w), cmap(cols[1])),
                pl.BlockSpec((rb, w), cmap(cols[2])), pl.BlockSpec((1, w), lambda b, n: (0, 0))]
    args = [proj, proj, proj, lb]
    if finalize:
        in_specs += [pl.BlockSpec((rb, w), cmap(0)), pl.BlockSpec((rb, w), cmap(cols[3])),
                     pl.BlockSpec((1, w), lambda b, n: (0, 0))]
        args += [o_fwd, proj, norm_g]
    vmem = 2 * (6 * rb * w * 4) + HGRN_HEADS * HGRN_HEAD_DIM * HGRN_HEAD_DIM * 4 + 24 * HGRN_CHUNK * w * 4
    return pl.pallas_call(
        functools.partial(_hgrn_kernel, reverse=reverse, finalize=finalize, n_chunks=rb // HGRN_CHUNK),
        grid=(bsz, nb),
        in_specs=in_specs,
        out_specs=pl.BlockSpec((rb, w), cmap(0)),
        out_shape=jax.ShapeDtypeStruct((bsz * s, w), BF16 if finalize else F32),
        scratch_shapes=[pltpu.VMEM((HGRN_HEADS, HGRN_HEAD_DIM, HGRN_HEAD_DIM), F32)],
        compiler_params=_params(("parallel", "arbitrary"), vmem),
        name="hgrn_bwd" if reverse else "hgrn_fwd",
    )(*args)


def _merge_kernel(x_ref, lg_ref, lbi_ref, gl_ref, xq_ref, fm_ref, ho_ref, kv_ref, wx_ref, wf_ref, bf_ref,
                  wh_ref, bg_ref, wo_ref, g1_ref, b1_ref, o_ref):
    d = x_ref.shape[1]
    h = _layer_norm(x_ref[...], lg_ref[...], lbi_ref[...])

    kv = kv_ref[0]
    xq = xq_ref[...].astype(BF16)
    heads = []
    for hh in range(XATTN_HEADS):
        sl = slice(hh * XATTN_HEAD_DIM, (hh + 1) * XATTN_HEAD_DIM)
        vsl = slice(XATTN_W + hh * XATTN_HEAD_DIM, XATTN_W + (hh + 1) * XATTN_HEAD_DIM)
        sc = _dot_nt(xq[:, sl], kv[:, sl]) * (XATTN_HEAD_DIM ** -0.5)
        sc = sc - jnp.max(sc, axis=-1, keepdims=True)
        e = jnp.exp(sc)
        pr = e / jnp.sum(e, axis=-1, keepdims=True)
        heads.append(_dot(pr.astype(BF16), kv[:, vsl]))
    att = jnp.concatenate(heads, axis=1).astype(BF16)

    y_x = _dot(att, wx_ref[...])
    y_f = _dot(fm_ref[...].astype(BF16), wf_ref[...]) + bf_ref[...]
    y_h = _dot(ho_ref[...], wh_ref[...])
    gate = jax.nn.sigmoid(gl_ref[...] + bg_ref[...])
    merged = gate[:, 0:d] * y_f + gate[:, d:2 * d] * y_h + gate[:, 2 * d:3 * d] * y_x
    y = _dot(merged.astype(BF16), wo_ref[...])
    o_ref[...] = _layer_norm(DEEPNORM_ALPHA * h + y, g1_ref[...], b1_ref[...])


def _merge(x2, ln_g, ln_b, proj, gate_cb, xq_cb, fm, ho, kv, w_xo, w_fo, b_fo, w_ho, b_gate, w_out, g1, b1, s):
    t, d = x2.shape
    tm = min(MERGE_TM, s)
    per_b = s // tm
    m, kvw = kv.shape[1], kv.shape[2]
    gw = N_BRANCHES * d

    def full(shape):
        return pl.BlockSpec(shape, lambda i: tuple(0 for _ in shape))

    vmem = (2 * (tm * d * 4 + tm * gw * 4 + tm * XATTN_W * 4 + tm * FOURIER_W * 4 + tm * HGRN_W * 2 + m * kvw * 2
                 + 3 * XATTN_W * d * 2 + d * d * 2 + tm * d * 4) + 10 * tm * d * 4 + 2 * tm * gw * 4)
    return pl.pallas_call(
        _merge_kernel,
        grid=(t // tm,),
        in_specs=[
            pl.BlockSpec((tm, d), lambda i: (i, 0)),
            full((1, d)), full((1, d)),
            pl.BlockSpec((tm, gw), lambda i: (i, gate_cb)),
            pl.BlockSpec((tm, XATTN_W), lambda i: (i, xq_cb)),
            pl.BlockSpec((tm, FOURIER_W), lambda i: (i, 0)),
            pl.BlockSpec((tm, HGRN_W), lambda i: (i, 0)),
            pl.BlockSpec((1, m, kvw), lambda i: (i // per_b, 0, 0)),
            full((XATTN_W, d)), full((FOURIER_W, d)), full((1, d)), full((HGRN_W, d)), full((1, gw)),
            full((d, d)), full((1, d)), full((1, d)),
        ],
        out_specs=pl.BlockSpec((tm, d), lambda i: (i, 0)),
        out_shape=jax.ShapeDtypeStruct((t, d), F32),
        compiler_params=_params(("parallel",), vmem),
        name="merge_out",
    )(x2, ln_g, ln_b, proj, proj, fm, ho, kv, w_xo, w_fo, b_fo, w_ho, b_gate, w_out, g1, b1)


def _route_kernel(h_ref, wr_ref, bias_ref, idx_ref, gw_ref):
    tm = h_ref.shape[0]
    neg = -jnp.inf
    aff = jax.nn.sigmoid(_dot_nt(wr_ref[...], h_ref[...].astype(BF16)))
    sel = aff + bias_ref[...]

    giota = lax.broadcasted_iota(jnp.int32, (GROUP_SIZE, tm), 0)
    scores = []
    for g in range(N_GROUPS):
        slab = sel[g * GROUP_SIZE:(g + 1) * GROUP_SIZE, :]
        m1 = jnp.max(slab, axis=0, keepdims=True)
        first = jnp.min(jnp.where(slab == m1, giota, GROUP_SIZE), axis=0, keepdims=True)
        m2 = jnp.max(jnp.where(giota == first, neg, slab), axis=0, keepdims=True)
        scores.append(m1 + m2)
    gs = jnp.concatenate(scores, axis=0)

    grow = lax.broadcasted_iota(jnp.int32, (N_GROUPS, tm), 0)
    gsel = jnp.zeros((N_GROUPS, tm), jnp.bool_)
    work = gs
    for _ in range(TOPK_GROUPS):
        m = jnp.max(work, axis=0, keepdims=True)
        first = jnp.min(jnp.where(work == m, grow, N_GROUPS), axis=0, keepdims=True)
        hit = grow == first
        gsel = jnp.logical_or(gsel, hit)
        work = jnp.where(hit, neg, work)

    masked = jnp.concatenate(
        [jnp.where(gsel[g:g + 1, :], sel[g * GROUP_SIZE:(g + 1) * GROUP_SIZE, :], neg) for g in range(N_GROUPS)],
        axis=0)

    erow = lax.broadcasted_iota(jnp.int32, (N_EXPERTS, tm), 0)
    ids, ws = [], []
    for _ in range(TOP_K):
        m = jnp.max(masked, axis=0, keepdims=True)
        first = jnp.min(jnp.where(masked == m, erow, N_EXPERTS), axis=0, keepdims=True)
        hit = erow == first
        ids.append(first)
        ws.append(jnp.sum(jnp.where(hit, aff, 0.0), axis=0, keepdims=True))
        masked = jnp.where(hit, neg, masked)
    w = jnp.concatenate(ws, axis=0)
    idx_ref[...] = jnp.concatenate(ids, axis=0)
    gw_ref[...] = w / jnp.sum(w, axis=0, keepdims=True) * ROUTED_SCALE


def _route(h1, w_router_t, bias_col):
    t, d = h1.shape
    tm = min(ROUTE_TM, t)
    vmem = 2 * (tm * d * 4 + N_EXPERTS * d * 2 + 2 * TOP_K * tm * 4) + tm * d * 2 + 12 * N_EXPERTS * tm * 4
    return pl.pallas_call(
        _route_kernel,
        grid=(t // tm,),
        in_specs=[
            pl.BlockSpec((tm, d), lambda i: (i, 0)),
            pl.BlockSpec((N_EXPERTS, d), lambda i: (0, 0)),
            pl.BlockSpec((N_EXPERTS, 1), lambda i: (0, 0)),
        ],
        out_specs=[pl.BlockSpec((TOP_K, tm), lambda i: (0, i)), pl.BlockSpec((TOP_K, tm), lambda i: (0, i))],
        out_shape=[jax.ShapeDtypeStruct((TOP_K, t), jnp.int32), jax.ShapeDtypeStruct((TOP_K, t), F32)],
        compiler_params=_params(("parallel",), vmem),
        name="router",
    )(h1, w_router_t, bias_col)


def _row_copy(src_ref, dst_ref, src_row, dst_row, sem):
    return pltpu.make_async_copy(src_ref.at[pl.ds(src_row, 1), :], dst_ref.at[pl.ds(dst_row, 1), :], sem)


def _gather_kernel(idx_ref, src_ref, o_ref, sem):
    rows = o_ref.shape[0]

    def issue(r, carry):
        _row_copy(src_ref, o_ref, idx_ref[r], r, sem).start()
        return carry

    lax.fori_loop(0, rows, issue, 0)

    def drain(r, carry):
        _row_copy(src_ref, o_ref, 0, r, sem).wait()
        return carry

    lax.fori_loop(0, rows, drain, 0)


def _gather_rows(src, idx):
    n = idx.shape[0]
    d = src.shape[1]
    rows = min(GATHER_ROWS, n)
    vmem = 4 * rows * d * 4
    return pl.pallas_call(
        _gather_kernel,
        grid=(n // rows,),
        in_specs=[
            pl.BlockSpec((rows,), lambda i: (i,), memory_space=pltpu.SMEM),
            pl.BlockSpec(memory_space=pl.ANY),
        ],
        out_specs=pl.BlockSpec((rows, d), lambda i: (i, 0)),
        out_shape=jax.ShapeDtypeStruct((n, d), src.dtype),
        scratch_shapes=[pltpu.SemaphoreType.DMA(())],
        compiler_params=_params(("arbitrary",), vmem),
        name="gather_rows",
    )(idx, src)


def _expert_kernel(be_ref, x_ref, g_ref, wg_ref, wu_ref, wd_ref, o_ref):
    del be_ref
    xb = x_ref[...].astype(BF16)
    hg = _dot(xb, wg_ref[0].astype(BF16))
    hu = _dot(xb, wu_ref[0].astype(BF16))
    hb = (_silu(hg) * hu).astype(BF16)
    o_ref[...] = _dot(hb, wd_ref[0].astype(BF16)) * g_ref[...]


def _experts(x_sorted, row_gate, block_exp, w_gate, w_up, w_down):
    r, d = x_sorted.shape
    ff = w_gate.shape[2]
    bm = MOE_BM
    vmem = 2 * (2 * bm * d * 4 + bm * 128 * 4 + 3 * d * ff * 4) + 3 * d * ff * 2 + 6 * bm * ff * 4 + bm * d * 4
    return pl.pallas_call(
        _expert_kernel,
        grid_spec=pltpu.PrefetchScalarGridSpec(
            num_scalar_prefetch=1,
            grid=(r // bm,),
            in_specs=[
                pl.BlockSpec((bm, d), lambda i, be: (i, 0)),
                pl.BlockSpec((bm, 1), lambda i, be: (i, 0)),
                pl.BlockSpec((1, d, ff), lambda i, be: (be[i], 0, 0)),
                pl.BlockSpec((1, d, ff), lambda i, be: (be[i], 0, 0)),
                pl.BlockSpec((1, ff, d), lambda i, be: (be[i], 0, 0)),
            ],
            out_specs=pl.BlockSpec((bm, d), lambda i, be: (i, 0)),
        ),
        out_shape=jax.ShapeDtypeStruct((r, d), F32),
        compiler_params=_params(("arbitrary",), vmem),
        name="expert_mlp",
    )(block_exp, x_sorted, row_gate, w_gate, w_up, w_down)


def _dispatch_plan(expert_idx, gate_w, t):
    bm = MOE_BM
    n_rows = t * TOP_K
    flat_e = expert_idx.reshape(-1)
    order = jnp.argsort(flat_e).astype(jnp.int32)
    sorted_e = flat_e[order]
    counts = jnp.bincount(flat_e, length=N_EXPERTS).astype(jnp.int32)
    padded = (counts + bm - 1) // bm * bm
    start = jnp.cumsum(counts) - counts
    pend = jnp.cumsum(padded)
    pstart = pend - padded
    dest = (pstart[sorted_e] + (jnp.arange(n_rows, dtype=jnp.int32) - start[sorted_e])).astype(jnp.int32)
    n_blocks = n_rows // bm + N_EXPERTS
    r_pad = n_blocks * bm
    row_tok = jnp.zeros((r_pad,), jnp.int32).at[dest].set(order // TOP_K)
    row_gate = jnp.zeros((r_pad,), F32).at[dest].set(gate_w.reshape(-1)[order])
    slot_row = jnp.zeros((n_rows,), jnp.int32).at[order].set(dest)
    block_start = jnp.arange(n_blocks, dtype=jnp.int32) * bm
    block_exp = jnp.minimum(jnp.searchsorted(pend, block_start, side='right'), N_EXPERTS - 1).astype(jnp.int32)
    return row_tok, row_gate.reshape(r_pad, 1), slot_row, block_exp


def _final_kernel(h_ref, y_ref, wg_ref, wu_ref, wd_ref, g_ref, b_ref, o_ref):
    d = h_ref.shape[1]
    h = h_ref[...]
    hb = h.astype(BF16)
    shared = _dot((_silu(_dot(hb, wg_ref[...])) * _dot(hb, wu_ref[...])).astype(BF16), wd_ref[...])
    routed = y_ref[:, 0:d]
    for kk in range(1, TOP_K):
        routed = routed + y_ref[:, kk * d:(kk + 1) * d]
    o_ref[...] = _layer_norm(DEEPNORM_ALPHA * h + (routed + shared), g_ref[...], b_ref[...])


def _final(h1, y_tok, w_sg, w_su, w_sd, g2, b2):
    t, d = h1.shape
    ff = w_sg.shape[1]
    tm = min(FINAL_TM, t)
    vmem = 2 * (2 * tm * d * 4 + tm * TOP_K * d * 4 + 3 * d * ff * 2) + 6 * tm * d * 4

    def full(shape):
        return pl.BlockSpec(shape, lambda i: tuple(0 for _ in shape))

    return pl.pallas_call(
        _final_kernel,
        grid=(t // tm,),
        in_specs=[
            pl.BlockSpec((tm, d), lambda i: (i, 0)),
            pl.BlockSpec((tm, TOP_K * d), lambda i: (i, 0)),
            full((d, ff)), full((d, ff)), full((ff, d)), full((1, d)), full((1, d)),
        ],
        out_specs=pl.BlockSpec((tm, d), lambda i: (i, 0)),
        out_shape=jax.ShapeDtypeStruct((t, d), F32),
        compiler_params=_params(("parallel",), vmem),
        name="shared_final",
    )(h1, y_tok, w_sg, w_su, w_sd, g2, b2)


def _lower_bound(p):
    return jnp.cumsum(jax.nn.softmax(p.astype(F32), axis=0), axis=0)[0:1]


def kernel(x, mem, ln_in_g, ln_in_b, ln_mem_g, ln_mem_b, hgrn_lb_fwd, hgrn_lb_bwd, w_in, b_gate, hgrn_norm_g, w_mem_kv, w_fourier_o, b_fourier_o, w_hgrn_o, w_xattn_o, w_out, ln1_g, ln1_b, w_router, router_bias, w_exp_gate, w_exp_up, w_exp_down, w_sh_gate, w_sh_up, w_sh_down, ln2_g, ln2_b):
    bsz, s, d = x.shape
    t = bsz * s
    l = 0
    row = lambda v: v.reshape(1, -1).astype(F32)

    n_gate = N_BRANCHES * d
    n_rest = w_in.shape[2] - n_gate
    w_in_r = jnp.concatenate([w_in[l][:, n_rest:], w_in[l][:, :n_rest]], axis=1).astype(BF16)
    gate_blocks = n_gate // INPROJ_TN
    cb_fourier = gate_blocks
    cb_hq, cb_hi, cb_zf, cb_zb, cb_hg, cb_xq = (gate_blocks + 1 + n for n in range(6))

    x2 = x.reshape(t, d)
    proj = _inproj(x2, row(ln_in_g), row(ln_in_b), w_in_r)
    kv = _mem_kv(mem, row(ln_mem_g), row(ln_mem_b), w_mem_kv[l].astype(BF16))

    fm = _fourier(proj, bsz, s, cb_fourier * (INPROJ_TN // FOURIER_GROUP_DIM))

    lb_f = _lower_bound(hgrn_lb_fwd)
    lb_b = _lower_bound(hgrn_lb_bwd)
    o_fwd = _hgrn_sweep(proj, lb_f, (cb_hq, cb_hi, cb_zf), bsz, s, reverse=False)
    ho = _hgrn_sweep(proj, lb_b, (cb_hq, cb_hi, cb_zb, cb_hg), bsz, s, reverse=True,
                     o_fwd=o_fwd, norm_g=row(hgrn_norm_g[l]))

    h1 = _merge(x2, row(ln_in_g), row(ln_in_b), proj, 0, cb_xq, fm, ho, kv,
                w_xattn_o[l].astype(BF16), w_fourier_o[l].astype(BF16), row(b_fourier_o[l]),
                w_hgrn_o[l].astype(BF16), row(b_gate[l]), w_out[l].astype(BF16), row(ln1_g[l]), row(ln1_b[l]), s)

    idx_t, gw_t = _route(h1, w_router[l].T.astype(BF16), router_bias[l].reshape(N_EXPERTS, 1).astype(F32))
    row_tok, row_gate, slot_row, block_exp = _dispatch_plan(idx_t.T, gw_t.T, t)

    x_sorted = _gather_rows(h1, row_tok)
    y_sorted = _experts(x_sorted, row_gate, block_exp, w_exp_gate[l], w_exp_up[l], w_exp_down[l])
    y_tok = _gather_rows(y_sorted, slot_row).reshape(t, TOP_K * d)

    out = _final(h1, y_tok, w_sh_gate[l].astype(BF16), w_sh_up[l].astype(BF16), w_sh_down[l].astype(BF16),
                 row(ln2_g[l]), row(ln2_b[l]))
    return out.reshape(bsz, s, d)
```

```python
import functools
import math

import numpy as np
import jax
import jax.numpy as jnp
from jax import lax
from jax.experimental import pallas as pl
from jax.experimental.pallas import tpu as pltpu

F32 = jnp.float32
BF16 = jnp.bfloat16

LN_EPS = 1e-5
DEPTH = 1
DEEPNORM_ALPHA = (2 * DEPTH) ** 0.25
FOURIER_GROUP_DIM = 128
FOURIER_W = 512
HGRN_HEADS = 4
HGRN_HEAD_DIM = 128
HGRN_W = HGRN_HEADS * HGRN_HEAD_DIM
HGRN_CHUNK = 64
XATTN_HEADS = 4
XATTN_HEAD_DIM = 128
XATTN_W = XATTN_HEADS * XATTN_HEAD_DIM
N_BRANCHES = 3
N_EXPERTS = 256
TOP_K = 8
N_GROUPS = 8
TOPK_GROUPS = 4
GROUP_SIZE = N_EXPERTS // N_GROUPS
ROUTED_SCALE = 2.5

V7X_SCOPED_VMEM_CAP_BYTES = 60000 * 1024

INPROJ_TM = 1024
INPROJ_TN = 512
FOURIER_P = 128
HGRN_ROWS = 512
MERGE_TM = 256
ROUTE_TM = 512
MOE_BM = 256
DISPATCH_TM = 256
FINAL_TM = 256


def _params(semantics, vmem_bytes):
    return pltpu.CompilerParams(dimension_semantics=semantics,
                                vmem_limit_bytes=int(min(vmem_bytes, V7X_SCOPED_VMEM_CAP_BYTES)))


def _layer_norm(x, g, b):
    mu = jnp.mean(x, axis=-1, keepdims=True)
    xc = x - mu
    var = jnp.mean(xc * xc, axis=-1, keepdims=True)
    return xc * lax.rsqrt(var + LN_EPS) * g + b


def _silu(x):
    return x * jax.nn.sigmoid(x)


def _dot(a, b):
    return jnp.dot(a, b, preferred_element_type=F32)


def _dot_nt(a, b):
    return lax.dot_general(a, b, (((1,), (1,)), ((), ())), preferred_element_type=F32)


def _dot_tn(a, b):
    return lax.dot_general(a, b, (((0,), (0,)), ((), ())), preferred_element_type=F32)


def _inproj_kernel(x_ref, g_ref, b_ref, w_ref, o_ref, hn_ref):
    @pl.when(pl.program_id(1) == 0)
    def _():
        hn_ref[...] = _layer_norm(x_ref[...], g_ref[...], b_ref[...]).astype(BF16)

    o_ref[...] = _dot(hn_ref[...], w_ref[...])


def _inproj(x2, g, b, w):
    t, d = x2.shape
    n = w.shape[1]
    tm, tn = min(INPROJ_TM, t), INPROJ_TN
    vmem = 2 * (tm * d * 4 + d * tn * 2 + tm * tn * 4) + tm * d * 2 + 3 * tm * d * 4
    return pl.pallas_call(
        _inproj_kernel,
        grid=(t // tm, n // tn),
        in_specs=[
            pl.BlockSpec((tm, d), lambda i, j: (i, 0)),
            pl.BlockSpec((1, d), lambda i, j: (0, 0)),
            pl.BlockSpec((1, d), lambda i, j: (0, 0)),
            pl.BlockSpec((d, tn), lambda i, j: (0, j)),
        ],
        out_specs=pl.BlockSpec((tm, tn), lambda i, j: (i, j)),
        out_shape=jax.ShapeDtypeStruct((t, n), F32),
        scratch_shapes=[pltpu.VMEM((tm, d), BF16)],
        compiler_params=_params(("parallel", "arbitrary"), vmem),
        name="ln_inproj",
    )(x2, g, b, w)


def _mem_kv_kernel(m_ref, g_ref, b_ref, w_ref, o_ref):
    mn = _layer_norm(m_ref[0], g_ref[...], b_ref[...]).astype(BF16)
    o_ref[0] = _dot(mn, w_ref[...]).astype(BF16)


def _mem_kv(mem, g, b, w):
    bsz, m, d = mem.shape
    n = w.shape[1]
    vmem = 2 * (m * d * 4 + d * n * 2 + m * n * 2) + 4 * m * d * 4
    return pl.pallas_call(
        _mem_kv_kernel,
        grid=(bsz,),
        in_specs=[
            pl.BlockSpec((1, m, d), lambda i: (i, 0, 0)),
            pl.BlockSpec((1, d), lambda i: (0, 0)),
            pl.BlockSpec((1, d), lambda i: (0, 0)),
            pl.BlockSpec((d, n), lambda i: (0, 0)),
        ],
        out_specs=pl.BlockSpec((1, m, n), lambda i: (i, 0, 0)),
        out_shape=jax.ShapeDtypeStruct((bsz, m, n), BF16),
        compiler_params=_params(("parallel",), vmem),
        name="mem_kv",
    )(mem, g, b, w)


def _fourier_tables(s, c, p):
    q = s // p
    ki = np.arange(p, dtype=np.int64)
    i = np.arange(p, dtype=np.int64)
    j = np.arange(q, dtype=np.int64)
    ph = (ki[None, :, None] * (q * i[None, None, :] + j[:, None, None])) % s
    ang = 2.0 * np.pi * ph.astype(np.float64) / s
    m1 = np.concatenate([np.cos(ang), -np.sin(ang)], axis=1)
    kj = np.arange(q, dtype=np.int64)
    a2 = 2.0 * np.pi * ((kj[:, None] * j[None, :]) % q).astype(np.float64) / q
    m2 = np.block([[np.cos(a2), np.sin(a2)], [-np.sin(a2), np.cos(a2)]])
    cc = np.arange(c, dtype=np.int64)
    a3 = 2.0 * np.pi * ((cc[:, None] * cc[None, :]) % c).astype(np.float64) / c
    mc = np.concatenate([np.cos(a3), np.sin(a3)], axis=0) / math.sqrt(s * c)
    return (jnp.asarray(m1, dtype=BF16), jnp.asarray(m2, dtype=BF16), jnp.asarray(mc, dtype=BF16))


def _fourier_kernel(x_ref, m1_ref, m2_ref, mc_ref, o_ref, ys_ref, *, p, q):
    def stage1(j, carry):
        xj = x_ref[pl.ds(j, p, stride=q), :].astype(BF16)
        y = _dot(m1_ref[j], xj)
        ys_ref[pl.ds(j, p, stride=2 * q), :] = y[:p]
        ys_ref[pl.ds(q + j, p, stride=2 * q), :] = y[p:]
        return carry

    lax.fori_loop(0, q, stage1, 0)

    def stage2(ki, carry):
        start = pl.multiple_of(ki * (2 * q), 2 * q)
        y = ys_ref[pl.ds(start, 2 * q), :].astype(BF16)
        z = _dot(m2_ref[...], y)
        zc = jnp.concatenate([z[:q], z[q:]], axis=1).astype(BF16)
        o_ref[pl.ds(ki, q, stride=p), :] = _dot(zc, mc_ref[...])
        return carry

    lax.fori_loop(0, p, stage2, 0)


def _fourier(proj, bsz, s, col_block0):
    c = FOURIER_GROUP_DIM
    groups = FOURIER_W // c
    p = min(FOURIER_P, s // 8)
    q = s // p
    m1, m2, mc = _fourier_tables(s, c, p)
    vmem = 2 * (s * c * 4 + m1.size * 2 + m2.size * 2 + mc.size * 2 + s * c * 4) + 2 * s * c * 4 + 8 * p * c * 4
    return pl.pallas_call(
        functools.partial(_fourier_kernel, p=p, q=q),
        grid=(bsz, groups),
        in_specs=[
            pl.BlockSpec((s, c), lambda b, g: (b, col_block0 + g)),
            pl.BlockSpec((q, 2 * p, p), lambda b, g: (0, 0, 0)),
            pl.BlockSpec((2 * q, 2 * q), lambda b, g: (0, 0)),
            pl.BlockSpec((2 * c, c), lambda b, g: (0, 0)),
        ],
        out_specs=pl.BlockSpec((s, c), lambda b, g: (b, g)),
        out_shape=jax.ShapeDtypeStruct((bsz * s, FOURIER_W), F32),
        scratch_shapes=[pltpu.VMEM((2 * s, c), F32)],
        compiler_params=_params(("parallel", "parallel"), vmem),
        name="fourier_mix",
    )(proj, m1, m2, mc)


def _hgrn_kernel(*refs, reverse, finalize, n_chunks):
    if finalize:
        q_ref, i_ref, z_ref, lb_ref, of_ref, g_ref, ng_ref, o_ref, st_ref = refs
    else:
        q_ref, i_ref, z_ref, lb_ref, o_ref, st_ref = refs

    @pl.when(pl.program_id(1) == 0)
    def _():
        st_ref[...] = jnp.zeros_like(st_ref)

    ck = HGRN_CHUNK
    hd = HGRN_HEAD_DIM
    row = lax.broadcasted_iota(jnp.int32, (ck, ck), 0)
    col = lax.broadcasted_iota(jnp.int32, (ck, ck), 1)
    tri = (col >= row) if reverse else (col <= row)
    cum = tri.astype(F32)
    lb = lb_ref[...]

    order = range(n_chunks - 1, -1, -1) if reverse else range(n_chunks)
    for c in order:
        rows = pl.ds(c * ck, ck)
        q = _silu(q_ref[rows, :])
        v = i_ref[rows, :]
        f = lb + (1.0 - lb) * jax.nn.sigmoid(z_ref[rows, :])
        k = 1.0 - f
        a = jnp.dot(cum, jnp.log(f), preferred_element_type=F32, precision=lax.Precision.HIGHEST)
        a_end = a[0:1, :] if reverse else a[ck - 1:ck, :]
        q_dec = (q * jnp.exp(a)).astype(BF16)
        k_inv = (k * jnp.exp(-a)).astype(BF16)
        k_end = (k * jnp.exp(a_end - a)).astype(BF16)
        decay = jnp.exp(a_end)
        vb = v.astype(BF16)
        outs = []
        for h in range(HGRN_HEADS):
            sl = slice(h * hd, (h + 1) * hd)
            sc = jnp.where(tri, _dot_nt(q_dec[:, sl], k_inv[:, sl]), 0.0)
            st = st_ref[h]
            o_h = _dot(sc.astype(BF16), vb[:, sl]) + _dot_nt(q_dec[:, sl], st.astype(BF16))
            st_ref[h] = st * decay[:, sl] + _dot_tn(vb[:, sl], k_end[:, sl])
            if finalize:
                o_h = o_h + of_ref[rows, sl]
                ms = jnp.mean(o_h * o_h, axis=-1, keepdims=True)
                o_h = o_h * lax.rsqrt(ms + LN_EPS) * ng_ref[:, sl]
            outs.append(o_h)
        o = jnp.concatenate(outs, axis=1)
        if finalize:
            o_ref[rows, :] = (o * _silu(g_ref[rows, :])).astype(o_ref.dtype)
        else:
            o_ref[rows, :] = o


def _hgrn_sweep(proj, lb, cols, bsz, s, *, reverse, o_fwd=None, norm_g=None):
    w = HGRN_W
    rb = min(HGRN_ROWS, s)
    nb = s // rb
    finalize = o_fwd is not None

    def rmap(b, n):
        return b * nb + (nb - 1 - n if reverse else n)

    def cmap(cb):
        return lambda b, n: (rmap(b, n), cb)

    in_specs = [pl.BlockSpec((rb, w), cmap(cols[0])), pl.BlockSpec((rb, w), cmap(cols[1])),
                pl.BlockSpec((rb, w), cmap(cols[2])), pl.BlockSpec((1, w), lambda b, n: (0, 0))]
    args = [proj, proj, proj, lb]
    if finalize:
        in_specs += [pl.BlockSpec((rb, w), cmap(0)), pl.BlockSpec((rb, w), cmap(cols[3])),
                     pl.BlockSpec((1, w), lambda b, n: (0, 0))]
        args += [o_fwd, proj, norm_g]
    vmem = 2 * (6 * rb * w * 4) + HGRN_HEADS * HGRN_HEAD_DIM * HGRN_HEAD_DIM * 4 + 24 * HGRN_CHUNK * w * 4
    return pl.pallas_call(
        functools.partial(_hgrn_kernel, reverse=reverse, finalize=finalize, n_chunks=rb // HGRN_CHUNK),
        grid=(bsz, nb),
        in_specs=in_specs,
        out_specs=pl.BlockSpec((rb, w), cmap(0)),
        out_shape=jax.ShapeDtypeStruct((bsz * s, w), BF16 if finalize else F32),
        scratch_shapes=[pltpu.VMEM((HGRN_HEADS, HGRN_HEAD_DIM, HGRN_HEAD_DIM), F32)],
        compiler_params=_params(("parallel", "arbitrary"), vmem),
        name="hgrn_bwd" if reverse else "hgrn_fwd",
    )(*args)


def _merge_kernel(x_ref, lg_ref, lbi_ref, gl_ref, xq_ref, fm_ref, ho_ref, kv_ref, wx_ref, wf_ref, bf_ref,
                  wh_ref, bg_ref, wo_ref, g1_ref, b1_ref, o_ref):
    d = x_ref.shape[1]
    h = _layer_norm(x_ref[...], lg_ref[...], lbi_ref[...])

    kv = kv_ref[0]
    xq = xq_ref[...].astype(BF16)
    heads = []
    for hh in range(XATTN_HEADS):
        sl = slice(hh * XATTN_HEAD_DIM, (hh + 1) * XATTN_HEAD_DIM)
        vsl = slice(XATTN_W + hh * XATTN_HEAD_DIM, XATTN_W + (hh + 1) * XATTN_HEAD_DIM)
        sc = _dot_nt(xq[:, sl], kv[:, sl]) * (XATTN_HEAD_DIM ** -0.5)
        sc = sc - jnp.max(sc, axis=-1, keepdims=True)
        e = jnp.exp(sc)
        pr = e / jnp.sum(e, axis=-1, keepdims=True)
        heads.append(_dot(pr.astype(BF16), kv[:, vsl]))
    att = jnp.concatenate(heads, axis=1).astype(BF16)

    y_x = _dot(att, wx_ref[...])
    y_f = _dot(fm_ref[...].astype(BF16), wf_ref[...]) + bf_ref[...]
    y_h = _dot(ho_ref[...], wh_ref[...])
    gate = jax.nn.sigmoid(gl_ref[...] + bg_ref[...])
    merged = gate[:, 0:d] * y_f + gate[:, d:2 * d] * y_h + gate[:, 2 * d:3 * d] * y_x
    y = _dot(merged.astype(BF16), wo_ref[...])
    o_ref[...] = _layer_norm(DEEPNORM_ALPHA * h + y, g1_ref[...], b1_ref[...])


def _merge(x2, ln_g, ln_b, proj, gate_cb, xq_cb, fm, ho, kv, w_xo, w_fo, b_fo, w_ho, b_gate, w_out, g1, b1, s):
    t, d = x2.shape
    tm = min(MERGE_TM, s)
    per_b = s // tm
    m, kvw = kv.shape[1], kv.shape[2]
    gw = N_BRANCHES * d

    def full(shape):
        return pl.BlockSpec(shape, lambda i: tuple(0 for _ in shape))

    vmem = (2 * (tm * d * 4 + tm * gw * 4 + tm * XATTN_W * 4 + tm * FOURIER_W * 4 + tm * HGRN_W * 2 + m * kvw * 2
                 + 3 * XATTN_W * d * 2 + d * d * 2 + tm * d * 4) + 10 * tm * d * 4 + 2 * tm * gw * 4)
    return pl.pallas_call(
        _merge_kernel,
        grid=(t // tm,),
        in_specs=[
            pl.BlockSpec((tm, d), lambda i: (i, 0)),
            full((1, d)), full((1, d)),
            pl.BlockSpec((tm, gw), lambda i: (i, gate_cb)),
            pl.BlockSpec((tm, XATTN_W), lambda i: (i, xq_cb)),
            pl.BlockSpec((tm, FOURIER_W), lambda i: (i, 0)),
            pl.BlockSpec((tm, HGRN_W), lambda i: (i, 0)),
            pl.BlockSpec((1, m, kvw), lambda i: (i // per_b, 0, 0)),
            full((XATTN_W, d)), full((FOURIER_W, d)), full((1, d)), full((HGRN_W, d)), full((1, gw)),
            full((d, d)), full((1, d)), full((1, d)),
        ],
        out_specs=pl.BlockSpec((tm, d), lambda i: (i, 0)),
        out_shape=jax.ShapeDtypeStruct((t, d), F32),
        compiler_params=_params(("parallel",), vmem),
        name="merge_out",
    )(x2, ln_g, ln_b, proj, proj, fm, ho, kv, w_xo, w_fo, b_fo, w_ho, b_gate, w_out, g1, b1)


def _route_kernel(h_ref, wr_ref, bias_ref, idx_ref, gw_ref, rank_ref, cnt_ref, carry_ref):
    tm = h_ref.shape[0]
    neg = -jnp.inf

    @pl.when(pl.program_id(0) == 0)
    def _():
        carry_ref[...] = jnp.zeros_like(carry_ref)

    aff = jax.nn.sigmoid(_dot_nt(wr_ref[...], h_ref[...].astype(BF16)))
    sel = aff + bias_ref[...]

    giota = lax.broadcasted_iota(jnp.int32, (GROUP_SIZE, tm), 0)
    scores = []
    for g in range(N_GROUPS):
        slab = sel[g * GROUP_SIZE:(g + 1) * GROUP_SIZE, :]
        m1 = jnp.max(slab, axis=0, keepdims=True)
        first = jnp.min(jnp.where(slab == m1, giota, GROUP_SIZE), axis=0, keepdims=True)
        m2 = jnp.max(jnp.where(giota == first, neg, slab), axis=0, keepdims=True)
        scores.append(m1 + m2)
    gs = jnp.concatenate(scores, axis=0)

    grow = lax.broadcasted_iota(jnp.int32, (N_GROUPS, tm), 0)
    gsel = jnp.zeros((N_GROUPS, tm), jnp.bool_)
    work = gs
    for _ in range(TOPK_GROUPS):
        m = jnp.max(work, axis=0, keepdims=True)
        first = jnp.min(jnp.where(work == m, grow, N_GROUPS), axis=0, keepdims=True)
        hit = grow == first
        gsel = jnp.logical_or(gsel, hit)
        work = jnp.where(hit, neg, work)

    masked = jnp.concatenate(
        [jnp.where(gsel[g:g + 1, :], sel[g * GROUP_SIZE:(g + 1) * GROUP_SIZE, :], neg) for g in range(N_GROUPS)],
        axis=0)

    erow = lax.broadcasted_iota(jnp.int32, (N_EXPERTS, tm), 0)
    ids, ws = [], []
    chosen = jnp.zeros((N_EXPERTS, tm), jnp.bool_)
    for _ in range(TOP_K):
        m = jnp.max(masked, axis=0, keepdims=True)
        first = jnp.min(jnp.where(masked == m, erow, N_EXPERTS), axis=0, keepdims=True)
        hit = erow == first
        ids.append(first)
        ws.append(jnp.sum(jnp.where(hit, aff, 0.0), axis=0, keepdims=True))
        masked = jnp.where(hit, neg, masked)
        chosen = jnp.logical_or(chosen, hit)
    w = jnp.concatenate(ws, axis=0)
    idx_ref[...] = jnp.concatenate(ids, axis=0)
    gw_ref[...] = w / jnp.sum(w, axis=0, keepdims=True) * ROUTED_SCALE

    chosen_f = jnp.where(chosen, 1.0, 0.0)
    srow = lax.broadcasted_iota(jnp.int32, (tm, tm), 0)
    scol = lax.broadcasted_iota(jnp.int32, (tm, tm), 1)
    before = jnp.where(srow < scol, 1.0, 0.0).astype(BF16)
    prefix = _dot(chosen_f.astype(BF16), before) + carry_ref[...]
    ranks = [jnp.sum(jnp.where(erow == ids[k], prefix, 0.0), axis=0, keepdims=True) for k in range(TOP_K)]
    rank_ref[...] = jnp.concatenate(ranks, axis=0).astype(jnp.int32)
    carry_ref[...] = carry_ref[...] + jnp.sum(chosen_f, axis=1, keepdims=True)
    cnt_ref[...] = carry_ref[...]


def _route(h1, w_router_t, bias_col):
    t, d = h1.shape
    tm = min(ROUTE_TM, t)
    vmem = 2 * (tm * d * 4 + N_EXPERTS * d * 2 + 3 * TOP_K * tm * 4) + tm * d * 2 + 16 * N_EXPERTS * tm * 4 + tm * tm * 8
    return pl.pallas_call(
        _route_kernel,
        grid=(t // tm,),
        in_specs=[
            pl.BlockSpec((tm, d), lambda i: (i, 0)),
            pl.BlockSpec((N_EXPERTS, d), lambda i: (0, 0)),
            pl.BlockSpec((N_EXPERTS, 1), lambda i: (0, 0)),
        ],
        out_specs=[pl.BlockSpec((TOP_K, tm), lambda i: (0, i)), pl.BlockSpec((TOP_K, tm), lambda i: (0, i)),
                   pl.BlockSpec((TOP_K, tm), lambda i: (0, i)), pl.BlockSpec((N_EXPERTS, 1), lambda i: (0, 0))],
        out_shape=[jax.ShapeDtypeStruct((TOP_K, t), jnp.int32), jax.ShapeDtypeStruct((TOP_K, t), F32),
                   jax.ShapeDtypeStruct((TOP_K, t), jnp.int32), jax.ShapeDtypeStruct((N_EXPERTS, 1), F32)],
        scratch_shapes=[pltpu.VMEM((N_EXPERTS, 1), F32)],
        compiler_params=_params(("arbitrary",), vmem),
        name="router",
    )(h1, w_router_t, bias_col)


def _dest_kernel(idx_ref, rank_ref, ps_ref, o_ref):
    tm = idx_ref.shape[1]
    erow = lax.broadcasted_iota(jnp.int32, (N_EXPERTS, tm), 0)
    ps = ps_ref[...]
    rows = []
    for k in range(TOP_K):
        base = jnp.sum(jnp.where(erow == idx_ref[k:k + 1, :], ps, 0.0), axis=0, keepdims=True)
        rows.append(base.astype(jnp.int32) + rank_ref[k:k + 1, :])
    o_ref[...] = jnp.concatenate(rows, axis=0)


def _slot_dest(idx_t, rank_t, pstart_col):
    t = idx_t.shape[1]
    tm = min(ROUTE_TM, t)
    vmem = 2 * (3 * TOP_K * tm * 4) + 6 * N_EXPERTS * tm * 4
    return pl.pallas_call(
        _dest_kernel,
        grid=(t // tm,),
        in_specs=[pl.BlockSpec((TOP_K, tm), lambda i: (0, i)), pl.BlockSpec((TOP_K, tm), lambda i: (0, i)),
                  pl.BlockSpec((N_EXPERTS, 1), lambda i: (0, 0))],
        out_specs=pl.BlockSpec((TOP_K, tm), lambda i: (0, i)),
        out_shape=jax.ShapeDtypeStruct((TOP_K, t), jnp.int32),
        compiler_params=_params(("parallel",), vmem),
        name="slot_dest",
    )(idx_t, rank_t, pstart_col)


LANES = 128


def _to_tiles(tile_ref, value, base=0):
    n, d = value.shape
    sub = d // LANES
    for c in range(sub):
        tile_ref[pl.ds(base + c, n, stride=sub), :] = value[:, c * LANES:(c + 1) * LANES]


def _from_tiles(tile_ref, n, sub, base=0):
    return jnp.concatenate([tile_ref[pl.ds(base + c, n, stride=sub), :] for c in range(sub)], axis=1)


def _dispatch_kernel(pend_ref, padded_ref, nu_ref, dest_ref, h_ref, xs_ref, tile_ref, zero_ref, sem, zsem):
    tm, d = h_ref.shape
    sub = d // LANES
    zrows = zero_ref.shape[0]
    n_blocks = xs_ref.shape[0] // zrows

    @pl.when(pl.program_id(0) == 0)
    def _():
        zero_ref[...] = jnp.zeros_like(zero_ref)

        def zero_block(first_row):
            return pltpu.make_async_copy(zero_ref, xs_ref.at[pl.ds(pl.multiple_of(first_row, 8), zrows), :], zsem)

        def issue(e, carry):
            @pl.when(padded_ref[e] > 0)
            def _():
                zero_block(pend_ref[e] * sub - zrows).start()
            return carry

        def drain(e, carry):
            @pl.when(padded_ref[e] > 0)
            def _():
                zero_block(pend_ref[e] * sub - zrows).wait()
            return carry

        def issue_tail(blk, carry):
            zero_block(blk * zrows).start()
            return carry

        def drain_tail(blk, carry):
            zero_block(blk * zrows).wait()
            return carry

        lax.fori_loop(0, N_EXPERTS, issue, 0)
        lax.fori_loop(nu_ref[0], n_blocks, issue_tail, 0)
        lax.fori_loop(0, N_EXPERTS, drain, 0)
        lax.fori_loop(nu_ref[0], n_blocks, drain_tail, 0)

    _to_tiles(tile_ref, h_ref[...])

    def row_copy(tok, dst_row):
        return pltpu.make_async_copy(tile_ref.at[pl.ds(pl.multiple_of(tok * sub, 8), sub), :],
                                     xs_ref.at[pl.ds(pl.multiple_of(dst_row * sub, 8), sub), :], sem)

    def issue(tok, carry):
        for k in range(TOP_K):
            row_copy(tok, dest_ref[tok * TOP_K + k]).start()
        return carry

    lax.fori_loop(0, tm, issue, 0)

    for _ in range(TOP_K):
        pltpu.make_async_copy(tile_ref, xs_ref.at[pl.ds(0, tm * sub), :], sem).wait()


def _dispatch(h1, dest_flat, pend, padded, n_used, n_rows_pad):
    t, d = h1.shape
    sub = d // LANES
    tm = min(DISPATCH_TM, t)
    vmem = 2 * (tm * d * 4) + tm * d * 4 + MOE_BM * d * 4 + 2 * tm * d * 4
    return pl.pallas_call(
        _dispatch_kernel,
        grid_spec=pltpu.PrefetchScalarGridSpec(
            num_scalar_prefetch=3,
            grid=(t // tm,),
            in_specs=[
                pl.BlockSpec((tm * TOP_K,), lambda i, pe, pa, nu: (i,), memory_space=pltpu.SMEM),
                pl.BlockSpec((tm, d), lambda i, pe, pa, nu: (i, 0)),
            ],
            out_specs=pl.BlockSpec(memory_space=pl.ANY),
            scratch_shapes=[pltpu.VMEM((tm * sub, LANES), F32), pltpu.VMEM((MOE_BM * sub, LANES), F32),
                            pltpu.SemaphoreType.DMA(()), pltpu.SemaphoreType.DMA(())],
        ),
        out_shape=jax.ShapeDtypeStruct((n_rows_pad * sub, LANES), F32),
        compiler_params=_params(("arbitrary",), vmem),
        name="dispatch_rows",
    )(pend, padded, n_used, dest_flat, h1)


def _expert_kernel(be_ref, nu_ref, x_ref, wg_ref, wu_ref, wd_ref, o_ref, wgb_ref, wub_ref, wdb_ref):
    i = pl.program_id(0)
    bm = x_ref.shape[0] * LANES // wg_ref.shape[1]
    sub = wg_ref.shape[1] // LANES

    @pl.when(i < nu_ref[0])
    def _():
        changed = jnp.logical_or(i == 0, be_ref[i] != be_ref[jnp.maximum(i - 1, 0)])

        @pl.when(changed)
        def _():
            wgb_ref[...] = wg_ref[0].astype(BF16)
            wub_ref[...] = wu_ref[0].astype(BF16)
            wdb_ref[...] = wd_ref[0].astype(BF16)

        xb = _from_tiles(x_ref, bm, sub).astype(BF16)
        hb = (_silu(_dot(xb, wgb_ref[...])) * _dot(xb, wub_ref[...])).astype(BF16)
        _to_tiles(o_ref, _dot(hb, wdb_ref[...]))

    @pl.when(i >= nu_ref[0])
    def _():
        o_ref[...] = jnp.zeros_like(o_ref)


def _experts(x_sorted, block_exp, n_used, w_gate, w_up, w_down):
    d, ff = w_gate.shape[1], w_gate.shape[2]
    sub = d // LANES
    bm = MOE_BM
    n_blocks = x_sorted.shape[0] // (bm * sub)
    vmem = 2 * (2 * bm * d * 4 + 3 * d * ff * 4) + 3 * d * ff * 2 + 6 * bm * ff * 4 + 3 * bm * d * 4

    def used(i, nu):
        return jnp.minimum(i, nu[0] - 1)

    return pl.pallas_call(
        _expert_kernel,
        grid_spec=pltpu.PrefetchScalarGridSpec(
            num_scalar_prefetch=2,
            grid=(n_blocks,),
            in_specs=[
                pl.BlockSpec((bm * sub, LANES), lambda i, be, nu: (used(i, nu), 0)),
                pl.BlockSpec((1, d, ff), lambda i, be, nu: (be[used(i, nu)], 0, 0)),
                pl.BlockSpec((1, d, ff), lambda i, be, nu: (be[used(i, nu)], 0, 0)),
                pl.BlockSpec((1, ff, d), lambda i, be, nu: (be[used(i, nu)], 0, 0)),
            ],
            out_specs=pl.BlockSpec((bm * sub, LANES), lambda i, be, nu: (i, 0)),
            scratch_shapes=[pltpu.VMEM((d, ff), BF16), pltpu.VMEM((d, ff), BF16), pltpu.VMEM((ff, d), BF16)],
        ),
        out_shape=jax.ShapeDtypeStruct((n_blocks * bm * sub, LANES), F32),
        compiler_params=_params(("arbitrary",), vmem),
        name="expert_mlp",
    )(block_exp, n_used, x_sorted, w_gate, w_up, w_down)


def _final_kernel(dest_ref, h_ref, gw_ref, y_ref, wg_ref, wu_ref, wd_ref, g_ref, b_ref, o_ref, rows_ref, sem):
    tm, d = h_ref.shape
    sub = d // LANES

    def row_copy(tok, k, src_row):
        return pltpu.make_async_copy(y_ref.at[pl.ds(pl.multiple_of(src_row * sub, 8), sub), :],
                                     rows_ref.at[pl.ds(pl.multiple_of((k * tm + tok) * sub, 8), sub), :], sem)

    def issue(tok, carry):
        for k in range(TOP_K):
            row_copy(tok, k, dest_ref[tok * TOP_K + k]).start()
        return carry

    lax.fori_loop(0, tm, issue, 0)

    h = h_ref[...]
    hb = h.astype(BF16)
    shared = _dot((_silu(_dot(hb, wg_ref[...])) * _dot(hb, wu_ref[...])).astype(BF16), wd_ref[...])

    pltpu.make_async_copy(y_ref.at[pl.ds(0, TOP_K * tm * sub), :], rows_ref, sem).wait()

    gw = gw_ref[...]
    routed = gw[:, 0:1] * _from_tiles(rows_ref, tm, sub)
    for k in range(1, TOP_K):
        routed = routed + gw[:, k:k + 1] * _from_tiles(rows_ref, tm, sub, base=k * tm * sub)
    o_ref[...] = _layer_norm(DEEPNORM_ALPHA * h + (routed + shared), g_ref[...], b_ref[...])


def _final(h1, dest_flat, gw, y_sorted, w_sg, w_su, w_sd, g2, b2):
    t, d = h1.shape
    ff = w_sg.shape[1]
    sub = d // LANES
    tm = min(FINAL_TM, t)
    vmem = 2 * (2 * tm * d * 4 + tm * LANES * 4 + 3 * d * ff * 2) + TOP_K * tm * d * 4 + 8 * tm * d * 4

    def full(shape):
        return pl.BlockSpec(shape, lambda i: tuple(0 for _ in shape))

    return pl.pallas_call(
        _final_kernel,
        grid=(t // tm,),
        in_specs=[
            pl.BlockSpec((tm * TOP_K,), lambda i: (i,), memory_space=pltpu.SMEM),
            pl.BlockSpec((tm, d), lambda i: (i, 0)),
            pl.BlockSpec((tm, TOP_K), lambda i: (i, 0)),
            pl.BlockSpec(memory_space=pl.ANY),
            full((d, ff)), full((d, ff)), full((ff, d)), full((1, d)), full((1, d)),
        ],
        out_specs=pl.BlockSpec((tm, d), lambda i: (i, 0)),
        out_shape=jax.ShapeDtypeStruct((t, d), F32),
        scratch_shapes=[pltpu.VMEM((TOP_K * tm * sub, LANES), F32), pltpu.SemaphoreType.DMA(())],
        compiler_params=_params(("arbitrary",), vmem),
        name="combine_final",
    )(dest_flat, h1, gw, y_sorted, w_sg, w_su, w_sd, g2, b2)


def _lower_bound(p):
    return jnp.cumsum(jax.nn.softmax(p.astype(F32), axis=0), axis=0)[0:1]


def _block_plan(counts, n_rows):
    bm = MOE_BM
    counts = counts.reshape(-1).astype(jnp.int32)
    padded = (counts + bm - 1) // bm * bm
    pend = jnp.cumsum(padded).astype(jnp.int32)
    pstart = pend - padded
    n_blocks = n_rows // bm + N_EXPERTS
    n_used = pend[-1:] // bm
    block_start = jnp.arange(n_blocks, dtype=jnp.int32) * bm
    block_exp = jnp.minimum(jnp.searchsorted(pend, block_start, side='right'), N_EXPERTS - 1).astype(jnp.int32)
    block_exp = jnp.where(block_start < pend[-1], block_exp, block_exp[jnp.maximum(n_used[0] - 1, 0)])
    return pstart, pend, padded, block_exp, n_used.astype(jnp.int32), n_blocks


def kernel(x, mem, ln_in_g, ln_in_b, ln_mem_g, ln_mem_b, hgrn_lb_fwd, hgrn_lb_bwd, w_in, b_gate, hgrn_norm_g, w_mem_kv, w_fourier_o, b_fourier_o, w_hgrn_o, w_xattn_o, w_out, ln1_g, ln1_b, w_router, router_bias, w_exp_gate, w_exp_up, w_exp_down, w_sh_gate, w_sh_up, w_sh_down, ln2_g, ln2_b):
    bsz, s, d = x.shape
    t = bsz * s
    l = 0
    row = lambda v: v.reshape(1, -1).astype(F32)

    n_gate = N_BRANCHES * d
    n_rest = w_in.shape[2] - n_gate
    w_in_r = jnp.concatenate([w_in[l][:, n_rest:], w_in[l][:, :n_rest]], axis=1).astype(BF16)
    gate_blocks = n_gate // INPROJ_TN
    cb_fourier = gate_blocks
    cb_hq, cb_hi, cb_zf, cb_zb, cb_hg, cb_xq = (gate_blocks + 1 + n for n in range(6))

    x2 = x.reshape(t, d)
    proj = _inproj(x2, row(ln_in_g), row(ln_in_b), w_in_r)
    kv = _mem_kv(mem, row(ln_mem_g), row(ln_mem_b), w_mem_kv[l].astype(BF16))

    fm = _fourier(proj, bsz, s, cb_fourier * (INPROJ_TN // FOURIER_GROUP_DIM))

    lb_f = _lower_bound(hgrn_lb_fwd)
    lb_b = _lower_bound(hgrn_lb_bwd)
    o_fwd = _hgrn_sweep(proj, lb_f, (cb_hq, cb_hi, cb_zf), bsz, s, reverse=False)
    ho = _hgrn_sweep(proj, lb_b, (cb_hq, cb_hi, cb_zb, cb_hg), bsz, s, reverse=True,
                     o_fwd=o_fwd, norm_g=row(hgrn_norm_g[l]))

    h1 = _merge(x2, row(ln_in_g), row(ln_in_b), proj, 0, cb_xq, fm, ho, kv,
                w_xattn_o[l].astype(BF16), w_fourier_o[l].astype(BF16), row(b_fourier_o[l]),
                w_hgrn_o[l].astype(BF16), row(b_gate[l]), w_out[l].astype(BF16), row(ln1_g[l]), row(ln1_b[l]), s)

    idx_t, gw_t, rank_t, counts = _route(h1, w_router[l].T.astype(BF16),
                                         router_bias[l].reshape(N_EXPERTS, 1).astype(F32))
    pstart, pend, padded, block_exp, n_used, n_blocks = _block_plan(counts, t * TOP_K)
    dest_t = _slot_dest(idx_t, rank_t, pstart.astype(F32).reshape(N_EXPERTS, 1))
    dest_flat = dest_t.T.reshape(-1)

    x_sorted = _dispatch(h1, dest_flat, pend, padded, n_used, n_blocks * MOE_BM)
    y_sorted = _experts(x_sorted, block_exp, n_used, w_exp_gate[l], w_exp_up[l], w_exp_down[l])
    out = _final(h1, dest_flat, gw_t.T, y_sorted, w_sh_gate[l].astype(BF16), w_sh_up[l].astype(BF16),
                 w_sh_down[l].astype(BF16), row(ln2_g[l]), row(ln2_b[l]))
    return out.reshape(bsz, s, d)
```

```python
import functools
import math

import numpy as np
import jax
import jax.numpy as jnp
from jax import lax
from jax.experimental import pallas as pl
from jax.experimental.pallas import tpu as pltpu

F32 = jnp.float32
BF16 = jnp.bfloat16

LN_EPS = 1e-5
DEPTH = 1
DEEPNORM_ALPHA = (2 * DEPTH) ** 0.25
FOURIER_GROUP_DIM = 128
FOURIER_W = 512
HGRN_HEADS = 4
HGRN_HEAD_DIM = 128
HGRN_W = HGRN_HEADS * HGRN_HEAD_DIM
HGRN_CHUNK = 64
XATTN_HEADS = 4
XATTN_HEAD_DIM = 128
XATTN_W = XATTN_HEADS * XATTN_HEAD_DIM
N_BRANCHES = 3
N_EXPERTS = 256
TOP_K = 8
N_GROUPS = 8
TOPK_GROUPS = 4
GROUP_SIZE = N_EXPERTS // N_GROUPS
ROUTED_SCALE = 2.5

V7X_SCOPED_VMEM_CAP_BYTES = 60000 * 1024

INPROJ_TM = 1024
INPROJ_TN = 512
FOURIER_P = 128
FOURIER_UNROLL = 8
HGRN_ROWS = 512
MERGE_TM = 256
ROUTE_TM = 512
MOE_BM = 256
DISPATCH_TM = 256
FINAL_TM = 256


def _params(semantics, vmem_bytes):
    return pltpu.CompilerParams(dimension_semantics=semantics,
                                vmem_limit_bytes=int(min(vmem_bytes, V7X_SCOPED_VMEM_CAP_BYTES)))


def _layer_norm(x, g, b):
    mu = jnp.mean(x, axis=-1, keepdims=True)
    xc = x - mu
    var = jnp.mean(xc * xc, axis=-1, keepdims=True)
    return xc * lax.rsqrt(var + LN_EPS) * g + b


def _silu(x):
    return x * jax.nn.sigmoid(x)


def _dot(a, b):
    return jnp.dot(a, b, preferred_element_type=F32)


def _dot_nt(a, b):
    return lax.dot_general(a, b, (((1,), (1,)), ((), ())), preferred_element_type=F32)


def _dot_tn(a, b):
    return lax.dot_general(a, b, (((0,), (0,)), ((), ())), preferred_element_type=F32)


def _inproj_kernel(x_ref, g_ref, b_ref, w_ref, o_ref, hn_ref):
    @pl.when(pl.program_id(1) == 0)
    def _():
        hn_ref[...] = _layer_norm(x_ref[...], g_ref[...], b_ref[...]).astype(BF16)

    o_ref[...] = _dot(hn_ref[...], w_ref[...])


def _inproj(x2, g, b, w):
    t, d = x2.shape
    n = w.shape[1]
    tm, tn = min(INPROJ_TM, t), INPROJ_TN
    vmem = 2 * (tm * d * 4 + d * tn * 2 + tm * tn * 4) + tm * d * 2 + 3 * tm * d * 4
    return pl.pallas_call(
        _inproj_kernel,
        grid=(t // tm, n // tn),
        in_specs=[
            pl.BlockSpec((tm, d), lambda i, j: (i, 0)),
            pl.BlockSpec((1, d), lambda i, j: (0, 0)),
            pl.BlockSpec((1, d), lambda i, j: (0, 0)),
            pl.BlockSpec((d, tn), lambda i, j: (0, j)),
        ],
        out_specs=pl.BlockSpec((tm, tn), lambda i, j: (i, j)),
        out_shape=jax.ShapeDtypeStruct((t, n), F32),
        scratch_shapes=[pltpu.VMEM((tm, d), BF16)],
        compiler_params=_params(("parallel", "arbitrary"), vmem),
        name="ln_inproj",
    )(x2, g, b, w)


def _mem_kv_kernel(m_ref, g_ref, b_ref, w_ref, o_ref):
    mn = _layer_norm(m_ref[0], g_ref[...], b_ref[...]).astype(BF16)
    o_ref[0] = _dot(mn, w_ref[...]).astype(BF16)


def _mem_kv(mem, g, b, w):
    bsz, m, d = mem.shape
    n = w.shape[1]
    vmem = 2 * (m * d * 4 + d * n * 2 + m * n * 2) + 4 * m * d * 4
    return pl.pallas_call(
        _mem_kv_kernel,
        grid=(bsz,),
        in_specs=[
            pl.BlockSpec((1, m, d), lambda i: (i, 0, 0)),
            pl.BlockSpec((1, d), lambda i: (0, 0)),
            pl.BlockSpec((1, d), lambda i: (0, 0)),
            pl.BlockSpec((d, n), lambda i: (0, 0)),
        ],
        out_specs=pl.BlockSpec((1, m, n), lambda i: (i, 0, 0)),
        out_shape=jax.ShapeDtypeStruct((bsz, m, n), BF16),
        compiler_params=_params(("parallel",), vmem),
        name="mem_kv",
    )(mem, g, b, w)


def _fourier_tables(s, c, p):
    q = s // p
    ki = np.arange(p, dtype=np.int64)
    i = np.arange(p, dtype=np.int64)
    j = np.arange(q, dtype=np.int64)
    ph = (ki[None, :, None] * (q * i[None, None, :] + j[:, None, None])) % s
    ang = 2.0 * np.pi * ph.astype(np.float64) / s
    m1 = np.concatenate([np.cos(ang), -np.sin(ang)], axis=1)
    kj = np.arange(q, dtype=np.int64)
    a2 = 2.0 * np.pi * ((kj[:, None] * j[None, :]) % q).astype(np.float64) / q
    m2 = np.block([[np.cos(a2), np.sin(a2)], [-np.sin(a2), np.cos(a2)]])
    cc = np.arange(c, dtype=np.int64)
    a3 = 2.0 * np.pi * ((cc[:, None] * cc[None, :]) % c).astype(np.float64) / c
    mc = np.concatenate([np.cos(a3), np.sin(a3)], axis=0) / math.sqrt(s * c)
    return (jnp.asarray(m1, dtype=BF16), jnp.asarray(m2, dtype=BF16), jnp.asarray(mc, dtype=BF16))


def _fourier_kernel(x_ref, m1_ref, m2_ref, mc_ref, o_ref, ys_ref, *, p, q):
    def stage1(j, carry):
        xj = x_ref[pl.ds(j, p, stride=q), :].astype(BF16)
        y = _dot(m1_ref[j], xj)
        ys_ref[pl.ds(j, p, stride=2 * q), :] = y[:p]
        ys_ref[pl.ds(q + j, p, stride=2 * q), :] = y[p:]
        return carry

    lax.fori_loop(0, q, stage1, 0, unroll=FOURIER_UNROLL)

    def stage2(ki, carry):
        start = pl.multiple_of(ki * (2 * q), 2 * q)
        y = ys_ref[pl.ds(start, 2 * q), :].astype(BF16)
        z = _dot(m2_ref[...], y)
        zc = jnp.concatenate([z[:q], z[q:]], axis=1).astype(BF16)
        o_ref[pl.ds(ki, q, stride=p), :] = _dot(zc, mc_ref[...])
        return carry

    lax.fori_loop(0, p, stage2, 0, unroll=FOURIER_UNROLL)


def _fourier(proj, bsz, s, col_block0):
    c = FOURIER_GROUP_DIM
    groups = FOURIER_W // c
    p = min(FOURIER_P, s // 8)
    q = s // p
    m1, m2, mc = _fourier_tables(s, c, p)
    vmem = 2 * (s * c * 4 + m1.size * 2 + m2.size * 2 + mc.size * 2 + s * c * 4) + 2 * s * c * 4 + 8 * p * c * 4
    return pl.pallas_call(
        functools.partial(_fourier_kernel, p=p, q=q),
        grid=(bsz, groups),
        in_specs=[
            pl.BlockSpec((s, c), lambda b, g: (b, col_block0 + g)),
            pl.BlockSpec((q, 2 * p, p), lambda b, g: (0, 0, 0)),
            pl.BlockSpec((2 * q, 2 * q), lambda b, g: (0, 0)),
            pl.BlockSpec((2 * c, c), lambda b, g: (0, 0)),
        ],
        out_specs=pl.BlockSpec((s, c), lambda b, g: (b, g)),
        out_shape=jax.ShapeDtypeStruct((bsz * s, FOURIER_W), F32),
        scratch_shapes=[pltpu.VMEM((2 * s, c), F32)],
        compiler_params=_params(("parallel", "parallel"), vmem),
        name="fourier_mix",
    )(proj, m1, m2, mc)


def _hgrn_kernel(*refs, reverse, finalize, n_chunks):
    if finalize:
        q_ref, i_ref, z_ref, lb_ref, of_ref, g_ref, ng_ref, o_ref, st_ref = refs
    else:
        q_ref, i_ref, z_ref, lb_ref, o_ref, st_ref = refs

    @pl.when(pl.program_id(1) == 0)
    def _():
        st_ref[...] = jnp.zeros_like(st_ref)

    ck = HGRN_CHUNK
    hd = HGRN_HEAD_DIM
    row = lax.broadcasted_iota(jnp.int32, (ck, ck), 0)
    col = lax.broadcasted_iota(jnp.int32, (ck, ck), 1)
    tri = (col >= row) if reverse else (col <= row)
    cum = tri.astype(F32)
    lb = lb_ref[...]

    order = range(n_chunks - 1, -1, -1) if reverse else range(n_chunks)
    for c in order:
        rows = pl.ds(c * ck, ck)
        q = _silu(q_ref[rows, :])
        v = i_ref[rows, :]
        f = lb + (1.0 - lb) * jax.nn.sigmoid(z_ref[rows, :])
        k = 1.0 - f
        a = jnp.dot(cum, jnp.log(f), preferred_element_type=F32, precision=lax.Precision.HIGHEST)
        a_end = a[0:1, :] if reverse else a[ck - 1:ck, :]
        q_dec = (q * jnp.exp(a)).astype(BF16)
        k_inv = (k * jnp.exp(-a)).astype(BF16)
        k_end = (k * jnp.exp(a_end - a)).astype(BF16)
        decay = jnp.exp(a_end)
        vb = v.astype(BF16)
        outs = []
        for h in range(HGRN_HEADS):
            sl = slice(h * hd, (h + 1) * hd)
            sc = jnp.where(tri, _dot_nt(q_dec[:, sl], k_inv[:, sl]), 0.0)
            st = st_ref[h]
            o_h = _dot(sc.astype(BF16), vb[:, sl]) + _dot_nt(q_dec[:, sl], st.astype(BF16))
            st_ref[h] = st * decay[:, sl] + _dot_tn(vb[:, sl], k_end[:, sl])
            if finalize:
                o_h = o_h + of_ref[rows, sl]
                ms = jnp.mean(o_h * o_h, axis=-1, keepdims=True)
                o_h = o_h * lax.rsqrt(ms + LN_EPS) * ng_ref[:, sl]
            outs.append(o_h)
        o = jnp.concatenate(outs, axis=1)
        if finalize:
            o_ref[rows, :] = (o * _silu(g_ref[rows, :])).astype(o_ref.dtype)
        else:
            o_ref[rows, :] = o


def _hgrn_sweep(proj, lb, cols, bsz, s, *, reverse, o_fwd=None, norm_g=None):
    w = HGRN_W
    rb = min(HGRN_ROWS, s)
    nb = s // rb
    finalize = o_fwd is not None

    def rmap(b, n):
        return b * nb + (nb - 1 - n if reverse else n)

    def cmap(cb):
        return lambda b, n: (rmap(b, n), cb)

    in_specs = [pl.BlockSpec((rb, w), cmap(cols[0])), pl.BlockSpec((rb, w), cmap(cols[1])),
                pl.BlockSpec((rb, w), cmap(cols[2])), pl.BlockSpec((1, w), lambda b, n: (0, 0))]
    args = [proj, proj, proj, lb]
    if finalize:
        in_specs += [pl.BlockSpec((rb, w), cmap(0)), pl.BlockSpec((rb, w), cmap(cols[3])),
                     pl.BlockSpec((1, w), lambda b, n: (0, 0))]
        args += [o_fwd, proj, norm_g]
    vmem = 2 * (6 * rb * w * 4) + HGRN_HEADS * HGRN_HEAD_DIM * HGRN_HEAD_DIM * 4 + 24 * HGRN_CHUNK * w * 4
    return pl.pallas_call(
        functools.partial(_hgrn_kernel, reverse=reverse, finalize=finalize, n_chunks=rb // HGRN_CHUNK),
        grid=(bsz, nb),
        in_specs=in_specs,
        out_specs=pl.BlockSpec((rb, w), cmap(0)),
        out_shape=jax.ShapeDtypeStruct((bsz * s, w), BF16 if finalize else F32),
        scratch_shapes=[pltpu.VMEM((HGRN_HEADS, HGRN_HEAD_DIM, HGRN_HEAD_DIM), F32)],
        compiler_params=_params(("parallel", "arbitrary"), vmem),
        name="hgrn_bwd" if reverse else "hgrn_fwd",
    )(*args)


def _merge_kernel(x_ref, lg_ref, lbi_ref, gl_ref, xq_ref, fm_ref, ho_ref, kv_ref, wx_ref, wf_ref, bf_ref,
                  wh_ref, bg_ref, wo_ref, g1_ref, b1_ref, o_ref):
    d = x_ref.shape[1]
    h = _layer_norm(x_ref[...], lg_ref[...], lbi_ref[...])

    kv = kv_ref[0]
    xq = xq_ref[...].astype(BF16)
    heads = []
    for hh in range(XATTN_HEADS):
        sl = slice(hh * XATTN_HEAD_DIM, (hh + 1) * XATTN_HEAD_DIM)
        vsl = slice(XATTN_W + hh * XATTN_HEAD_DIM, XATTN_W + (hh + 1) * XATTN_HEAD_DIM)
        sc = _dot_nt(xq[:, sl], kv[:, sl]) * (XATTN_HEAD_DIM ** -0.5)
        sc = sc - jnp.max(sc, axis=-1, keepdims=True)
        e = jnp.exp(sc)
        pr = e / jnp.sum(e, axis=-1, keepdims=True)
        heads.append(_dot(pr.astype(BF16), kv[:, vsl]))
    att = jnp.concatenate(heads, axis=1).astype(BF16)

    y_x = _dot(att, wx_ref[...])
    y_f = _dot(fm_ref[...].astype(BF16), wf_ref[...]) + bf_ref[...]
    y_h = _dot(ho_ref[...], wh_ref[...])
    gate = jax.nn.sigmoid(gl_ref[...] + bg_ref[...])
    merged = gate[:, 0:d] * y_f + gate[:, d:2 * d] * y_h + gate[:, 2 * d:3 * d] * y_x
    y = _dot(merged.astype(BF16), wo_ref[...])
    o_ref[...] = _layer_norm(DEEPNORM_ALPHA * h + y, g1_ref[...], b1_ref[...])


def _merge(x2, ln_g, ln_b, proj, gate_cb, xq_cb, fm, ho, kv, w_xo, w_fo, b_fo, w_ho, b_gate, w_out, g1, b1, s):
    t, d = x2.shape
    tm = min(MERGE_TM, s)
    per_b = s // tm
    m, kvw = kv.shape[1], kv.shape[2]
    gw = N_BRANCHES * d

    def full(shape):
        return pl.BlockSpec(shape, lambda i: tuple(0 for _ in shape))

    vmem = (2 * (tm * d * 4 + tm * gw * 4 + tm * XATTN_W * 4 + tm * FOURIER_W * 4 + tm * HGRN_W * 2 + m * kvw * 2
                 + 3 * XATTN_W * d * 2 + d * d * 2 + tm * d * 4) + 10 * tm * d * 4 + 2 * tm * gw * 4)
    return pl.pallas_call(
        _merge_kernel,
        grid=(t // tm,),
        in_specs=[
            pl.BlockSpec((tm, d), lambda i: (i, 0)),
            full((1, d)), full((1, d)),
            pl.BlockSpec((tm, gw), lambda i: (i, gate_cb)),
            pl.BlockSpec((tm, XATTN_W), lambda i: (i, xq_cb)),
            pl.BlockSpec((tm, FOURIER_W), lambda i: (i, 0)),
            pl.BlockSpec((tm, HGRN_W), lambda i: (i, 0)),
            pl.BlockSpec((1, m, kvw), lambda i: (i // per_b, 0, 0)),
            full((XATTN_W, d)), full((FOURIER_W, d)), full((1, d)), full((HGRN_W, d)), full((1, gw)),
            full((d, d)), full((1, d)), full((1, d)),
        ],
        out_specs=pl.BlockSpec((tm, d), lambda i: (i, 0)),
        out_shape=jax.ShapeDtypeStruct((t, d), F32),
        compiler_params=_params(("parallel",), vmem),
        name="merge_out",
    )(x2, ln_g, ln_b, proj, proj, fm, ho, kv, w_xo, w_fo, b_fo, w_ho, b_gate, w_out, g1, b1)


def _route_kernel(h_ref, wr_ref, bias_ref, idx_ref, gw_ref, rank_ref, cnt_ref, carry_ref):
    tm = h_ref.shape[0]
    neg = -jnp.inf

    @pl.when(pl.program_id(0) == 0)
    def _():
        carry_ref[...] = jnp.zeros_like(carry_ref)

    aff = jax.nn.sigmoid(_dot_nt(wr_ref[...], h_ref[...].astype(BF16)))
    sel = aff + bias_ref[...]

    giota = lax.broadcasted_iota(jnp.int32, (GROUP_SIZE, tm), 0)
    scores = []
    for g in range(N_GROUPS):
        slab = sel[g * GROUP_SIZE:(g + 1) * GROUP_SIZE, :]
        m1 = jnp.max(slab, axis=0, keepdims=True)
        first = jnp.min(jnp.where(slab == m1, giota, GROUP_SIZE), axis=0, keepdims=True)
        m2 = jnp.max(jnp.where(giota == first, neg, slab), axis=0, keepdims=True)
        scores.append(m1 + m2)
    gs = jnp.concatenate(scores, axis=0)

    grow = lax.broadcasted_iota(jnp.int32, (N_GROUPS, tm), 0)
    gsel = jnp.zeros((N_GROUPS, tm), jnp.bool_)
    work = gs
    for _ in range(TOPK_GROUPS):
        m = jnp.max(work, axis=0, keepdims=True)
        first = jnp.min(jnp.where(work == m, grow, N_GROUPS), axis=0, keepdims=True)
        hit = grow == first
        gsel = jnp.logical_or(gsel, hit)
        work = jnp.where(hit, neg, work)

    masked = jnp.concatenate(
        [jnp.where(gsel[g:g + 1, :], sel[g * GROUP_SIZE:(g + 1) * GROUP_SIZE, :], neg) for g in range(N_GROUPS)],
        axis=0)

    erow = lax.broadcasted_iota(jnp.int32, (N_EXPERTS, tm), 0)
    ids, ws = [], []
    chosen = jnp.zeros((N_EXPERTS, tm), jnp.bool_)
    for _ in range(TOP_K):
        m = jnp.max(masked, axis=0, keepdims=True)
        first = jnp.min(jnp.where(masked == m, erow, N_EXPERTS), axis=0, keepdims=True)
        hit = erow == first
        ids.append(first)
        ws.append(jnp.sum(jnp.where(hit, aff, 0.0), axis=0, keepdims=True))
        masked = jnp.where(hit, neg, masked)
        chosen = jnp.logical_or(chosen, hit)
    w = jnp.concatenate(ws, axis=0)
    idx_ref[...] = jnp.concatenate(ids, axis=0)
    gw_ref[...] = w / jnp.sum(w, axis=0, keepdims=True) * ROUTED_SCALE

    chosen_f = jnp.where(chosen, 1.0, 0.0)
    srow = lax.broadcasted_iota(jnp.int32, (tm, tm), 0)
    scol = lax.broadcasted_iota(jnp.int32, (tm, tm), 1)
    before = jnp.where(srow < scol, 1.0, 0.0).astype(BF16)
    prefix = _dot(chosen_f.astype(BF16), before) + carry_ref[...]
    ranks = [jnp.sum(jnp.where(erow == ids[k], prefix, 0.0), axis=0, keepdims=True) for k in range(TOP_K)]
    rank_ref[...] = jnp.concatenate(ranks, axis=0).astype(jnp.int32)
    carry_ref[...] = carry_ref[...] + jnp.sum(chosen_f, axis=1, keepdims=True)
    cnt_ref[...] = carry_ref[...]


def _route(h1, w_router_t, bias_col):
    t, d = h1.shape
    tm = min(ROUTE_TM, t)
    vmem = 2 * (tm * d * 4 + N_EXPERTS * d * 2 + 3 * TOP_K * tm * 4) + tm * d * 2 + 16 * N_EXPERTS * tm * 4 + tm * tm * 8
    return pl.pallas_call(
        _route_kernel,
        grid=(t // tm,),
        in_specs=[
            pl.BlockSpec((tm, d), lambda i: (i, 0)),
            pl.BlockSpec((N_EXPERTS, d), lambda i: (0, 0)),
            pl.BlockSpec((N_EXPERTS, 1), lambda i: (0, 0)),
        ],
        out_specs=[pl.BlockSpec((TOP_K, tm), lambda i: (0, i)), pl.BlockSpec((TOP_K, tm), lambda i: (0, i)),
                   pl.BlockSpec((TOP_K, tm), lambda i: (0, i)), pl.BlockSpec((N_EXPERTS, 1), lambda i: (0, 0))],
        out_shape=[jax.ShapeDtypeStruct((TOP_K, t), jnp.int32), jax.ShapeDtypeStruct((TOP_K, t), F32),
                   jax.ShapeDtypeStruct((TOP_K, t), jnp.int32), jax.ShapeDtypeStruct((N_EXPERTS, 1), F32)],
        scratch_shapes=[pltpu.VMEM((N_EXPERTS, 1), F32)],
        compiler_params=_params(("arbitrary",), vmem),
        name="router",
    )(h1, w_router_t, bias_col)


def _dest_kernel(idx_ref, rank_ref, ps_ref, o_ref):
    tm = idx_ref.shape[1]
    erow = lax.broadcasted_iota(jnp.int32, (N_EXPERTS, tm), 0)
    ps = ps_ref[...]
    rows = []
    for k in range(TOP_K):
        base = jnp.sum(jnp.where(erow == idx_ref[k:k + 1, :], ps, 0.0), axis=0, keepdims=True)
        rows.append(base.astype(jnp.int32) + rank_ref[k:k + 1, :])
    o_ref[...] = jnp.concatenate(rows, axis=0)


def _slot_dest(idx_t, rank_t, pstart_col):
    t = idx_t.shape[1]
    tm = min(ROUTE_TM, t)
    vmem = 2 * (3 * TOP_K * tm * 4) + 6 * N_EXPERTS * tm * 4
    return pl.pallas_call(
        _dest_kernel,
        grid=(t // tm,),
        in_specs=[pl.BlockSpec((TOP_K, tm), lambda i: (0, i)), pl.BlockSpec((TOP_K, tm), lambda i: (0, i)),
                  pl.BlockSpec((N_EXPERTS, 1), lambda i: (0, 0))],
        out_specs=pl.BlockSpec((TOP_K, tm), lambda i: (0, i)),
        out_shape=jax.ShapeDtypeStruct((TOP_K, t), jnp.int32),
        compiler_params=_params(("parallel",), vmem),
        name="slot_dest",
    )(idx_t, rank_t, pstart_col)


LANES = 128
U32 = jnp.uint32
HIGH_HALF = np.uint32(0xFFFF0000)
N_DMA_PRIORITIES = 2


def _tile_rows(d):
    return d // (2 * LANES)


def _pack_tiles(tile_ref, value, base=0):
    n, d = value.shape
    sub = _tile_rows(d)
    bits = lax.bitcast_convert_type(value.astype(BF16).astype(F32), U32)
    for c in range(sub):
        low = bits[:, c * LANES:(c + 1) * LANES] >> 16
        high = bits[:, (c + sub) * LANES:(c + sub + 1) * LANES] & HIGH_HALF
        tile_ref[pl.ds(base + c, n, stride=sub), :] = low | high


def _unpack_tiles(tile_ref, n, d, base=0):
    sub = _tile_rows(d)
    words = [tile_ref[pl.ds(base + c, n, stride=sub), :] for c in range(sub)]
    lows = [lax.bitcast_convert_type(u << 16, F32) for u in words]
    highs = [lax.bitcast_convert_type(u & HIGH_HALF, F32) for u in words]
    return jnp.concatenate(lows + highs, axis=1)


def _dispatch_kernel(pend_ref, padded_ref, nu_ref, dest_ref, h_ref, xs_ref, tile_ref, zero_ref, sem, zsem):
    tm, d = h_ref.shape
    sub = _tile_rows(d)
    zrows = zero_ref.shape[0]
    n_blocks = xs_ref.shape[0] // zrows

    @pl.when(pl.program_id(0) == 0)
    def _():
        zero_ref[...] = jnp.zeros_like(zero_ref)

        def zero_block(first_row):
            return pltpu.make_async_copy(zero_ref, xs_ref.at[pl.ds(pl.multiple_of(first_row, 8), zrows), :], zsem)

        def issue(e, carry):
            @pl.when(padded_ref[e] > 0)
            def _():
                zero_block(pend_ref[e] * sub - zrows).start()
            return carry

        def drain(e, carry):
            @pl.when(padded_ref[e] > 0)
            def _():
                zero_block(pend_ref[e] * sub - zrows).wait()
            return carry

        def issue_tail(blk, carry):
            zero_block(blk * zrows).start()
            return carry

        def drain_tail(blk, carry):
            zero_block(blk * zrows).wait()
            return carry

        lax.fori_loop(0, N_EXPERTS, issue, 0)
        lax.fori_loop(nu_ref[0], n_blocks, issue_tail, 0)
        lax.fori_loop(0, N_EXPERTS, drain, 0)
        lax.fori_loop(nu_ref[0], n_blocks, drain_tail, 0)

    _pack_tiles(tile_ref, h_ref[...])

    def row_copy(tok, dst_row):
        return pltpu.make_async_copy(tile_ref.at[pl.ds(pl.multiple_of(tok * sub, sub), sub), :],
                                     xs_ref.at[pl.ds(pl.multiple_of(dst_row * sub, sub), sub), :], sem)

    def issue(tok, carry):
        for k in range(TOP_K):
            row_copy(tok, dest_ref[tok * TOP_K + k]).start(priority=k % N_DMA_PRIORITIES)
        return carry

    lax.fori_loop(0, tm, issue, 0)

    for _ in range(TOP_K):
        pltpu.make_async_copy(tile_ref, xs_ref.at[pl.ds(0, tm * sub), :], sem).wait()


def _dispatch(h1, dest_flat, pend, padded, n_used, n_rows_pad):
    t, d = h1.shape
    sub = _tile_rows(d)
    tm = min(DISPATCH_TM, t)
    vmem = 2 * (tm * d * 4) + tm * d * 2 + MOE_BM * d * 2 + 4 * tm * d * 4
    return pl.pallas_call(
        _dispatch_kernel,
        grid_spec=pltpu.PrefetchScalarGridSpec(
            num_scalar_prefetch=3,
            grid=(t // tm,),
            in_specs=[
                pl.BlockSpec((tm * TOP_K,), lambda i, pe, pa, nu: (i,), memory_space=pltpu.SMEM),
                pl.BlockSpec((tm, d), lambda i, pe, pa, nu: (i, 0)),
            ],
            out_specs=pl.BlockSpec(memory_space=pl.ANY),
            scratch_shapes=[pltpu.VMEM((tm * sub, LANES), U32), pltpu.VMEM((MOE_BM * sub, LANES), U32),
                            pltpu.SemaphoreType.DMA(()), pltpu.SemaphoreType.DMA(())],
        ),
        out_shape=jax.ShapeDtypeStruct((n_rows_pad * sub, LANES), U32),
        compiler_params=_params(("arbitrary",), vmem),
        name="dispatch_rows",
    )(pend, padded, n_used, dest_flat, h1)


def _expert_kernel(be_ref, nu_ref, x_ref, wg_ref, wu_ref, wd_ref, o_ref, wgb_ref, wub_ref, wdb_ref):
    i = pl.program_id(0)
    d = wg_ref.shape[1]
    bm = x_ref.shape[0] // _tile_rows(d)

    @pl.when(i < nu_ref[0])
    def _():
        changed = jnp.logical_or(i == 0, be_ref[i] != be_ref[jnp.maximum(i - 1, 0)])

        @pl.when(changed)
        def _():
            wgb_ref[...] = wg_ref[0].astype(BF16)
            wub_ref[...] = wu_ref[0].astype(BF16)
            wdb_ref[...] = wd_ref[0].astype(BF16)

        xb = _unpack_tiles(x_ref, bm, d).astype(BF16)
        hb = (_silu(_dot(xb, wgb_ref[...])) * _dot(xb, wub_ref[...])).astype(BF16)
        _pack_tiles(o_ref, _dot(hb, wdb_ref[...]))

    @pl.when(i >= nu_ref[0])
    def _():
        o_ref[...] = jnp.zeros_like(o_ref)


def _experts(x_sorted, block_exp, n_used, w_gate, w_up, w_down):
    d, ff = w_gate.shape[1], w_gate.shape[2]
    sub = _tile_rows(d)
    bm = MOE_BM
    n_blocks = x_sorted.shape[0] // (bm * sub)
    vmem = 2 * (2 * bm * d * 2 + 3 * d * ff * 4) + 3 * d * ff * 2 + 6 * bm * ff * 4 + 5 * bm * d * 4

    def used(i, nu):
        return jnp.minimum(i, nu[0] - 1)

    return pl.pallas_call(
        _expert_kernel,
        grid_spec=pltpu.PrefetchScalarGridSpec(
            num_scalar_prefetch=2,
            grid=(n_blocks,),
            in_specs=[
                pl.BlockSpec((bm * sub, LANES), lambda i, be, nu: (used(i, nu), 0)),
                pl.BlockSpec((1, d, ff), lambda i, be, nu: (be[used(i, nu)], 0, 0)),
                pl.BlockSpec((1, d, ff), lambda i, be, nu: (be[used(i, nu)], 0, 0)),
                pl.BlockSpec((1, ff, d), lambda i, be, nu: (be[used(i, nu)], 0, 0)),
            ],
            out_specs=pl.BlockSpec((bm * sub, LANES), lambda i, be, nu: (i, 0)),
            scratch_shapes=[pltpu.VMEM((d, ff), BF16), pltpu.VMEM((d, ff), BF16), pltpu.VMEM((ff, d), BF16)],
        ),
        out_shape=jax.ShapeDtypeStruct((n_blocks * bm * sub, LANES), U32),
        compiler_params=_params(("arbitrary",), vmem),
        name="expert_mlp",
    )(block_exp, n_used, x_sorted, w_gate, w_up, w_down)


def _final_kernel(dest_ref, h_ref, gw_ref, y_ref, wg_ref, wu_ref, wd_ref, g_ref, b_ref, o_ref, rows_ref, sem):
    tm, d = h_ref.shape
    sub = _tile_rows(d)

    def row_copy(tok, k, src_row):
        return pltpu.make_async_copy(y_ref.at[pl.ds(pl.multiple_of(src_row * sub, sub), sub), :],
                                     rows_ref.at[pl.ds(pl.multiple_of((k * tm + tok) * sub, sub), sub), :], sem)

    def issue(tok, carry):
        for k in range(TOP_K):
            row_copy(tok, k, dest_ref[tok * TOP_K + k]).start(priority=k % N_DMA_PRIORITIES)
        return carry

    lax.fori_loop(0, tm, issue, 0)

    h = h_ref[...]
    hb = h.astype(BF16)
    shared = _dot((_silu(_dot(hb, wg_ref[...])) * _dot(hb, wu_ref[...])).astype(BF16), wd_ref[...])

    pltpu.make_async_copy(y_ref.at[pl.ds(0, TOP_K * tm * sub), :], rows_ref, sem).wait()

    gw = gw_ref[...]
    routed = gw[:, 0:1] * _unpack_tiles(rows_ref, tm, d)
    for k in range(1, TOP_K):
        routed = routed + gw[:, k:k + 1] * _unpack_tiles(rows_ref, tm, d, base=k * tm * sub)
    o_ref[...] = _layer_norm(DEEPNORM_ALPHA * h + (routed + shared), g_ref[...], b_ref[...])


def _final(h1, dest_flat, gw, y_sorted, w_sg, w_su, w_sd, g2, b2):
    t, d = h1.shape
    ff = w_sg.shape[1]
    sub = _tile_rows(d)
    tm = min(FINAL_TM, t)
    vmem = 2 * (2 * tm * d * 4 + tm * LANES * 4 + 3 * d * ff * 2) + TOP_K * tm * d * 2 + 8 * tm * d * 4

    def full(shape):
        return pl.BlockSpec(shape, lambda i: tuple(0 for _ in shape))

    return pl.pallas_call(
        _final_kernel,
        grid=(t // tm,),
        in_specs=[
            pl.BlockSpec((tm * TOP_K,), lambda i: (i,), memory_space=pltpu.SMEM),
            pl.BlockSpec((tm, d), lambda i: (i, 0)),
            pl.BlockSpec((tm, TOP_K), lambda i: (i, 0)),
            pl.BlockSpec(memory_space=pl.ANY),
            full((d, ff)), full((d, ff)), full((ff, d)), full((1, d)), full((1, d)),
        ],
        out_specs=pl.BlockSpec((tm, d), lambda i: (i, 0)),
        out_shape=jax.ShapeDtypeStruct((t, d), F32),
        scratch_shapes=[pltpu.VMEM((TOP_K * tm * sub, LANES), U32), pltpu.SemaphoreType.DMA(())],
        compiler_params=_params(("arbitrary",), vmem),
        name="combine_final",
    )(dest_flat, h1, gw, y_sorted, w_sg, w_su, w_sd, g2, b2)


def _lower_bound(p):
    return jnp.cumsum(jax.nn.softmax(p.astype(F32), axis=0), axis=0)[0:1]


def _block_plan(counts, n_rows):
    bm = MOE_BM
    counts = counts.reshape(-1).astype(jnp.int32)
    padded = (counts + bm - 1) // bm * bm
    pend = jnp.cumsum(padded).astype(jnp.int32)
    pstart = pend - padded
    n_blocks = n_rows // bm + N_EXPERTS
    n_used = pend[-1:] // bm
    block_start = jnp.arange(n_blocks, dtype=jnp.int32) * bm
    block_exp = jnp.sum((pend[None, :] <= block_start[:, None]).astype(jnp.int32), axis=1)
    last_used = jnp.max(jnp.where(padded > 0, jnp.arange(N_EXPERTS, dtype=jnp.int32), 0))
    block_exp = jnp.where(block_start < pend[-1], block_exp, last_used).astype(jnp.int32)
    return pstart, pend, padded, block_exp, n_used.astype(jnp.int32), n_blocks


def kernel(x, mem, ln_in_g, ln_in_b, ln_mem_g, ln_mem_b, hgrn_lb_fwd, hgrn_lb_bwd, w_in, b_gate, hgrn_norm_g, w_mem_kv, w_fourier_o, b_fourier_o, w_hgrn_o, w_xattn_o, w_out, ln1_g, ln1_b, w_router, router_bias, w_exp_gate, w_exp_up, w_exp_down, w_sh_gate, w_sh_up, w_sh_down, ln2_g, ln2_b):
    bsz, s, d = x.shape
    t = bsz * s
    l = 0
    row = lambda v: v.reshape(1, -1).astype(F32)

    n_gate = N_BRANCHES * d
    n_rest = w_in.shape[2] - n_gate
    w_in_r = jnp.concatenate([w_in[l][:, n_rest:], w_in[l][:, :n_rest]], axis=1).astype(BF16)
    gate_blocks = n_gate // INPROJ_TN
    cb_fourier = gate_blocks
    cb_hq, cb_hi, cb_zf, cb_zb, cb_hg, cb_xq = (gate_blocks + 1 + n for n in range(6))

    x2 = x.reshape(t, d)
    proj = _inproj(x2, row(ln_in_g), row(ln_in_b), w_in_r)
    kv = _mem_kv(mem, row(ln_mem_g), row(ln_mem_b), w_mem_kv[l].astype(BF16))

    fm = _fourier(proj, bsz, s, cb_fourier * (INPROJ_TN // FOURIER_GROUP_DIM))

    lb_f = _lower_bound(hgrn_lb_fwd)
    lb_b = _lower_bound(hgrn_lb_bwd)
    o_fwd = _hgrn_sweep(proj, lb_f, (cb_hq, cb_hi, cb_zf), bsz, s, reverse=False)
    ho = _hgrn_sweep(proj, lb_b, (cb_hq, cb_hi, cb_zb, cb_hg), bsz, s, reverse=True,
                     o_fwd=o_fwd, norm_g=row(hgrn_norm_g[l]))

    h1 = _merge(x2, row(ln_in_g), row(ln_in_b), proj, 0, cb_xq, fm, ho, kv,
                w_xattn_o[l].astype(BF16), w_fourier_o[l].astype(BF16), row(b_fourier_o[l]),
                w_hgrn_o[l].astype(BF16), row(b_gate[l]), w_out[l].astype(BF16), row(ln1_g[l]), row(ln1_b[l]), s)

    idx_t, gw_t, rank_t, counts = _route(h1, w_router[l].T.astype(BF16),
                                         router_bias[l].reshape(N_EXPERTS, 1).astype(F32))
    pstart, pend, padded, block_exp, n_used, n_blocks = _block_plan(counts, t * TOP_K)
    dest_t = _slot_dest(idx_t, rank_t, pstart.astype(F32).reshape(N_EXPERTS, 1))
    dest_flat = dest_t.T.reshape(-1)

    x_sorted = _dispatch(h1, dest_flat, pend, padded, n_used, n_blocks * MOE_BM)
    y_sorted = _experts(x_sorted, block_exp, n_used, w_exp_gate[l], w_exp_up[l], w_exp_down[l])
    out = _final(h1, dest_flat, gw_t.T, y_sorted, w_sh_gate[l].astype(BF16), w_sh_up[l].astype(BF16),
                 w_sh_down[l].astype(BF16), row(ln2_g[l]), row(ln2_b[l]))
    return out.reshape(bsz, s, d)
```

```python
import functools
import math

import numpy as np
import jax
import jax.numpy as jnp
from jax import lax
from jax.experimental import pallas as pl
from jax.experimental.pallas import tpu as pltpu

F32 = jnp.float32
BF16 = jnp.bfloat16

LN_EPS = 1e-5
DEPTH = 1
DEEPNORM_ALPHA = (2 * DEPTH) ** 0.25
FOURIER_GROUP_DIM = 128
FOURIER_W = 512
HGRN_HEADS = 4
HGRN_HEAD_DIM = 128
HGRN_W = HGRN_HEADS * HGRN_HEAD_DIM
HGRN_CHUNK = 64
XATTN_HEADS = 4
XATTN_HEAD_DIM = 128
XATTN_W = XATTN_HEADS * XATTN_HEAD_DIM
N_BRANCHES = 3
N_EXPERTS = 256
TOP_K = 8
N_GROUPS = 8
TOPK_GROUPS = 4
GROUP_SIZE = N_EXPERTS // N_GROUPS
ROUTED_SCALE = 2.5

V7X_SCOPED_VMEM_CAP_BYTES = 60000 * 1024

INPROJ_TM = 1024
INPROJ_TN = 512
FOURIER_P = 128
FOURIER_UNROLL = 8
HGRN_ROWS = 512
MERGE_TM = 256
ROUTE_TM = 512
MOE_BM = 256
DISPATCH_TM = 256
FINAL_TM = 256
COMBINE_ISSUE_UNROLL = 4


def _params(semantics, vmem_bytes):
    return pltpu.CompilerParams(dimension_semantics=semantics,
                                vmem_limit_bytes=int(min(vmem_bytes, V7X_SCOPED_VMEM_CAP_BYTES)))


def _layer_norm(x, g, b):
    mu = jnp.mean(x, axis=-1, keepdims=True)
    xc = x - mu
    var = jnp.mean(xc * xc, axis=-1, keepdims=True)
    return xc * lax.rsqrt(var + LN_EPS) * g + b


def _silu(x):
    return x * jax.nn.sigmoid(x)


def _dot(a, b):
    return jnp.dot(a, b, preferred_element_type=F32)


def _dot_nt(a, b):
    return lax.dot_general(a, b, (((1,), (1,)), ((), ())), preferred_element_type=F32)


def _dot_tn(a, b):
    return lax.dot_general(a, b, (((0,), (0,)), ((), ())), preferred_element_type=F32)


def _inproj_kernel(x_ref, g_ref, b_ref, w_ref, o_ref, hn_ref):
    @pl.when(pl.program_id(1) == 0)
    def _():
        hn_ref[...] = _layer_norm(x_ref[...], g_ref[...], b_ref[...]).astype(BF16)

    o_ref[...] = _dot(hn_ref[...], w_ref[...])


def _inproj(x2, g, b, w):
    t, d = x2.shape
    n = w.shape[1]
    tm, tn = min(INPROJ_TM, t), INPROJ_TN
    vmem = 2 * (tm * d * 4 + d * tn * 2 + tm * tn * 4) + tm * d * 2 + 3 * tm * d * 4
    return pl.pallas_call(
        _inproj_kernel,
        grid=(t // tm, n // tn),
        in_specs=[
            pl.BlockSpec((tm, d), lambda i, j: (i, 0)),
            pl.BlockSpec((1, d), lambda i, j: (0, 0)),
            pl.BlockSpec((1, d), lambda i, j: (0, 0)),
            pl.BlockSpec((d, tn), lambda i, j: (0, j)),
        ],
        out_specs=pl.BlockSpec((tm, tn), lambda i, j: (i, j)),
        out_shape=jax.ShapeDtypeStruct((t, n), F32),
        scratch_shapes=[pltpu.VMEM((tm, d), BF16)],
        compiler_params=_params(("parallel", "arbitrary"), vmem),
        name="ln_inproj",
    )(x2, g, b, w)


def _mem_kv_kernel(m_ref, g_ref, b_ref, w_ref, o_ref):
    mn = _layer_norm(m_ref[0], g_ref[...], b_ref[...]).astype(BF16)
    o_ref[0] = _dot(mn, w_ref[...]).astype(BF16)


def _mem_kv(mem, g, b, w):
    bsz, m, d = mem.shape
    n = w.shape[1]
    vmem = 2 * (m * d * 4 + d * n * 2 + m * n * 2) + 4 * m * d * 4
    return pl.pallas_call(
        _mem_kv_kernel,
        grid=(bsz,),
        in_specs=[
            pl.BlockSpec((1, m, d), lambda i: (i, 0, 0)),
            pl.BlockSpec((1, d), lambda i: (0, 0)),
            pl.BlockSpec((1, d), lambda i: (0, 0)),
            pl.BlockSpec((d, n), lambda i: (0, 0)),
        ],
        out_specs=pl.BlockSpec((1, m, n), lambda i: (i, 0, 0)),
        out_shape=jax.ShapeDtypeStruct((bsz, m, n), BF16),
        compiler_params=_params(("parallel",), vmem),
        name="mem_kv",
    )(mem, g, b, w)


def _fourier_tables(s, c, p):
    q = s // p
    ki = np.arange(p, dtype=np.int64)
    i = np.arange(p, dtype=np.int64)
    j = np.arange(q, dtype=np.int64)
    ph = (ki[None, :, None] * (q * i[None, None, :] + j[:, None, None])) % s
    ang = 2.0 * np.pi * ph.astype(np.float64) / s
    m1 = np.concatenate([np.cos(ang), -np.sin(ang)], axis=1)
    kj = np.arange(q, dtype=np.int64)
    a2 = 2.0 * np.pi * ((kj[:, None] * j[None, :]) % q).astype(np.float64) / q
    m2 = np.block([[np.cos(a2), np.sin(a2)], [-np.sin(a2), np.cos(a2)]])
    cc = np.arange(c, dtype=np.int64)
    a3 = 2.0 * np.pi * ((cc[:, None] * cc[None, :]) % c).astype(np.float64) / c
    mc = np.concatenate([np.cos(a3), np.sin(a3)], axis=0) / math.sqrt(s * c)
    return (jnp.asarray(m1, dtype=BF16), jnp.asarray(m2, dtype=BF16), jnp.asarray(mc, dtype=BF16))


def _fourier_kernel(x_ref, m1_ref, m2_ref, mc_ref, o_ref, ys_ref, *, p, q):
    def stage1(j, carry):
        xj = x_ref[pl.ds(j, p, stride=q), :].astype(BF16)
        y = _dot(m1_ref[j], xj)
        ys_ref[pl.ds(j, p, stride=2 * q), :] = y[:p]
        ys_ref[pl.ds(q + j, p, stride=2 * q), :] = y[p:]
        return carry

    lax.fori_loop(0, q, stage1, 0, unroll=FOURIER_UNROLL)

    def stage2(ki, carry):
        start = pl.multiple_of(ki * (2 * q), 2 * q)
        y = ys_ref[pl.ds(start, 2 * q), :].astype(BF16)
        z = _dot(m2_ref[...], y)
        zc = jnp.concatenate([z[:q], z[q:]], axis=1).astype(BF16)
        o_ref[pl.ds(ki, q, stride=p), :] = _dot(zc, mc_ref[...])
        return carry

    lax.fori_loop(0, p, stage2, 0, unroll=FOURIER_UNROLL)


def _fourier(proj, bsz, s, col_block0):
    c = FOURIER_GROUP_DIM
    groups = FOURIER_W // c
    p = min(FOURIER_P, s // 8)
    q = s // p
    m1, m2, mc = _fourier_tables(s, c, p)
    vmem = 2 * (s * c * 4 + m1.size * 2 + m2.size * 2 + mc.size * 2 + s * c * 4) + 2 * s * c * 4 + 8 * p * c * 4
    return pl.pallas_call(
        functools.partial(_fourier_kernel, p=p, q=q),
        grid=(bsz, groups),
        in_specs=[
            pl.BlockSpec((s, c), lambda b, g: (b, col_block0 + g)),
            pl.BlockSpec((q, 2 * p, p), lambda b, g: (0, 0, 0)),
            pl.BlockSpec((2 * q, 2 * q), lambda b, g: (0, 0)),
            pl.BlockSpec((2 * c, c), lambda b, g: (0, 0)),
        ],
        out_specs=pl.BlockSpec((s, c), lambda b, g: (b, g)),
        out_shape=jax.ShapeDtypeStruct((bsz * s, FOURIER_W), F32),
        scratch_shapes=[pltpu.VMEM((2 * s, c), F32)],
        compiler_params=_params(("parallel", "parallel"), vmem),
        name="fourier_mix",
    )(proj, m1, m2, mc)


def _hgrn_kernel(*refs, reverse, finalize, n_chunks):
    if finalize:
        q_ref, i_ref, z_ref, lb_ref, of_ref, g_ref, ng_ref, o_ref, st_ref = refs
    else:
        q_ref, i_ref, z_ref, lb_ref, o_ref, st_ref = refs

    @pl.when(pl.program_id(1) == 0)
    def _():
        st_ref[...] = jnp.zeros_like(st_ref)

    ck = HGRN_CHUNK
    hd = HGRN_HEAD_DIM
    row = lax.broadcasted_iota(jnp.int32, (ck, ck), 0)
    col = lax.broadcasted_iota(jnp.int32, (ck, ck), 1)
    tri = (col >= row) if reverse else (col <= row)
    cum = tri.astype(F32)
    lb = lb_ref[...]

    order = range(n_chunks - 1, -1, -1) if reverse else range(n_chunks)
    for c in order:
        rows = pl.ds(c * ck, ck)
        q = _silu(q_ref[rows, :])
        v = i_ref[rows, :]
        f = lb + (1.0 - lb) * jax.nn.sigmoid(z_ref[rows, :])
        k = 1.0 - f
        a = jnp.dot(cum, jnp.log(f), preferred_element_type=F32, precision=lax.Precision.HIGHEST)
        a_end = a[0:1, :] if reverse else a[ck - 1:ck, :]
        q_dec = (q * jnp.exp(a)).astype(BF16)
        k_inv = (k * jnp.exp(-a)).astype(BF16)
        k_end = (k * jnp.exp(a_end - a)).astype(BF16)
        decay = jnp.exp(a_end)
        vb = v.astype(BF16)
        outs = []
        for h in range(HGRN_HEADS):
            sl = slice(h * hd, (h + 1) * hd)
            sc = jnp.where(tri, _dot_nt(q_dec[:, sl], k_inv[:, sl]), 0.0)
            st = st_ref[h]
            o_h = _dot(sc.astype(BF16), vb[:, sl]) + _dot_nt(q_dec[:, sl], st.astype(BF16))
            st_ref[h] = st * decay[:, sl] + _dot_tn(vb[:, sl], k_end[:, sl])
            if finalize:
                o_h = o_h + of_ref[rows, sl]
                ms = jnp.mean(o_h * o_h, axis=-1, keepdims=True)
                o_h = o_h * lax.rsqrt(ms + LN_EPS) * ng_ref[:, sl]
            outs.append(o_h)
        o = jnp.concatenate(outs, axis=1)
        if finalize:
            o_ref[rows, :] = (o * _silu(g_ref[rows, :])).astype(o_ref.dtype)
        else:
            o_ref[rows, :] = o


def _hgrn_sweep(proj, lb, cols, bsz, s, *, reverse, o_fwd=None, norm_g=None):
    w = HGRN_W
    rb = min(HGRN_ROWS, s)
    nb = s // rb
    finalize = o_fwd is not None

    def rmap(b, n):
        return b * nb + (nb - 1 - n if reverse else n)

    def cmap(cb):
        return lambda b, n: (rmap(b, n), cb)

    in_specs = [pl.BlockSpec((rb, w), cmap(cols[0])), pl.BlockSpec((rb, w), cmap(cols[1])),
                pl.BlockSpec((rb, w), cmap(cols[2])), pl.BlockSpec((1, w), lambda b, n: (0, 0))]
    args = [proj, proj, proj, lb]
    if finalize:
        in_specs += [pl.BlockSpec((rb, w), cmap(0)), pl.BlockSpec((rb, w), cmap(cols[3])),
                     pl.BlockSpec((1, w), lambda b, n: (0, 0))]
        args += [o_fwd, proj, norm_g]
    vmem = 2 * (6 * rb * w * 4) + HGRN_HEADS * HGRN_HEAD_DIM * HGRN_HEAD_DIM * 4 + 24 * HGRN_CHUNK * w * 4
    return pl.pallas_call(
        functools.partial(_hgrn_kernel, reverse=reverse, finalize=finalize, n_chunks=rb // HGRN_CHUNK),
        grid=(bsz, nb),
        in_specs=in_specs,
        out_specs=pl.BlockSpec((rb, w), cmap(0)),
        out_shape=jax.ShapeDtypeStruct((bsz * s, w), BF16 if finalize else F32),
        scratch_shapes=[pltpu.VMEM((HGRN_HEADS, HGRN_HEAD_DIM, HGRN_HEAD_DIM), F32)],
        compiler_params=_params(("parallel", "arbitrary"), vmem),
        name="hgrn_bwd" if reverse else "hgrn_fwd",
    )(*args)


def _merge_kernel(x_ref, lg_ref, lbi_ref, gl_ref, xq_ref, fm_ref, ho_ref, kv_ref, wx_ref, wf_ref, bf_ref,
                  wh_ref, bg_ref, wo_ref, g1_ref, b1_ref, o_ref):
    d = x_ref.shape[1]
    h = _layer_norm(x_ref[...], lg_ref[...], lbi_ref[...])

    kv = kv_ref[0]
    xq = xq_ref[...].astype(BF16)
    heads = []
    for hh in range(XATTN_HEADS):
        sl = slice(hh * XATTN_HEAD_DIM, (hh + 1) * XATTN_HEAD_DIM)
        vsl = slice(XATTN_W + hh * XATTN_HEAD_DIM, XATTN_W + (hh + 1) * XATTN_HEAD_DIM)
        sc = _dot_nt(xq[:, sl], kv[:, sl]) * (XATTN_HEAD_DIM ** -0.5)
        sc = sc - jnp.max(sc, axis=-1, keepdims=True)
        e = jnp.exp(sc)
        pr = e / jnp.sum(e, axis=-1, keepdims=True)
        heads.append(_dot(pr.astype(BF16), kv[:, vsl]))
    att = jnp.concatenate(heads, axis=1).astype(BF16)

    y_x = _dot(att, wx_ref[...])
    y_f = _dot(fm_ref[...].astype(BF16), wf_ref[...]) + bf_ref[...]
    y_h = _dot(ho_ref[...], wh_ref[...])
    gate = jax.nn.sigmoid(gl_ref[...] + bg_ref[...])
    merged = gate[:, 0:d] * y_f + gate[:, d:2 * d] * y_h + gate[:, 2 * d:3 * d] * y_x
    y = _dot(merged.astype(BF16), wo_ref[...])
    o_ref[...] = _layer_norm(DEEPNORM_ALPHA * h + y, g1_ref[...], b1_ref[...])


def _merge(x2, ln_g, ln_b, proj, gate_cb, xq_cb, fm, ho, kv, w_xo, w_fo, b_fo, w_ho, b_gate, w_out, g1, b1, s):
    t, d = x2.shape
    tm = min(MERGE_TM, s)
    per_b = s // tm
    m, kvw = kv.shape[1], kv.shape[2]
    gw = N_BRANCHES * d

    def full(shape):
        return pl.BlockSpec(shape, lambda i: tuple(0 for _ in shape))

    vmem = (2 * (tm * d * 4 + tm * gw * 4 + tm * XATTN_W * 4 + tm * FOURIER_W * 4 + tm * HGRN_W * 2 + m * kvw * 2
                 + 3 * XATTN_W * d * 2 + d * d * 2 + tm * d * 4) + 10 * tm * d * 4 + 2 * tm * gw * 4)
    return pl.pallas_call(
        _merge_kernel,
        grid=(t // tm,),
        in_specs=[
            pl.BlockSpec((tm, d), lambda i: (i, 0)),
            full((1, d)), full((1, d)),
            pl.BlockSpec((tm, gw), lambda i: (i, gate_cb)),
            pl.BlockSpec((tm, XATTN_W), lambda i: (i, xq_cb)),
            pl.BlockSpec((tm, FOURIER_W), lambda i: (i, 0)),
            pl.BlockSpec((tm, HGRN_W), lambda i: (i, 0)),
            pl.BlockSpec((1, m, kvw), lambda i: (i // per_b, 0, 0)),
            full((XATTN_W, d)), full((FOURIER_W, d)), full((1, d)), full((HGRN_W, d)), full((1, gw)),
            full((d, d)), full((1, d)), full((1, d)),
        ],
        out_specs=pl.BlockSpec((tm, d), lambda i: (i, 0)),
        out_shape=jax.ShapeDtypeStruct((t, d), F32),
        compiler_params=_params(("parallel",), vmem),
        name="merge_out",
    )(x2, ln_g, ln_b, proj, proj, fm, ho, kv, w_xo, w_fo, b_fo, w_ho, b_gate, w_out, g1, b1)


def _route_kernel(h_ref, wr_ref, bias_ref, idx_ref, gw_ref, rank_ref, cnt_ref, carry_ref):
    tm = h_ref.shape[0]
    neg = -jnp.inf

    @pl.when(pl.program_id(0) == 0)
    def _():
        carry_ref[...] = jnp.zeros_like(carry_ref)

    aff = jax.nn.sigmoid(_dot_nt(wr_ref[...], h_ref[...].astype(BF16)))
    sel = aff + bias_ref[...]

    giota = lax.broadcasted_iota(jnp.int32, (GROUP_SIZE, tm), 0)
    scores = []
    for g in range(N_GROUPS):
        slab = sel[g * GROUP_SIZE:(g + 1) * GROUP_SIZE, :]
        m1 = jnp.max(slab, axis=0, keepdims=True)
        first = jnp.min(jnp.where(slab == m1, giota, GROUP_SIZE), axis=0, keepdims=True)
        m2 = jnp.max(jnp.where(giota == first, neg, slab), axis=0, keepdims=True)
        scores.append(m1 + m2)
    gs = jnp.concatenate(scores, axis=0)

    grow = lax.broadcasted_iota(jnp.int32, (N_GROUPS, tm), 0)
    gsel = jnp.zeros((N_GROUPS, tm), jnp.bool_)
    work = gs
    for _ in range(TOPK_GROUPS):
        m = jnp.max(work, axis=0, keepdims=True)
        first = jnp.min(jnp.where(work == m, grow, N_GROUPS), axis=0, keepdims=True)
        hit = grow == first
        gsel = jnp.logical_or(gsel, hit)
        work = jnp.where(hit, neg, work)

    masked = jnp.concatenate(
        [jnp.where(gsel[g:g + 1, :], sel[g * GROUP_SIZE:(g + 1) * GROUP_SIZE, :], neg) for g in range(N_GROUPS)],
        axis=0)

    erow = lax.broadcasted_iota(jnp.int32, (N_EXPERTS, tm), 0)
    ids, ws = [], []
    chosen = jnp.zeros((N_EXPERTS, tm), jnp.bool_)
    for _ in range(TOP_K):
        m = jnp.max(masked, axis=0, keepdims=True)
        first = jnp.min(jnp.where(masked == m, erow, N_EXPERTS), axis=0, keepdims=True)
        hit = erow == first
        ids.append(first)
        ws.append(jnp.sum(jnp.where(hit, aff, 0.0), axis=0, keepdims=True))
        masked = jnp.where(hit, neg, masked)
        chosen = jnp.logical_or(chosen, hit)
    w = jnp.concatenate(ws, axis=0)
    idx_ref[...] = jnp.concatenate(ids, axis=0)
    gw_ref[...] = w / jnp.sum(w, axis=0, keepdims=True) * ROUTED_SCALE

    chosen_f = jnp.where(chosen, 1.0, 0.0)
    srow = lax.broadcasted_iota(jnp.int32, (tm, tm), 0)
    scol = lax.broadcasted_iota(jnp.int32, (tm, tm), 1)
    before = jnp.where(srow < scol, 1.0, 0.0).astype(BF16)
    prefix = _dot(chosen_f.astype(BF16), before) + carry_ref[...]
    ranks = [jnp.sum(jnp.where(erow == ids[k], prefix, 0.0), axis=0, keepdims=True) for k in range(TOP_K)]
    rank_ref[...] = jnp.concatenate(ranks, axis=0).astype(jnp.int32)
    carry_ref[...] = carry_ref[...] + jnp.sum(chosen_f, axis=1, keepdims=True)
    cnt_ref[...] = carry_ref[...]


def _route(h1, w_router_t, bias_col):
    t, d = h1.shape
    tm = min(ROUTE_TM, t)
    vmem = 2 * (tm * d * 4 + N_EXPERTS * d * 2 + 3 * TOP_K * tm * 4) + tm * d * 2 + 16 * N_EXPERTS * tm * 4 + tm * tm * 8
    return pl.pallas_call(
        _route_kernel,
        grid=(t // tm,),
        in_specs=[
            pl.BlockSpec((tm, d), lambda i: (i, 0)),
            pl.BlockSpec((N_EXPERTS, d), lambda i: (0, 0)),
            pl.BlockSpec((N_EXPERTS, 1), lambda i: (0, 0)),
        ],
        out_specs=[pl.BlockSpec((TOP_K, tm), lambda i: (0, i)), pl.BlockSpec((TOP_K, tm), lambda i: (0, i)),
                   pl.BlockSpec((TOP_K, tm), lambda i: (0, i)), pl.BlockSpec((N_EXPERTS, 1), lambda i: (0, 0))],
        out_shape=[jax.ShapeDtypeStruct((TOP_K, t), jnp.int32), jax.ShapeDtypeStruct((TOP_K, t), F32),
                   jax.ShapeDtypeStruct((TOP_K, t), jnp.int32), jax.ShapeDtypeStruct((N_EXPERTS, 1), F32)],
        scratch_shapes=[pltpu.VMEM((N_EXPERTS, 1), F32)],
        compiler_params=_params(("arbitrary",), vmem),
        name="router",
    )(h1, w_router_t, bias_col)


def _dest_kernel(idx_ref, rank_ref, ps_ref, o_ref):
    tm = idx_ref.shape[1]
    erow = lax.broadcasted_iota(jnp.int32, (N_EXPERTS, tm), 0)
    ps = ps_ref[...]
    rows = []
    for k in range(TOP_K):
        base = jnp.sum(jnp.where(erow == idx_ref[k:k + 1, :], ps, 0.0), axis=0, keepdims=True)
        rows.append(base.astype(jnp.int32) + rank_ref[k:k + 1, :])
    o_ref[...] = jnp.concatenate(rows, axis=0)


def _slot_dest(idx_t, rank_t, pstart_col):
    t = idx_t.shape[1]
    tm = min(ROUTE_TM, t)
    vmem = 2 * (3 * TOP_K * tm * 4) + 6 * N_EXPERTS * tm * 4
    return pl.pallas_call(
        _dest_kernel,
        grid=(t // tm,),
        in_specs=[pl.BlockSpec((TOP_K, tm), lambda i: (0, i)), pl.BlockSpec((TOP_K, tm), lambda i: (0, i)),
                  pl.BlockSpec((N_EXPERTS, 1), lambda i: (0, 0))],
        out_specs=pl.BlockSpec((TOP_K, tm), lambda i: (0, i)),
        out_shape=jax.ShapeDtypeStruct((TOP_K, t), jnp.int32),
        compiler_params=_params(("parallel",), vmem),
        name="slot_dest",
    )(idx_t, rank_t, pstart_col)


LANES = 128
U32 = jnp.uint32
HIGH_HALF = np.uint32(0xFFFF0000)
N_DMA_PRIORITIES = 2


def _tile_rows(d):
    return d // (2 * LANES)


def _pack_tiles(tile_ref, value, base=0):
    n, d = value.shape
    sub = _tile_rows(d)
    bits = lax.bitcast_convert_type(value.astype(BF16).astype(F32), U32)
    for c in range(sub):
        low = bits[:, c * LANES:(c + 1) * LANES] >> 16
        high = bits[:, (c + sub) * LANES:(c + sub + 1) * LANES] & HIGH_HALF
        tile_ref[pl.ds(base + c, n, stride=sub), :] = low | high


def _unpack_tiles(tile_ref, n, d, base=0):
    sub = _tile_rows(d)
    words = [tile_ref[pl.ds(base + c, n, stride=sub), :] for c in range(sub)]
    lows = [lax.bitcast_convert_type(u << 16, F32) for u in words]
    highs = [lax.bitcast_convert_type(u & HIGH_HALF, F32) for u in words]
    return jnp.concatenate(lows + highs, axis=1)


def _dispatch_kernel(pend_ref, padded_ref, nu_ref, dest_ref, h_ref, xs_ref, tile_ref, zero_ref, sem, zsem):
    tm, d = h_ref.shape
    sub = _tile_rows(d)
    zrows = zero_ref.shape[0]
    n_blocks = xs_ref.shape[0] // zrows

    @pl.when(pl.program_id(0) == 0)
    def _():
        zero_ref[...] = jnp.zeros_like(zero_ref)

        def zero_block(first_row):
            return pltpu.make_async_copy(zero_ref, xs_ref.at[pl.ds(pl.multiple_of(first_row, 8), zrows), :], zsem)

        def issue(e, carry):
            @pl.when(padded_ref[e] > 0)
            def _():
                zero_block(pend_ref[e] * sub - zrows).start()
            return carry

        def drain(e, carry):
            @pl.when(padded_ref[e] > 0)
            def _():
                zero_block(pend_ref[e] * sub - zrows).wait()
            return carry

        def issue_tail(blk, carry):
            zero_block(blk * zrows).start()
            return carry

        def drain_tail(blk, carry):
            zero_block(blk * zrows).wait()
            return carry

        lax.fori_loop(0, N_EXPERTS, issue, 0)
        lax.fori_loop(nu_ref[0], n_blocks, issue_tail, 0)
        lax.fori_loop(0, N_EXPERTS, drain, 0)
        lax.fori_loop(nu_ref[0], n_blocks, drain_tail, 0)

    _pack_tiles(tile_ref, h_ref[...])

    def row_copy(tok, dst_row):
        return pltpu.make_async_copy(tile_ref.at[pl.ds(pl.multiple_of(tok * sub, sub), sub), :],
                                     xs_ref.at[pl.ds(pl.multiple_of(dst_row * sub, sub), sub), :], sem)

    def issue(tok, carry):
        for k in range(TOP_K):
            row_copy(tok, dest_ref[tok * TOP_K + k]).start(priority=k % N_DMA_PRIORITIES)
        return carry

    lax.fori_loop(0, tm, issue, 0)

    for _ in range(TOP_K):
        pltpu.make_async_copy(tile_ref, xs_ref.at[pl.ds(0, tm * sub), :], sem).wait()


def _dispatch(h1, dest_flat, pend, padded, n_used, n_rows_pad):
    t, d = h1.shape
    sub = _tile_rows(d)
    tm = min(DISPATCH_TM, t)
    vmem = 2 * (tm * d * 4) + tm * d * 2 + MOE_BM * d * 2 + 4 * tm * d * 4
    return pl.pallas_call(
        _dispatch_kernel,
        grid_spec=pltpu.PrefetchScalarGridSpec(
            num_scalar_prefetch=3,
            grid=(t // tm,),
            in_specs=[
                pl.BlockSpec((tm * TOP_K,), lambda i, pe, pa, nu: (i,), memory_space=pltpu.SMEM),
                pl.BlockSpec((tm, d), lambda i, pe, pa, nu: (i, 0)),
            ],
            out_specs=pl.BlockSpec(memory_space=pl.ANY),
            scratch_shapes=[pltpu.VMEM((tm * sub, LANES), U32), pltpu.VMEM((MOE_BM * sub, LANES), U32),
                            pltpu.SemaphoreType.DMA(()), pltpu.SemaphoreType.DMA(())],
        ),
        out_shape=jax.ShapeDtypeStruct((n_rows_pad * sub, LANES), U32),
        compiler_params=_params(("arbitrary",), vmem),
        name="dispatch_rows",
    )(pend, padded, n_used, dest_flat, h1)


def _expert_kernel(ps_ref, pd_ref, nu_ref, wg_ref, wu_ref, wd_ref, x_hbm, y_hbm,
                   xbuf, ybuf, wgb_ref, wub_ref, wdb_ref, xsem, ysem):
    e = pl.program_id(0)
    d = wg_ref.shape[1]
    rows = xbuf.shape[1]
    bm = rows // _tile_rows(d)
    n_blocks = y_hbm.shape[0] // rows
    n_used = nu_ref[0]
    first = ps_ref[e] // bm
    nb = pd_ref[e] // bm

    def x_copy(g, slot):
        return pltpu.make_async_copy(x_hbm.at[pl.ds(pl.multiple_of(g * rows, rows), rows), :], xbuf.at[slot],
                                     xsem.at[slot])

    def y_copy(g, slot):
        return pltpu.make_async_copy(ybuf.at[slot], y_hbm.at[pl.ds(pl.multiple_of(g * rows, rows), rows), :],
                                     ysem.at[slot])

    @pl.when(e == 0)
    def _():
        x_copy(0, 0).start()

    @pl.when(nb > 0)
    def _():
        wgb_ref[...] = wg_ref[0].astype(BF16)
        wub_ref[...] = wu_ref[0].astype(BF16)
        wdb_ref[...] = wd_ref[0].astype(BF16)

        def block(j, carry):
            g = first + j
            slot = lax.rem(g, 2)
            x_copy(g, slot).wait()

            @pl.when(g + 1 < n_used)
            def _():
                x_copy(g + 1, 1 - slot).start()

            @pl.when(g >= 2)
            def _():
                y_copy(g - 2, slot).wait()

            xb = _unpack_tiles(xbuf.at[slot], bm, d).astype(BF16)
            hb = (_silu(_dot(xb, wgb_ref[...])) * _dot(xb, wub_ref[...])).astype(BF16)
            _pack_tiles(ybuf.at[slot], _dot(hb, wdb_ref[...]))
            y_copy(g, slot).start()
            return carry

        lax.fori_loop(0, nb, block, 0)

    @pl.when(e == pl.num_programs(0) - 1)
    def _():
        @pl.when(n_used >= 2)
        def _():
            y_copy(n_used - 2, lax.rem(n_used, 2)).wait()

        y_copy(n_used - 1, lax.rem(n_used - 1, 2)).wait()
        ybuf[0] = jnp.zeros_like(ybuf[0])

        def issue(g, carry):
            y_copy(g, 0).start()
            return carry

        def drain(g, carry):
            y_copy(g, 0).wait()
            return carry

        lax.fori_loop(n_used, n_blocks, issue, 0)
        lax.fori_loop(n_used, n_blocks, drain, 0)


def _experts(x_sorted, pstart, padded, n_used, w_gate, w_up, w_down):
    n_exp, d, ff = w_gate.shape
    sub = _tile_rows(d)
    bm = MOE_BM
    vmem = 2 * (3 * d * ff * 4) + 3 * d * ff * 2 + 4 * bm * d * 2 + 6 * bm * ff * 4 + 6 * bm * d * 4
    return pl.pallas_call(
        _expert_kernel,
        grid_spec=pltpu.PrefetchScalarGridSpec(
            num_scalar_prefetch=3,
            grid=(n_exp,),
            in_specs=[
                pl.BlockSpec((1, d, ff), lambda e, ps, pd, nu: (e, 0, 0)),
                pl.BlockSpec((1, d, ff), lambda e, ps, pd, nu: (e, 0, 0)),
                pl.BlockSpec((1, ff, d), lambda e, ps, pd, nu: (e, 0, 0)),
                pl.BlockSpec(memory_space=pl.ANY),
            ],
            out_specs=pl.BlockSpec(memory_space=pl.ANY),
            scratch_shapes=[pltpu.VMEM((2, bm * sub, LANES), U32), pltpu.VMEM((2, bm * sub, LANES), U32),
                            pltpu.VMEM((d, ff), BF16), pltpu.VMEM((d, ff), BF16), pltpu.VMEM((ff, d), BF16),
                            pltpu.SemaphoreType.DMA((2,)), pltpu.SemaphoreType.DMA((2,))],
        ),
        out_shape=jax.ShapeDtypeStruct(x_sorted.shape, U32),
        compiler_params=_params(("arbitrary",), vmem),
        name="expert_mlp",
    )(pstart, padded, n_used, w_gate, w_up, w_down, x_sorted)


def _final_kernel(dest_ref, dest_next_ref, h_ref, gw_ref, y_ref, wg_ref, wu_ref, wd_ref, g_ref, b_ref, o_ref,
                  rows_ref, sem):
    i = pl.program_id(0)
    tm, d = h_ref.shape
    sub = _tile_rows(d)
    slot = lax.rem(i, 2)

    def issue_tile(dref, s):
        def row_copy(tok, k, src_row):
            return pltpu.make_async_copy(
                y_ref.at[pl.ds(pl.multiple_of(src_row * sub, sub), sub), :],
                rows_ref.at[s, pl.ds(pl.multiple_of((k * tm + tok) * sub, sub), sub), :], sem.at[s])

        def issue(tok, carry):
            for k in range(TOP_K):
                row_copy(tok, k, dref[tok * TOP_K + k]).start(priority=k % N_DMA_PRIORITIES)
            return carry

        lax.fori_loop(0, tm, issue, 0, unroll=COMBINE_ISSUE_UNROLL)

    @pl.when(i == 0)
    def _():
        issue_tile(dest_ref, 0)

    @pl.when(i + 1 < pl.num_programs(0))
    def _():
        issue_tile(dest_next_ref, 1 - slot)

    h = h_ref[...]
    hb = h.astype(BF16)
    shared = _dot((_silu(_dot(hb, wg_ref[...])) * _dot(hb, wu_ref[...])).astype(BF16), wd_ref[...])

    pltpu.make_async_copy(y_ref.at[pl.ds(0, TOP_K * tm * sub), :], rows_ref.at[slot], sem.at[slot]).wait()

    rows = rows_ref.at[slot]
    gw = gw_ref[...]
    routed = gw[:, 0:1] * _unpack_tiles(rows, tm, d)
    for k in range(1, TOP_K):
        routed = routed + gw[:, k:k + 1] * _unpack_tiles(rows, tm, d, base=k * tm * sub)
    o_ref[...] = _layer_norm(DEEPNORM_ALPHA * h + (routed + shared), g_ref[...], b_ref[...])


def _final(h1, dest_flat, gw, y_sorted, w_sg, w_su, w_sd, g2, b2):
    t, d = h1.shape
    ff = w_sg.shape[1]
    sub = _tile_rows(d)
    tm = min(FINAL_TM, t)
    nt = t // tm
    vmem = 2 * (2 * tm * d * 4 + tm * LANES * 4 + 3 * d * ff * 2) + 2 * TOP_K * tm * d * 2 + 8 * tm * d * 4

    def full(shape):
        return pl.BlockSpec(shape, lambda i: tuple(0 for _ in shape))

    return pl.pallas_call(
        _final_kernel,
        grid=(nt,),
        in_specs=[
            pl.BlockSpec((tm * TOP_K,), lambda i: (i,), memory_space=pltpu.SMEM),
            pl.BlockSpec((tm * TOP_K,), lambda i: (jnp.minimum(i + 1, nt - 1),), memory_space=pltpu.SMEM),
            pl.BlockSpec((tm, d), lambda i: (i, 0)),
            pl.BlockSpec((tm, TOP_K), lambda i: (i, 0)),
            pl.BlockSpec(memory_space=pl.ANY),
            full((d, ff)), full((d, ff)), full((ff, d)), full((1, d)), full((1, d)),
        ],
        out_specs=pl.BlockSpec((tm, d), lambda i: (i, 0)),
        out_shape=jax.ShapeDtypeStruct((t, d), F32),
        scratch_shapes=[pltpu.VMEM((2, TOP_K * tm * sub, LANES), U32), pltpu.SemaphoreType.DMA((2,))],
        compiler_params=_params(("arbitrary",), vmem),
        name="combine_final",
    )(dest_flat, dest_flat, h1, gw, y_sorted, w_sg, w_su, w_sd, g2, b2)


def _lower_bound(p):
    return jnp.cumsum(jax.nn.softmax(p.astype(F32), axis=0), axis=0)[0:1]


def _block_plan(counts, n_rows):
    bm = MOE_BM
    counts = counts.reshape(-1).astype(jnp.int32)
    padded = (counts + bm - 1) // bm * bm
    pend = jnp.cumsum(padded).astype(jnp.int32)
    pstart = pend - padded
    n_blocks = n_rows // bm + N_EXPERTS
    n_used = pend[-1:] // bm
    return pstart, pend, padded, n_used.astype(jnp.int32), n_blocks


def kernel(x, mem, ln_in_g, ln_in_b, ln_mem_g, ln_mem_b, hgrn_lb_fwd, hgrn_lb_bwd, w_in, b_gate, hgrn_norm_g, w_mem_kv, w_fourier_o, b_fourier_o, w_hgrn_o, w_xattn_o, w_out, ln1_g, ln1_b, w_router, router_bias, w_exp_gate, w_exp_up, w_exp_down, w_sh_gate, w_sh_up, w_sh_down, ln2_g, ln2_b):
    bsz, s, d = x.shape
    t = bsz * s
    l = 0
    row = lambda v: v.reshape(1, -1).astype(F32)

    n_gate = N_BRANCHES * d
    n_rest = w_in.shape[2] - n_gate
    w_in_r = jnp.concatenate([w_in[l][:, n_rest:], w_in[l][:, :n_rest]], axis=1).astype(BF16)
    gate_blocks = n_gate // INPROJ_TN
    cb_fourier = gate_blocks
    cb_hq, cb_hi, cb_zf, cb_zb, cb_hg, cb_xq = (gate_blocks + 1 + n for n in range(6))

    x2 = x.reshape(t, d)
    proj = _inproj(x2, row(ln_in_g), row(ln_in_b), w_in_r)
    kv = _mem_kv(mem, row(ln_mem_g), row(ln_mem_b), w_mem_kv[l].astype(BF16))

    fm = _fourier(proj, bsz, s, cb_fourier * (INPROJ_TN // FOURIER_GROUP_DIM))

    lb_f = _lower_bound(hgrn_lb_fwd)
    lb_b = _lower_bound(hgrn_lb_bwd)
    o_fwd = _hgrn_sweep(proj, lb_f, (cb_hq, cb_hi, cb_zf), bsz, s, reverse=False)
    ho = _hgrn_sweep(proj, lb_b, (cb_hq, cb_hi, cb_zb, cb_hg), bsz, s, reverse=True,
                     o_fwd=o_fwd, norm_g=row(hgrn_norm_g[l]))

    h1 = _merge(x2, row(ln_in_g), row(ln_in_b), proj, 0, cb_xq, fm, ho, kv,
                w_xattn_o[l].astype(BF16), w_fourier_o[l].astype(BF16), row(b_fourier_o[l]),
                w_hgrn_o[l].astype(BF16), row(b_gate[l]), w_out[l].astype(BF16), row(ln1_g[l]), row(ln1_b[l]), s)

    idx_t, gw_t, rank_t, counts = _route(h1, w_router[l].T.astype(BF16),
                                         router_bias[l].reshape(N_EXPERTS, 1).astype(F32))
    pstart, pend, padded, n_used, n_blocks = _block_plan(counts, t * TOP_K)
    dest_t = _slot_dest(idx_t, rank_t, pstart.astype(F32).reshape(N_EXPERTS, 1))
    dest_flat = dest_t.T.reshape(-1)

    x_sorted = _dispatch(h1, dest_flat, pend, padded, n_used, n_blocks * MOE_BM)
    y_sorted = _experts(x_sorted, pstart, padded, n_used, w_exp_gate[l], w_exp_up[l], w_exp_down[l])
    out = _final(h1, dest_flat, gw_t.T, y_sorted, w_sh_gate[l].astype(BF16), w_sh_up[l].astype(BF16),
                 w_sh_down[l].astype(BF16), row(ln2_g[l]), row(ln2_b[l]))
    return out.reshape(bsz, s, d)
```

```python
import functools
import math

import numpy as np
import jax
import jax.numpy as jnp
from jax import lax
from jax.experimental import pallas as pl
from jax.experimental.pallas import tpu as pltpu

F32 = jnp.float32
BF16 = jnp.bfloat16

LN_EPS = 1e-5
DEPTH = 1
DEEPNORM_ALPHA = (2 * DEPTH) ** 0.25
FOURIER_GROUP_DIM = 128
FOURIER_W = 512
HGRN_HEADS = 4
HGRN_HEAD_DIM = 128
HGRN_W = HGRN_HEADS * HGRN_HEAD_DIM
HGRN_CHUNK = 64
XATTN_HEADS = 4
XATTN_HEAD_DIM = 128
XATTN_W = XATTN_HEADS * XATTN_HEAD_DIM
N_BRANCHES = 3
N_EXPERTS = 256
TOP_K = 8
N_GROUPS = 8
TOPK_GROUPS = 4
GROUP_SIZE = N_EXPERTS // N_GROUPS
ROUTED_SCALE = 2.5

V7X_SCOPED_VMEM_CAP_BYTES = 60000 * 1024

INPROJ_TM = 1024
INPROJ_TN = 512
FOURIER_P = 128
FOURIER_UNROLL = 8
HGRN_ROWS = 512
MERGE_TM = 256
ROUTE_TM = 512
MOE_BM = 256
EXPERT_SLOTS = 4
DISPATCH_TM = 256
FINAL_TM = 256
COMBINE_ISSUE_UNROLL = 4


def _params(semantics, vmem_bytes):
    return pltpu.CompilerParams(dimension_semantics=semantics,
                                vmem_limit_bytes=int(min(vmem_bytes, V7X_SCOPED_VMEM_CAP_BYTES)))


def _layer_norm(x, g, b):
    mu = jnp.mean(x, axis=-1, keepdims=True)
    xc = x - mu
    var = jnp.mean(xc * xc, axis=-1, keepdims=True)
    return xc * lax.rsqrt(var + LN_EPS) * g + b


def _silu(x):
    return x * jax.nn.sigmoid(x)


def _dot(a, b):
    return jnp.dot(a, b, preferred_element_type=F32)


def _dot_nt(a, b):
    return lax.dot_general(a, b, (((1,), (1,)), ((), ())), preferred_element_type=F32)


def _dot_tn(a, b):
    return lax.dot_general(a, b, (((0,), (0,)), ((), ())), preferred_element_type=F32)


def _inproj_kernel(x_ref, g_ref, b_ref, w_ref, o_ref, hn_ref):
    @pl.when(pl.program_id(1) == 0)
    def _():
        hn_ref[...] = _layer_norm(x_ref[...], g_ref[...], b_ref[...]).astype(BF16)

    o_ref[...] = _dot(hn_ref[...], w_ref[...])


def _inproj(x2, g, b, w):
    t, d = x2.shape
    n = w.shape[1]
    tm, tn = min(INPROJ_TM, t), INPROJ_TN
    vmem = 2 * (tm * d * 4 + d * tn * 2 + tm * tn * 4) + tm * d * 2 + 3 * tm * d * 4
    return pl.pallas_call(
        _inproj_kernel,
        grid=(t // tm, n // tn),
        in_specs=[
            pl.BlockSpec((tm, d), lambda i, j: (i, 0)),
            pl.BlockSpec((1, d), lambda i, j: (0, 0)),
            pl.BlockSpec((1, d), lambda i, j: (0, 0)),
            pl.BlockSpec((d, tn), lambda i, j: (0, j)),
        ],
        out_specs=pl.BlockSpec((tm, tn), lambda i, j: (i, j)),
        out_shape=jax.ShapeDtypeStruct((t, n), F32),
        scratch_shapes=[pltpu.VMEM((tm, d), BF16)],
        compiler_params=_params(("parallel", "arbitrary"), vmem),
        name="ln_inproj",
    )(x2, g, b, w)


def _mem_kv_kernel(m_ref, g_ref, b_ref, w_ref, o_ref):
    mn = _layer_norm(m_ref[0], g_ref[...], b_ref[...]).astype(BF16)
    o_ref[0] = _dot(mn, w_ref[...]).astype(BF16)


def _mem_kv(mem, g, b, w):
    bsz, m, d = mem.shape
    n = w.shape[1]
    vmem = 2 * (m * d * 4 + d * n * 2 + m * n * 2) + 4 * m * d * 4
    return pl.pallas_call(
        _mem_kv_kernel,
        grid=(bsz,),
        in_specs=[
            pl.BlockSpec((1, m, d), lambda i: (i, 0, 0)),
            pl.BlockSpec((1, d), lambda i: (0, 0)),
            pl.BlockSpec((1, d), lambda i: (0, 0)),
            pl.BlockSpec((d, n), lambda i: (0, 0)),
        ],
        out_specs=pl.BlockSpec((1, m, n), lambda i: (i, 0, 0)),
        out_shape=jax.ShapeDtypeStruct((bsz, m, n), BF16),
        compiler_params=_params(("parallel",), vmem),
        name="mem_kv",
    )(mem, g, b, w)


def _fourier_tables(s, c, p):
    q = s // p
    ki = np.arange(p, dtype=np.int64)
    i = np.arange(p, dtype=np.int64)
    j = np.arange(q, dtype=np.int64)
    ph = (ki[None, :, None] * (q * i[None, None, :] + j[:, None, None])) % s
    ang = 2.0 * np.pi * ph.astype(np.float64) / s
    m1 = np.concatenate([np.cos(ang), -np.sin(ang)], axis=1)
    kj = np.arange(q, dtype=np.int64)
    a2 = 2.0 * np.pi * ((kj[:, None] * j[None, :]) % q).astype(np.float64) / q
    m2 = np.block([[np.cos(a2), np.sin(a2)], [-np.sin(a2), np.cos(a2)]])
    cc = np.arange(c, dtype=np.int64)
    a3 = 2.0 * np.pi * ((cc[:, None] * cc[None, :]) % c).astype(np.float64) / c
    mc = np.concatenate([np.cos(a3), np.sin(a3)], axis=0) / math.sqrt(s * c)
    return (jnp.asarray(m1, dtype=BF16), jnp.asarray(m2, dtype=BF16), jnp.asarray(mc, dtype=BF16))


def _fourier_kernel(x_ref, m1_ref, m2_ref, mc_ref, o_ref, ys_ref, *, p, q):
    def stage1(j, carry):
        xj = x_ref[pl.ds(j, p, stride=q), :].astype(BF16)
        y = _dot(m1_ref[j], xj)
        ys_ref[pl.ds(j, p, stride=2 * q), :] = y[:p]
        ys_ref[pl.ds(q + j, p, stride=2 * q), :] = y[p:]
        return carry

    lax.fori_loop(0, q, stage1, 0, unroll=FOURIER_UNROLL)

    def stage2(ki, carry):
        start = pl.multiple_of(ki * (2 * q), 2 * q)
        y = ys_ref[pl.ds(start, 2 * q), :].astype(BF16)
        z = _dot(m2_ref[...], y)
        zc = jnp.concatenate([z[:q], z[q:]], axis=1).astype(BF16)
        o_ref[pl.ds(ki, q, stride=p), :] = _dot(zc, mc_ref[...])
        return carry

    lax.fori_loop(0, p, stage2, 0, unroll=FOURIER_UNROLL)


def _fourier(proj, bsz, s, col_block0):
    c = FOURIER_GROUP_DIM
    groups = FOURIER_W // c
    p = min(FOURIER_P, s // 8)
    q = s // p
    m1, m2, mc = _fourier_tables(s, c, p)
    vmem = 2 * (s * c * 4 + m1.size * 2 + m2.size * 2 + mc.size * 2 + s * c * 4) + 2 * s * c * 4 + 8 * p * c * 4
    return pl.pallas_call(
        functools.partial(_fourier_kernel, p=p, q=q),
        grid=(bsz, groups),
        in_specs=[
            pl.BlockSpec((s, c), lambda b, g: (b, col_block0 + g)),
            pl.BlockSpec((q, 2 * p, p), lambda b, g: (0, 0, 0)),
            pl.BlockSpec((2 * q, 2 * q), lambda b, g: (0, 0)),
            pl.BlockSpec((2 * c, c), lambda b, g: (0, 0)),
        ],
        out_specs=pl.BlockSpec((s, c), lambda b, g: (b, g)),
        out_shape=jax.ShapeDtypeStruct((bsz * s, FOURIER_W), F32),
        scratch_shapes=[pltpu.VMEM((2 * s, c), F32)],
        compiler_params=_params(("parallel", "parallel"), vmem),
        name="fourier_mix",
    )(proj, m1, m2, mc)


def _hgrn_kernel(*refs, reverse, finalize, n_chunks):
    if finalize:
        q_ref, i_ref, z_ref, lb_ref, of_ref, g_ref, ng_ref, o_ref, st_ref = refs
    else:
        q_ref, i_ref, z_ref, lb_ref, o_ref, st_ref = refs

    @pl.when(pl.program_id(1) == 0)
    def _():
        st_ref[...] = jnp.zeros_like(st_ref)

    ck = HGRN_CHUNK
    hd = HGRN_HEAD_DIM
    row = lax.broadcasted_iota(jnp.int32, (ck, ck), 0)
    col = lax.broadcasted_iota(jnp.int32, (ck, ck), 1)
    tri = (col >= row) if reverse else (col <= row)
    cum = tri.astype(F32)
    lb = lb_ref[...]

    order = range(n_chunks - 1, -1, -1) if reverse else range(n_chunks)
    for c in order:
        rows = pl.ds(c * ck, ck)
        q = _silu(q_ref[rows, :])
        v = i_ref[rows, :]
        f = lb + (1.0 - lb) * jax.nn.sigmoid(z_ref[rows, :])
        k = 1.0 - f
        a = jnp.dot(cum, jnp.log(f), preferred_element_type=F32, precision=lax.Precision.HIGHEST)
        a_end = a[0:1, :] if reverse else a[ck - 1:ck, :]
        q_dec = (q * jnp.exp(a)).astype(BF16)
        k_inv = (k * jnp.exp(-a)).astype(BF16)
        k_end = (k * jnp.exp(a_end - a)).astype(BF16)
        decay = jnp.exp(a_end)
        vb = v.astype(BF16)
        outs = []
        for h in range(HGRN_HEADS):
            sl = slice(h * hd, (h + 1) * hd)
            sc = jnp.where(tri, _dot_nt(q_dec[:, sl], k_inv[:, sl]), 0.0)
            st = st_ref[h]
            o_h = _dot(sc.astype(BF16), vb[:, sl]) + _dot_nt(q_dec[:, sl], st.astype(BF16))
            st_ref[h] = st * decay[:, sl] + _dot_tn(vb[:, sl], k_end[:, sl])
            if finalize:
                o_h = o_h + of_ref[rows, sl]
                ms = jnp.mean(o_h * o_h, axis=-1, keepdims=True)
                o_h = o_h * lax.rsqrt(ms + LN_EPS) * ng_ref[:, sl]
            outs.append(o_h)
        o = jnp.concatenate(outs, axis=1)
        if finalize:
            o_ref[rows, :] = (o * _silu(g_ref[rows, :])).astype(o_ref.dtype)
        else:
            o_ref[rows, :] = o


def _hgrn_sweep(proj, lb, cols, bsz, s, *, reverse, o_fwd=None, norm_g=None):
    w = HGRN_W
    rb = min(HGRN_ROWS, s)
    nb = s // rb
    finalize = o_fwd is not None

    def rmap(b, n):
        return b * nb + (nb - 1 - n if reverse else n)

    def cmap(cb):
        return lambda b, n: (rmap(b, n), cb)

    in_specs = [pl.BlockSpec((rb, w), cmap(cols[0])), pl.BlockSpec((rb, w), cmap(cols[1])),
                pl.BlockSpec((rb, w), cmap(cols[2])), pl.BlockSpec((1, w), lambda b, n: (0, 0))]
    args = [proj, proj, proj, lb]
    if finalize:
        in_specs += [pl.BlockSpec((rb, w), cmap(0)), pl.BlockSpec((rb, w), cmap(cols[3])),
                     pl.BlockSpec((1, w), lambda b, n: (0, 0))]
        args += [o_fwd, proj, norm_g]
    vmem = 2 * (6 * rb * w * 4) + HGRN_HEADS * HGRN_HEAD_DIM * HGRN_HEAD_DIM * 4 + 24 * HGRN_CHUNK * w * 4
    return pl.pallas_call(
        functools.partial(_hgrn_kernel, reverse=reverse, finalize=finalize, n_chunks=rb // HGRN_CHUNK),
        grid=(bsz, nb),
        in_specs=in_specs,
        out_specs=pl.BlockSpec((rb, w), cmap(0)),
        out_shape=jax.ShapeDtypeStruct((bsz * s, w), BF16 if finalize else F32),
        scratch_shapes=[pltpu.VMEM((HGRN_HEADS, HGRN_HEAD_DIM, HGRN_HEAD_DIM), F32)],
        compiler_params=_params(("parallel", "arbitrary"), vmem),
        name="hgrn_bwd" if reverse else "hgrn_fwd",
    )(*args)


def _merge_kernel(x_ref, lg_ref, lbi_ref, gl_ref, xq_ref, fm_ref, ho_ref, kv_ref, wx_ref, wf_ref, bf_ref,
                  wh_ref, bg_ref, wo_ref, g1_ref, b1_ref, o_ref):
    d = x_ref.shape[1]
    h = _layer_norm(x_ref[...], lg_ref[...], lbi_ref[...])

    kv = kv_ref[0]
    xq = xq_ref[...].astype(BF16)
    heads = []
    for hh in range(XATTN_HEADS):
        sl = slice(hh * XATTN_HEAD_DIM, (hh + 1) * XATTN_HEAD_DIM)
        vsl = slice(XATTN_W + hh * XATTN_HEAD_DIM, XATTN_W + (hh + 1) * XATTN_HEAD_DIM)
        sc = _dot_nt(xq[:, sl], kv[:, sl]) * (XATTN_HEAD_DIM ** -0.5)
        sc = sc - jnp.max(sc, axis=-1, keepdims=True)
        e = jnp.exp(sc)
        pr = e / jnp.sum(e, axis=-1, keepdims=True)
        heads.append(_dot(pr.astype(BF16), kv[:, vsl]))
    att = jnp.concatenate(heads, axis=1).astype(BF16)

    y_x = _dot(att, wx_ref[...])
    y_f = _dot(fm_ref[...].astype(BF16), wf_ref[...]) + bf_ref[...]
    y_h = _dot(ho_ref[...], wh_ref[...])
    gate = jax.nn.sigmoid(gl_ref[...] + bg_ref[...])
    merged = gate[:, 0:d] * y_f + gate[:, d:2 * d] * y_h + gate[:, 2 * d:3 * d] * y_x
    y = _dot(merged.astype(BF16), wo_ref[...])
    o_ref[...] = _layer_norm(DEEPNORM_ALPHA * h + y, g1_ref[...], b1_ref[...])


def _merge(x2, ln_g, ln_b, proj, gate_cb, xq_cb, fm, ho, kv, w_xo, w_fo, b_fo, w_ho, b_gate, w_out, g1, b1, s):
    t, d = x2.shape
    tm = min(MERGE_TM, s)
    per_b = s // tm
    m, kvw = kv.shape[1], kv.shape[2]
    gw = N_BRANCHES * d

    def full(shape):
        return pl.BlockSpec(shape, lambda i: tuple(0 for _ in shape))

    vmem = (2 * (tm * d * 4 + tm * gw * 4 + tm * XATTN_W * 4 + tm * FOURIER_W * 4 + tm * HGRN_W * 2 + m * kvw * 2
                 + 3 * XATTN_W * d * 2 + d * d * 2 + tm * d * 4) + 10 * tm * d * 4 + 2 * tm * gw * 4)
    return pl.pallas_call(
        _merge_kernel,
        grid=(t // tm,),
        in_specs=[
            pl.BlockSpec((tm, d), lambda i: (i, 0)),
            full((1, d)), full((1, d)),
            pl.BlockSpec((tm, gw), lambda i: (i, gate_cb)),
            pl.BlockSpec((tm, XATTN_W), lambda i: (i, xq_cb)),
            pl.BlockSpec((tm, FOURIER_W), lambda i: (i, 0)),
            pl.BlockSpec((tm, HGRN_W), lambda i: (i, 0)),
            pl.BlockSpec((1, m, kvw), lambda i: (i // per_b, 0, 0)),
            full((XATTN_W, d)), full((FOURIER_W, d)), full((1, d)), full((HGRN_W, d)), full((1, gw)),
            full((d, d)), full((1, d)), full((1, d)),
        ],
        out_specs=pl.BlockSpec((tm, d), lambda i: (i, 0)),
        out_shape=jax.ShapeDtypeStruct((t, d), F32),
        compiler_params=_params(("parallel",), vmem),
        name="merge_out",
    )(x2, ln_g, ln_b, proj, proj, fm, ho, kv, w_xo, w_fo, b_fo, w_ho, b_gate, w_out, g1, b1)


def _route_kernel(h_ref, wr_ref, bias_ref, idx_ref, gw_ref, rank_ref, cnt_ref, carry_ref):
    tm = h_ref.shape[0]
    neg = -jnp.inf

    @pl.when(pl.program_id(0) == 0)
    def _():
        carry_ref[...] = jnp.zeros_like(carry_ref)

    aff = jax.nn.sigmoid(_dot_nt(wr_ref[...], h_ref[...].astype(BF16)))
    sel = aff + bias_ref[...]

    giota = lax.broadcasted_iota(jnp.int32, (GROUP_SIZE, tm), 0)
    scores = []
    for g in range(N_GROUPS):
        slab = sel[g * GROUP_SIZE:(g + 1) * GROUP_SIZE, :]
        m1 = jnp.max(slab, axis=0, keepdims=True)
        first = jnp.min(jnp.where(slab == m1, giota, GROUP_SIZE), axis=0, keepdims=True)
        m2 = jnp.max(jnp.where(giota == first, neg, slab), axis=0, keepdims=True)
        scores.append(m1 + m2)
    gs = jnp.concatenate(scores, axis=0)

    grow = lax.broadcasted_iota(jnp.int32, (N_GROUPS, tm), 0)
    gsel = jnp.zeros((N_GROUPS, tm), jnp.bool_)
    work = gs
    for _ in range(TOPK_GROUPS):
        m = jnp.max(work, axis=0, keepdims=True)
        first = jnp.min(jnp.where(work == m, grow, N_GROUPS), axis=0, keepdims=True)
        hit = grow == first
        gsel = jnp.logical_or(gsel, hit)
        work = jnp.where(hit, neg, work)

    masked = jnp.concatenate(
        [jnp.where(gsel[g:g + 1, :], sel[g * GROUP_SIZE:(g + 1) * GROUP_SIZE, :], neg) for g in range(N_GROUPS)],
        axis=0)

    erow = lax.broadcasted_iota(jnp.int32, (N_EXPERTS, tm), 0)
    ids, ws = [], []
    chosen = jnp.zeros((N_EXPERTS, tm), jnp.bool_)
    for _ in range(TOP_K):
        m = jnp.max(masked, axis=0, keepdims=True)
        first = jnp.min(jnp.where(masked == m, erow, N_EXPERTS), axis=0, keepdims=True)
        hit = erow == first
        ids.append(first)
        ws.append(jnp.sum(jnp.where(hit, aff, 0.0), axis=0, keepdims=True))
        masked = jnp.where(hit, neg, masked)
        chosen = jnp.logical_or(chosen, hit)
    w = jnp.concatenate(ws, axis=0)
    idx_ref[...] = jnp.concatenate(ids, axis=0)
    gw_ref[...] = w / jnp.sum(w, axis=0, keepdims=True) * ROUTED_SCALE

    chosen_f = jnp.where(chosen, 1.0, 0.0)
    srow = lax.broadcasted_iota(jnp.int32, (tm, tm), 0)
    scol = lax.broadcasted_iota(jnp.int32, (tm, tm), 1)
    before = jnp.where(srow < scol, 1.0, 0.0).astype(BF16)
    prefix = _dot(chosen_f.astype(BF16), before) + carry_ref[...]
    ranks = [jnp.sum(jnp.where(erow == ids[k], prefix, 0.0), axis=0, keepdims=True) for k in range(TOP_K)]
    rank_ref[...] = jnp.concatenate(ranks, axis=0).astype(jnp.int32)
    carry_ref[...] = carry_ref[...] + jnp.sum(chosen_f, axis=1, keepdims=True)
    cnt_ref[...] = carry_ref[...]


def _route(h1, w_router_t, bias_col):
    t, d = h1.shape
    tm = min(ROUTE_TM, t)
    vmem = 2 * (tm * d * 4 + N_EXPERTS * d * 2 + 3 * TOP_K * tm * 4) + tm * d * 2 + 16 * N_EXPERTS * tm * 4 + tm * tm * 8
    return pl.pallas_call(
        _route_kernel,
        grid=(t // tm,),
        in_specs=[
            pl.BlockSpec((tm, d), lambda i: (i, 0)),
            pl.BlockSpec((N_EXPERTS, d), lambda i: (0, 0)),
            pl.BlockSpec((N_EXPERTS, 1), lambda i: (0, 0)),
        ],
        out_specs=[pl.BlockSpec((TOP_K, tm), lambda i: (0, i)), pl.BlockSpec((TOP_K, tm), lambda i: (0, i)),
                   pl.BlockSpec((TOP_K, tm), lambda i: (0, i)), pl.BlockSpec((N_EXPERTS, 1), lambda i: (0, 0))],
        out_shape=[jax.ShapeDtypeStruct((TOP_K, t), jnp.int32), jax.ShapeDtypeStruct((TOP_K, t), F32),
                   jax.ShapeDtypeStruct((TOP_K, t), jnp.int32), jax.ShapeDtypeStruct((N_EXPERTS, 1), F32)],
        scratch_shapes=[pltpu.VMEM((N_EXPERTS, 1), F32)],
        compiler_params=_params(("arbitrary",), vmem),
        name="router",
    )(h1, w_router_t, bias_col)


def _dest_kernel(idx_ref, rank_ref, ps_ref, o_ref):
    tm = idx_ref.shape[1]
    erow = lax.broadcasted_iota(jnp.int32, (N_EXPERTS, tm), 0)
    ps = ps_ref[...]
    rows = []
    for k in range(TOP_K):
        base = jnp.sum(jnp.where(erow == idx_ref[k:k + 1, :], ps, 0.0), axis=0, keepdims=True)
        rows.append(base.astype(jnp.int32) + rank_ref[k:k + 1, :])
    o_ref[...] = jnp.concatenate(rows, axis=0)


def _slot_dest(idx_t, rank_t, pstart_col):
    t = idx_t.shape[1]
    tm = min(ROUTE_TM, t)
    vmem = 2 * (3 * TOP_K * tm * 4) + 6 * N_EXPERTS * tm * 4
    return pl.pallas_call(
        _dest_kernel,
        grid=(t // tm,),
        in_specs=[pl.BlockSpec((TOP_K, tm), lambda i: (0, i)), pl.BlockSpec((TOP_K, tm), lambda i: (0, i)),
                  pl.BlockSpec((N_EXPERTS, 1), lambda i: (0, 0))],
        out_specs=pl.BlockSpec((TOP_K, tm), lambda i: (0, i)),
        out_shape=jax.ShapeDtypeStruct((TOP_K, t), jnp.int32),
        compiler_params=_params(("parallel",), vmem),
        name="slot_dest",
    )(idx_t, rank_t, pstart_col)


LANES = 128
U32 = jnp.uint32
HIGH_HALF = np.uint32(0xFFFF0000)
N_DMA_PRIORITIES = 2


def _tile_rows(d):
    return d // (2 * LANES)


def _pack_tiles(tile_ref, value, base=0):
    n, d = value.shape
    sub = _tile_rows(d)
    bits = lax.bitcast_convert_type(value.astype(BF16).astype(F32), U32)
    for c in range(sub):
        low = bits[:, c * LANES:(c + 1) * LANES] >> 16
        high = bits[:, (c + sub) * LANES:(c + sub + 1) * LANES] & HIGH_HALF
        tile_ref[pl.ds(base + c, n, stride=sub), :] = low | high


def _unpack_tiles(tile_ref, n, d, base=0):
    sub = _tile_rows(d)
    words = [tile_ref[pl.ds(base + c, n, stride=sub), :] for c in range(sub)]
    lows = [lax.bitcast_convert_type(u << 16, F32) for u in words]
    highs = [lax.bitcast_convert_type(u & HIGH_HALF, F32) for u in words]
    return jnp.concatenate(lows + highs, axis=1)


def _dispatch_kernel(pend_ref, padded_ref, nu_ref, dest_ref, h_ref, xs_ref, tile_ref, zero_ref, sem, zsem):
    tm, d = h_ref.shape
    sub = _tile_rows(d)
    zrows = zero_ref.shape[0]
    n_blocks = xs_ref.shape[0] // zrows

    @pl.when(pl.program_id(0) == 0)
    def _():
        zero_ref[...] = jnp.zeros_like(zero_ref)

        def zero_block(first_row):
            return pltpu.make_async_copy(zero_ref, xs_ref.at[pl.ds(pl.multiple_of(first_row, 8), zrows), :], zsem)

        def issue(e, carry):
            @pl.when(padded_ref[e] > 0)
            def _():
                zero_block(pend_ref[e] * sub - zrows).start()
            return carry

        def drain(e, carry):
            @pl.when(padded_ref[e] > 0)
            def _():
                zero_block(pend_ref[e] * sub - zrows).wait()
            return carry

        def issue_tail(blk, carry):
            zero_block(blk * zrows).start()
            return carry

        def drain_tail(blk, carry):
            zero_block(blk * zrows).wait()
            return carry

        lax.fori_loop(0, N_EXPERTS, issue, 0)
        lax.fori_loop(nu_ref[0], n_blocks, issue_tail, 0)
        lax.fori_loop(0, N_EXPERTS, drain, 0)
        lax.fori_loop(nu_ref[0], n_blocks, drain_tail, 0)

    i = pl.program_id(0)
    slot = lax.rem(i, 2)
    _pack_tiles(tile_ref.at[slot], h_ref[...])

    def row_copy(tok, dst_row):
        return pltpu.make_async_copy(tile_ref.at[slot, pl.ds(pl.multiple_of(tok * sub, sub), sub), :],
                                     xs_ref.at[pl.ds(pl.multiple_of(dst_row * sub, sub), sub), :], sem.at[slot])

    def issue(tok, carry):
        for k in range(TOP_K):
            row_copy(tok, dest_ref[tok * TOP_K + k]).start(priority=k % N_DMA_PRIORITIES)
        return carry

    lax.fori_loop(0, tm, issue, 0)

    def wait_tile(s):
        for _ in range(TOP_K):
            pltpu.make_async_copy(tile_ref.at[s], xs_ref.at[pl.ds(0, tm * sub), :], sem.at[s]).wait()

    @pl.when(i > 0)
    def _():
        wait_tile(1 - slot)

    @pl.when(i == pl.num_programs(0) - 1)
    def _():
        wait_tile(slot)


def _dispatch(h1, dest_flat, pend, padded, n_used, n_rows_pad):
    t, d = h1.shape
    sub = _tile_rows(d)
    tm = min(DISPATCH_TM, t)
    vmem = 2 * (tm * d * 4) + tm * d * 2 + MOE_BM * d * 2 + 4 * tm * d * 4
    return pl.pallas_call(
        _dispatch_kernel,
        grid_spec=pltpu.PrefetchScalarGridSpec(
            num_scalar_prefetch=3,
            grid=(t // tm,),
            in_specs=[
                pl.BlockSpec((tm * TOP_K,), lambda i, pe, pa, nu: (i,), memory_space=pltpu.SMEM),
                pl.BlockSpec((tm, d), lambda i, pe, pa, nu: (i, 0)),
            ],
            out_specs=pl.BlockSpec(memory_space=pl.ANY),
            scratch_shapes=[pltpu.VMEM((2, tm * sub, LANES), U32), pltpu.VMEM((MOE_BM * sub, LANES), U32),
                            pltpu.SemaphoreType.DMA((2,)), pltpu.SemaphoreType.DMA(())],
        ),
        out_shape=jax.ShapeDtypeStruct((n_rows_pad * sub, LANES), U32),
        compiler_params=_params(("arbitrary",), vmem),
        name="dispatch_rows",
    )(pend, padded, n_used, dest_flat, h1)


def _expert_kernel(ps_ref, pd_ref, nu_ref, wg_ref, wu_ref, wd_ref, x_hbm, y_hbm,
                   xbuf, ybuf, wgb_ref, wub_ref, wdb_ref, xsem, ysem):
    e = pl.program_id(0)
    d = wg_ref.shape[1]
    slots = xbuf.shape[0]
    ahead = slots - 1
    rows = xbuf.shape[1]
    bm = rows // _tile_rows(d)
    n_blocks = y_hbm.shape[0] // rows
    n_used = nu_ref[0]
    first = ps_ref[e] // bm
    nb = pd_ref[e] // bm

    def x_copy(g, slot):
        return pltpu.make_async_copy(x_hbm.at[pl.ds(pl.multiple_of(g * rows, rows), rows), :], xbuf.at[slot],
                                     xsem.at[slot])

    def y_copy(g, slot):
        return pltpu.make_async_copy(ybuf.at[slot], y_hbm.at[pl.ds(pl.multiple_of(g * rows, rows), rows), :],
                                     ysem.at[slot])

    @pl.when(e == 0)
    def _():
        for g0 in range(ahead):
            @pl.when(g0 < n_used)
            def _():
                x_copy(g0, g0).start()

    @pl.when(nb > 0)
    def _():
        wgb_ref[...] = wg_ref[0].astype(BF16)
        wub_ref[...] = wu_ref[0].astype(BF16)
        wdb_ref[...] = wd_ref[0].astype(BF16)

        def block(j, carry):
            g = first + j
            slot = lax.rem(g, slots)
            x_copy(g, slot).wait()

            @pl.when(g + ahead < n_used)
            def _():
                x_copy(g + ahead, lax.rem(g + ahead, slots)).start()

            @pl.when(g >= slots)
            def _():
                y_copy(g - slots, slot).wait()

            xb = _unpack_tiles(xbuf.at[slot], bm, d).astype(BF16)
            hb = (_silu(_dot(xb, wgb_ref[...])) * _dot(xb, wub_ref[...])).astype(BF16)
            _pack_tiles(ybuf.at[slot], _dot(hb, wdb_ref[...]))
            y_copy(g, slot).start()
            return carry

        lax.fori_loop(0, nb, block, 0)

    @pl.when(e == pl.num_programs(0) - 1)
    def _():
        def drain_used(g, carry):
            y_copy(g, lax.rem(g, slots)).wait()
            return carry

        lax.fori_loop(jnp.maximum(n_used - slots, 0), n_used, drain_used, 0)
        ybuf[0] = jnp.zeros_like(ybuf[0])

        def issue(g, carry):
            y_copy(g, 0).start()
            return carry

        def drain(g, carry):
            y_copy(g, 0).wait()
            return carry

        lax.fori_loop(n_used, n_blocks, issue, 0)
        lax.fori_loop(n_used, n_blocks, drain, 0)


def _experts(x_sorted, pstart, padded, n_used, w_gate, w_up, w_down):
    n_exp, d, ff = w_gate.shape
    sub = _tile_rows(d)
    bm = MOE_BM
    vmem = 2 * (3 * d * ff * 4) + 3 * d * ff * 2 + 2 * EXPERT_SLOTS * bm * d * 2 + 6 * bm * ff * 4 + 6 * bm * d * 4
    return pl.pallas_call(
        _expert_kernel,
        grid_spec=pltpu.PrefetchScalarGridSpec(
            num_scalar_prefetch=3,
            grid=(n_exp,),
            in_specs=[
                pl.BlockSpec((1, d, ff), lambda e, ps, pd, nu: (e, 0, 0)),
                pl.BlockSpec((1, d, ff), lambda e, ps, pd, nu: (e, 0, 0)),
                pl.BlockSpec((1, ff, d), lambda e, ps, pd, nu: (e, 0, 0)),
                pl.BlockSpec(memory_space=pl.ANY),
            ],
            out_specs=pl.BlockSpec(memory_space=pl.ANY),
            scratch_shapes=[pltpu.VMEM((EXPERT_SLOTS, bm * sub, LANES), U32),
                            pltpu.VMEM((EXPERT_SLOTS, bm * sub, LANES), U32),
                            pltpu.VMEM((d, ff), BF16), pltpu.VMEM((d, ff), BF16), pltpu.VMEM((ff, d), BF16),
                            pltpu.SemaphoreType.DMA((EXPERT_SLOTS,)), pltpu.SemaphoreType.DMA((EXPERT_SLOTS,))],
        ),
        out_shape=jax.ShapeDtypeStruct(x_sorted.shape, U32),
        compiler_params=_params(("arbitrary",), vmem),
        name="expert_mlp",
    )(pstart, padded, n_used, w_gate, w_up, w_down, x_sorted)


def _final_kernel(dest_ref, dest_next_ref, h_ref, gw_ref, y_ref, wg_ref, wu_ref, wd_ref, g_ref, b_ref, o_ref,
                  rows_ref, sem):
    i = pl.program_id(0)
    tm, d = h_ref.shape
    sub = _tile_rows(d)
    slot = lax.rem(i, 2)

    def issue_tile(dref, s):
        def row_copy(tok, k, src_row):
            return pltpu.make_async_copy(
                y_ref.at[pl.ds(pl.multiple_of(src_row * sub, sub), sub), :],
                rows_ref.at[s, pl.ds(pl.multiple_of((k * tm + tok) * sub, sub), sub), :], sem.at[s])

        def issue(tok, carry):
            for k in range(TOP_K):
                row_copy(tok, k, dref[tok * TOP_K + k]).start(priority=k % N_DMA_PRIORITIES)
            return carry

        lax.fori_loop(0, tm, issue, 0, unroll=COMBINE_ISSUE_UNROLL)

    @pl.when(i == 0)
    def _():
        issue_tile(dest_ref, 0)

    @pl.when(i + 1 < pl.num_programs(0))
    def _():
        issue_tile(dest_next_ref, 1 - slot)

    h = h_ref[...]
    hb = h.astype(BF16)
    shared = _dot((_silu(_dot(hb, wg_ref[...])) * _dot(hb, wu_ref[...])).astype(BF16), wd_ref[...])

    pltpu.make_async_copy(y_ref.at[pl.ds(0, TOP_K * tm * sub), :], rows_ref.at[slot], sem.at[slot]).wait()

    rows = rows_ref.at[slot]
    gw = gw_ref[...]
    routed = gw[:, 0:1] * _unpack_tiles(rows, tm, d)
    for k in range(1, TOP_K):
        routed = routed + gw[:, k:k + 1] * _unpack_tiles(rows, tm, d, base=k * tm * sub)
    o_ref[...] = _layer_norm(DEEPNORM_ALPHA * h + (routed + shared), g_ref[...], b_ref[...])


def _final(h1, dest_flat, gw, y_sorted, w_sg, w_su, w_sd, g2, b2):
    t, d = h1.shape
    ff = w_sg.shape[1]
    sub = _tile_rows(d)
    tm = min(FINAL_TM, t)
    nt = t // tm
    vmem = 2 * (2 * tm * d * 4 + tm * LANES * 4 + 3 * d * ff * 2) + 2 * TOP_K * tm * d * 2 + 8 * tm * d * 4

    def full(shape):
        return pl.BlockSpec(shape, lambda i: tuple(0 for _ in shape))

    return pl.pallas_call(
        _final_kernel,
        grid=(nt,),
        in_specs=[
            pl.BlockSpec((tm * TOP_K,), lambda i: (i,), memory_space=pltpu.SMEM),
            pl.BlockSpec((tm * TOP_K,), lambda i: (jnp.minimum(i + 1, nt - 1),), memory_space=pltpu.SMEM),
            pl.BlockSpec((tm, d), lambda i: (i, 0)),
            pl.BlockSpec((tm, TOP_K), lambda i: (i, 0)),
            pl.BlockSpec(memory_space=pl.ANY),
            full((d, ff)), full((d, ff)), full((ff, d)), full((1, d)), full((1, d)),
        ],
        out_specs=pl.BlockSpec((tm, d), lambda i: (i, 0)),
        out_shape=jax.ShapeDtypeStruct((t, d), F32),
        scratch_shapes=[pltpu.VMEM((2, TOP_K * tm * sub, LANES), U32), pltpu.SemaphoreType.DMA((2,))],
        compiler_params=_params(("arbitrary",), vmem),
        name="combine_final",
    )(dest_flat, dest_flat, h1, gw, y_sorted, w_sg, w_su, w_sd, g2, b2)


def _lower_bound(p):
    return jnp.cumsum(jax.nn.softmax(p.astype(F32), axis=0), axis=0)[0:1]


def _block_plan(counts, n_rows):
    bm = MOE_BM
    counts = counts.reshape(-1).astype(jnp.int32)
    padded = (counts + bm - 1) // bm * bm
    pend = jnp.cumsum(padded).astype(jnp.int32)
    pstart = pend - padded
    n_blocks = n_rows // bm + N_EXPERTS
    n_used = pend[-1:] // bm
    return pstart, pend, padded, n_used.astype(jnp.int32), n_blocks


def kernel(x, mem, ln_in_g, ln_in_b, ln_mem_g, ln_mem_b, hgrn_lb_fwd, hgrn_lb_bwd, w_in, b_gate, hgrn_norm_g, w_mem_kv, w_fourier_o, b_fourier_o, w_hgrn_o, w_xattn_o, w_out, ln1_g, ln1_b, w_router, router_bias, w_exp_gate, w_exp_up, w_exp_down, w_sh_gate, w_sh_up, w_sh_down, ln2_g, ln2_b):
    bsz, s, d = x.shape
    t = bsz * s
    l = 0
    row = lambda v: v.reshape(1, -1).astype(F32)

    n_gate = N_BRANCHES * d
    n_rest = w_in.shape[2] - n_gate
    w_in_r = jnp.concatenate([w_in[l][:, n_rest:], w_in[l][:, :n_rest]], axis=1).astype(BF16)
    gate_blocks = n_gate // INPROJ_TN
    cb_fourier = gate_blocks
    cb_hq, cb_hi, cb_zf, cb_zb, cb_hg, cb_xq = (gate_blocks + 1 + n for n in range(6))

    x2 = x.reshape(t, d)
    proj = _inproj(x2, row(ln_in_g), row(ln_in_b), w_in_r)
    kv = _mem_kv(mem, row(ln_mem_g), row(ln_mem_b), w_mem_kv[l].astype(BF16))

    fm = _fourier(proj, bsz, s, cb_fourier * (INPROJ_TN // FOURIER_GROUP_DIM))

    lb_f = _lower_bound(hgrn_lb_fwd)
    lb_b = _lower_bound(hgrn_lb_bwd)
    o_fwd = _hgrn_sweep(proj, lb_f, (cb_hq, cb_hi, cb_zf), bsz, s, reverse=False)
    ho = _hgrn_sweep(proj, lb_b, (cb_hq, cb_hi, cb_zb, cb_hg), bsz, s, reverse=True,
                     o_fwd=o_fwd, norm_g=row(hgrn_norm_g[l]))

    h1 = _merge(x2, row(ln_in_g), row(ln_in_b), proj, 0, cb_xq, fm, ho, kv,
                w_xattn_o[l].astype(BF16), w_fourier_o[l].astype(BF16), row(b_fourier_o[l]),
                w_hgrn_o[l].astype(BF16), row(b_gate[l]), w_out[l].astype(BF16), row(ln1_g[l]), row(ln1_b[l]), s)

    idx_t, gw_t, rank_t, counts = _route(h1, w_router[l].T.astype(BF16),
                                         router_bias[l].reshape(N_EXPERTS, 1).astype(F32))
    pstart, pend, padded, n_used, n_blocks = _block_plan(counts, t * TOP_K)
    dest_t = _slot_dest(idx_t, rank_t, pstart.astype(F32).reshape(N_EXPERTS, 1))
    dest_flat = dest_t.T.reshape(-1)

    x_sorted = _dispatch(h1, dest_flat, pend, padded, n_used, n_blocks * MOE_BM)
    y_sorted = _experts(x_sorted, pstart, padded, n_used, w_exp_gate[l], w_exp_up[l], w_exp_down[l])
    out = _final(h1, dest_flat, gw_t.T, y_sorted, w_sh_gate[l].astype(BF16), w_sh_up[l].astype(BF16),
                 w_sh_down[l].astype(BF16), row(ln2_g[l]), row(ln2_b[l]))
    return out.reshape(bsz, s, d)
```

```python
import functools
import math

import numpy as np
import jax
import jax.numpy as jnp
from jax import lax
from jax.experimental import pallas as pl
from jax.experimental.pallas import tpu as pltpu

F32 = jnp.float32
BF16 = jnp.bfloat16

LN_EPS = 1e-5
DEPTH = 1
DEEPNORM_ALPHA = (2 * DEPTH) ** 0.25
FOURIER_GROUP_DIM = 128
FOURIER_W = 512
HGRN_HEADS = 4
HGRN_HEAD_DIM = 128
HGRN_W = HGRN_HEADS * HGRN_HEAD_DIM
HGRN_CHUNK = 64
XATTN_HEADS = 4
XATTN_HEAD_DIM = 128
XATTN_W = XATTN_HEADS * XATTN_HEAD_DIM
N_BRANCHES = 3
N_EXPERTS = 256
TOP_K = 8
N_GROUPS = 8
TOPK_GROUPS = 4
GROUP_SIZE = N_EXPERTS // N_GROUPS
ROUTED_SCALE = 2.5

V7X_SCOPED_VMEM_CAP_BYTES = 60000 * 1024

INPROJ_TM = 1024
INPROJ_TN = 512
FOURIER_P = 128
FOURIER_UNROLL = 8
HGRN_ROWS = 512
MERGE_TM = 256
ROUTE_TM = 512
MOE_BM = 256
EXPERT_SLOTS = 4
DISPATCH_TM = 256
FINAL_TM = 256
COMBINE_ISSUE_UNROLL = 4


def _params(semantics, vmem_bytes):
    return pltpu.CompilerParams(dimension_semantics=semantics,
                                vmem_limit_bytes=int(min(vmem_bytes, V7X_SCOPED_VMEM_CAP_BYTES)))


def _layer_norm(x, g, b):
    mu = jnp.mean(x, axis=-1, keepdims=True)
    xc = x - mu
    var = jnp.mean(xc * xc, axis=-1, keepdims=True)
    return xc * lax.rsqrt(var + LN_EPS) * g + b


def _silu(x):
    return x * jax.nn.sigmoid(x)


def _dot(a, b):
    return jnp.dot(a, b, preferred_element_type=F32)


def _dot_nt(a, b):
    return lax.dot_general(a, b, (((1,), (1,)), ((), ())), preferred_element_type=F32)


def _dot_tn(a, b):
    return lax.dot_general(a, b, (((0,), (0,)), ((), ())), preferred_element_type=F32)


def _inproj_kernel(x_ref, g_ref, b_ref, w_ref, o_ref, hn_ref):
    @pl.when(pl.program_id(1) == 0)
    def _():
        hn_ref[...] = _layer_norm(x_ref[...], g_ref[...], b_ref[...]).astype(BF16)

    o_ref[...] = _dot(hn_ref[...], w_ref[...])


def _inproj(x2, g, b, w):
    t, d = x2.shape
    n = w.shape[1]
    tm, tn = min(INPROJ_TM, t), INPROJ_TN
    vmem = 2 * (tm * d * 4 + d * tn * 2 + tm * tn * 4) + tm * d * 2 + 3 * tm * d * 4
    return pl.pallas_call(
        _inproj_kernel,
        grid=(t // tm, n // tn),
        in_specs=[
            pl.BlockSpec((tm, d), lambda i, j: (i, 0)),
            pl.BlockSpec((1, d), lambda i, j: (0, 0)),
            pl.BlockSpec((1, d), lambda i, j: (0, 0)),
            pl.BlockSpec((d, tn), lambda i, j: (0, j)),
        ],
        out_specs=pl.BlockSpec((tm, tn), lambda i, j: (i, j)),
        out_shape=jax.ShapeDtypeStruct((t, n), F32),
        scratch_shapes=[pltpu.VMEM((tm, d), BF16)],
        compiler_params=_params(("parallel", "arbitrary"), vmem),
        name="ln_inproj",
    )(x2, g, b, w)


def _mem_kv_kernel(m_ref, g_ref, b_ref, w_ref, o_ref):
    mn = _layer_norm(m_ref[0], g_ref[...], b_ref[...]).astype(BF16)
    o_ref[0] = _dot(mn, w_ref[...]).astype(BF16)


def _mem_kv(mem, g, b, w):
    bsz, m, d = mem.shape
    n = w.shape[1]
    vmem = 2 * (m * d * 4 + d * n * 2 + m * n * 2) + 4 * m * d * 4
    return pl.pallas_call(
        _mem_kv_kernel,
        grid=(bsz,),
        in_specs=[
            pl.BlockSpec((1, m, d), lambda i: (i, 0, 0)),
            pl.BlockSpec((1, d), lambda i: (0, 0)),
            pl.BlockSpec((1, d), lambda i: (0, 0)),
            pl.BlockSpec((d, n), lambda i: (0, 0)),
        ],
        out_specs=pl.BlockSpec((1, m, n), lambda i: (i, 0, 0)),
        out_shape=jax.ShapeDtypeStruct((bsz, m, n), BF16),
        compiler_params=_params(("parallel",), vmem),
        name="mem_kv",
    )(mem, g, b, w)


def _fourier_tables(s, c, p):
    q = s // p
    ki = np.arange(p, dtype=np.int64)
    i = np.arange(p, dtype=np.int64)
    j = np.arange(q, dtype=np.int64)
    ph = (ki[None, :, None] * (q * i[None, None, :] + j[:, None, None])) % s
    ang = 2.0 * np.pi * ph.astype(np.float64) / s
    m1 = np.concatenate([np.cos(ang), -np.sin(ang)], axis=1)
    kj = np.arange(q, dtype=np.int64)
    a2 = 2.0 * np.pi * ((kj[:, None] * j[None, :]) % q).astype(np.float64) / q
    m2 = np.block([[np.cos(a2), np.sin(a2)], [-np.sin(a2), np.cos(a2)]])
    cc = np.arange(c, dtype=np.int64)
    a3 = 2.0 * np.pi * ((cc[:, None] * cc[None, :]) % c).astype(np.float64) / c
    mc = np.concatenate([np.cos(a3), np.sin(a3)], axis=0) / math.sqrt(s * c)
    return (jnp.asarray(m1, dtype=BF16), jnp.asarray(m2, dtype=BF16), jnp.asarray(mc, dtype=BF16))


SUBLANES = 8


def _fourier_pitch(q):
    return 2 * q + SUBLANES if (2 * q // SUBLANES) % 2 == 0 else 2 * q


def _fourier_kernel(x_ref, m1_ref, m2_ref, mc_ref, o_ref, ys_ref, *, p, q):
    grp = FOURIER_UNROLL
    pitch = _fourier_pitch(q)

    def stage1(jg, carry):
        js = [jg * grp + u for u in range(grp)]
        xs = [x_ref[pl.ds(j, p, stride=q), :].astype(BF16) for j in js]
        ys = [_dot(m1_ref[j], xj) for j, xj in zip(js, xs)]
        for j, y in zip(js, ys):
            ys_ref[pl.ds(j, p, stride=pitch), :] = y[:p]
            ys_ref[pl.ds(q + j, p, stride=pitch), :] = y[p:]
        return carry

    lax.fori_loop(0, q // grp, stage1, 0)

    def stage2(kg, carry):
        kis = [kg * grp + u for u in range(grp)]
        ys = [ys_ref[pl.ds(pl.multiple_of(ki * pitch, SUBLANES), 2 * q), :].astype(BF16) for ki in kis]
        zs = [_dot(m2_ref[...], y) for y in ys]
        zcs = [jnp.concatenate([z[:q], z[q:]], axis=1).astype(BF16) for z in zs]
        outs = [_dot(zc, mc_ref[...]) for zc in zcs]
        for ki, o in zip(kis, outs):
            o_ref[pl.ds(ki, q, stride=p), :] = o
        return carry

    lax.fori_loop(0, p // grp, stage2, 0)


def _fourier(proj, bsz, s, col_block0):
    c = FOURIER_GROUP_DIM
    groups = FOURIER_W // c
    p = min(FOURIER_P, s // 8)
    q = s // p
    m1, m2, mc = _fourier_tables(s, c, p)
    vmem = 2 * (s * c * 4 + m1.size * 2 + m2.size * 2 + mc.size * 2 + s * c * 4) + 2 * s * c * 4 + 8 * p * c * 4
    return pl.pallas_call(
        functools.partial(_fourier_kernel, p=p, q=q),
        grid=(bsz, groups),
        in_specs=[
            pl.BlockSpec((s, c), lambda b, g: (b, col_block0 + g)),
            pl.BlockSpec((q, 2 * p, p), lambda b, g: (0, 0, 0)),
            pl.BlockSpec((2 * q, 2 * q), lambda b, g: (0, 0)),
            pl.BlockSpec((2 * c, c), lambda b, g: (0, 0)),
        ],
        out_specs=pl.BlockSpec((s, c), lambda b, g: (b, g)),
        out_shape=jax.ShapeDtypeStruct((bsz * s, FOURIER_W), F32),
        scratch_shapes=[pltpu.VMEM((p * _fourier_pitch(q), c), F32)],
        compiler_params=_params(("parallel", "parallel"), vmem),
        name="fourier_mix",
    )(proj, m1, m2, mc)


def _cumsum_dot(ones_mask, x):
    n = x.shape[1]
    hi = x.astype(BF16)
    r1 = x - hi.astype(F32)
    mid = r1.astype(BF16)
    lo = (r1 - mid.astype(F32)).astype(BF16)
    parts = _dot(ones_mask, jnp.concatenate([hi, mid, lo], axis=1))
    return parts[:, 0:n] + parts[:, n:2 * n] + parts[:, 2 * n:3 * n]


def _hgrn_kernel(*refs, reverse, finalize, n_chunks):
    if finalize:
        q_ref, i_ref, z_ref, lb_ref, of_ref, g_ref, ng_ref, o_ref, st_ref = refs
    else:
        q_ref, i_ref, z_ref, lb_ref, o_ref, st_ref = refs

    @pl.when(pl.program_id(1) == 0)
    def _():
        st_ref[...] = jnp.zeros_like(st_ref)

    ck = HGRN_CHUNK
    hd = HGRN_HEAD_DIM
    row = lax.broadcasted_iota(jnp.int32, (ck, ck), 0)
    col = lax.broadcasted_iota(jnp.int32, (ck, ck), 1)
    tri = (col >= row) if reverse else (col <= row)
    cum = jnp.where(tri, 1.0, 0.0).astype(BF16)
    lb = lb_ref[...]

    order = list(range(n_chunks - 1, -1, -1) if reverse else range(n_chunks))
    heads = [slice(h * hd, (h + 1) * hd) for h in range(HGRN_HEADS)]
    rows = {c: pl.ds(c * ck, ck) for c in order}

    q_dec, k_inv, k_end, decay, vb = {}, {}, {}, {}, {}
    for c in order:
        f = lb + (1.0 - lb) * jax.nn.sigmoid(z_ref[rows[c], :])
        k = 1.0 - f
        a = _cumsum_dot(cum, jnp.log(f))
        a_end = a[0:1, :] if reverse else a[ck - 1:ck, :]
        q_dec[c] = (_silu(q_ref[rows[c], :]) * jnp.exp(a)).astype(BF16)
        k_inv[c] = (k * jnp.exp(-a)).astype(BF16)
        k_end[c] = (k * jnp.exp(a_end - a)).astype(BF16)
        decay[c] = jnp.exp(a_end)
        vb[c] = i_ref[rows[c], :].astype(BF16)

    scores = {(c, h): jnp.where(tri, _dot_nt(q_dec[c][:, sl], k_inv[c][:, sl]), 0.0).astype(BF16)
              for c in order for h, sl in enumerate(heads)}
    update = {(c, h): _dot_tn(vb[c][:, sl], k_end[c][:, sl]) for c in order for h, sl in enumerate(heads)}
    o_intra = {(c, h): _dot(scores[c, h], vb[c][:, sl]) for c in order for h, sl in enumerate(heads)}

    state_in = {}
    for h, sl in enumerate(heads):
        st = st_ref[h]
        for c in order:
            state_in[c, h] = st.astype(BF16)
            st = st * decay[c][:, sl] + update[c, h]
        st_ref[h] = st

    o_inter = {(c, h): _dot_nt(q_dec[c][:, sl], state_in[c, h]) for c in order for h, sl in enumerate(heads)}

    for c in order:
        outs = []
        for h, sl in enumerate(heads):
            o_h = o_intra[c, h] + o_inter[c, h]
            if finalize:
                o_h = o_h + of_ref[rows[c], sl]
                ms = jnp.mean(o_h * o_h, axis=-1, keepdims=True)
                o_h = o_h * lax.rsqrt(ms + LN_EPS) * ng_ref[:, sl]
            outs.append(o_h)
        o = jnp.concatenate(outs, axis=1)
        if finalize:
            o_ref[rows[c], :] = (o * _silu(g_ref[rows[c], :])).astype(o_ref.dtype)
        else:
            o_ref[rows[c], :] = o


def _hgrn_sweep(proj, lb, cols, bsz, s, *, reverse, o_fwd=None, norm_g=None):
    w = HGRN_W
    rb = min(HGRN_ROWS, s)
    nb = s // rb
    finalize = o_fwd is not None

    def rmap(b, n):
        return b * nb + (nb - 1 - n if reverse else n)

    def cmap(cb):
        return lambda b, n: (rmap(b, n), cb)

    in_specs = [pl.BlockSpec((rb, w), cmap(cols[0])), pl.BlockSpec((rb, w), cmap(cols[1])),
                pl.BlockSpec((rb, w), cmap(cols[2])), pl.BlockSpec((1, w), lambda b, n: (0, 0))]
    args = [proj, proj, proj, lb]
    if finalize:
        in_specs += [pl.BlockSpec((rb, w), cmap(0)), pl.BlockSpec((rb, w), cmap(cols[3])),
                     pl.BlockSpec((1, w), lambda b, n: (0, 0))]
        args += [o_fwd, proj, norm_g]
    vmem = 2 * (6 * rb * w * 4) + HGRN_HEADS * HGRN_HEAD_DIM * HGRN_HEAD_DIM * 4 + 24 * HGRN_CHUNK * w * 4
    return pl.pallas_call(
        functools.partial(_hgrn_kernel, reverse=reverse, finalize=finalize, n_chunks=rb // HGRN_CHUNK),
        grid=(bsz, nb),
        in_specs=in_specs,
        out_specs=pl.BlockSpec((rb, w), cmap(0)),
        out_shape=jax.ShapeDtypeStruct((bsz * s, w), BF16 if finalize else F32),
        scratch_shapes=[pltpu.VMEM((HGRN_HEADS, HGRN_HEAD_DIM, HGRN_HEAD_DIM), F32)],
        compiler_params=_params(("parallel", "arbitrary"), vmem),
        name="hgrn_bwd" if reverse else "hgrn_fwd",
    )(*args)


def _merge_kernel(x_ref, lg_ref, lbi_ref, gl_ref, xq_ref, fm_ref, ho_ref, kv_ref, wx_ref, wf_ref, bf_ref,
                  wh_ref, bg_ref, wo_ref, g1_ref, b1_ref, o_ref):
    d = x_ref.shape[1]
    h = _layer_norm(x_ref[...], lg_ref[...], lbi_ref[...])

    kv = kv_ref[0]
    xq = xq_ref[...].astype(BF16)
    key_sl = [slice(hh * XATTN_HEAD_DIM, (hh + 1) * XATTN_HEAD_DIM) for hh in range(XATTN_HEADS)]
    val_sl = [slice(XATTN_W + hh * XATTN_HEAD_DIM, XATTN_W + (hh + 1) * XATTN_HEAD_DIM) for hh in range(XATTN_HEADS)]
    logits = [_dot_nt(xq[:, sl], kv[:, sl]) for sl in key_sl]
    y_f = _dot(fm_ref[...].astype(BF16), wf_ref[...]) + bf_ref[...]
    y_h = _dot(ho_ref[...], wh_ref[...])
    probs = []
    for sc in logits:
        sc = sc * (XATTN_HEAD_DIM ** -0.5)
        sc = sc - jnp.max(sc, axis=-1, keepdims=True)
        e = jnp.exp(sc)
        probs.append((e / jnp.sum(e, axis=-1, keepdims=True)).astype(BF16))
    att = jnp.concatenate([_dot(pr, kv[:, sl]) for pr, sl in zip(probs, val_sl)], axis=1).astype(BF16)

    y_x = _dot(att, wx_ref[...])
    gate = jax.nn.sigmoid(gl_ref[...] + bg_ref[...])
    merged = gate[:, 0:d] * y_f + gate[:, d:2 * d] * y_h + gate[:, 2 * d:3 * d] * y_x
    y = _dot(merged.astype(BF16), wo_ref[...])
    o_ref[...] = _layer_norm(DEEPNORM_ALPHA * h + y, g1_ref[...], b1_ref[...])


def _merge(x2, ln_g, ln_b, proj, gate_cb, xq_cb, fm, ho, kv, w_xo, w_fo, b_fo, w_ho, b_gate, w_out, g1, b1, s):
    t, d = x2.shape
    tm = min(MERGE_TM, s)
    per_b = s // tm
    m, kvw = kv.shape[1], kv.shape[2]
    gw = N_BRANCHES * d

    def full(shape):
        return pl.BlockSpec(shape, lambda i: tuple(0 for _ in shape))

    vmem = (2 * (tm * d * 4 + tm * gw * 4 + tm * XATTN_W * 4 + tm * FOURIER_W * 4 + tm * HGRN_W * 2 + m * kvw * 2
                 + 3 * XATTN_W * d * 2 + d * d * 2 + tm * d * 4) + 10 * tm * d * 4 + 2 * tm * gw * 4)
    return pl.pallas_call(
        _merge_kernel,
        grid=(t // tm,),
        in_specs=[
            pl.BlockSpec((tm, d), lambda i: (i, 0)),
            full((1, d)), full((1, d)),
            pl.BlockSpec((tm, gw), lambda i: (i, gate_cb)),
            pl.BlockSpec((tm, XATTN_W), lambda i: (i, xq_cb)),
            pl.BlockSpec((tm, FOURIER_W), lambda i: (i, 0)),
            pl.BlockSpec((tm, HGRN_W), lambda i: (i, 0)),
            pl.BlockSpec((1, m, kvw), lambda i: (i // per_b, 0, 0)),
            full((XATTN_W, d)), full((FOURIER_W, d)), full((1, d)), full((HGRN_W, d)), full((1, gw)),
            full((d, d)), full((1, d)), full((1, d)),
        ],
        out_specs=pl.BlockSpec((tm, d), lambda i: (i, 0)),
        out_shape=jax.ShapeDtypeStruct((t, d), F32),
        compiler_params=_params(("parallel",), vmem),
        name="merge_out",
    )(x2, ln_g, ln_b, proj, proj, fm, ho, kv, w_xo, w_fo, b_fo, w_ho, b_gate, w_out, g1, b1)


def _route_kernel(h_ref, wr_ref, bias_ref, idx_ref, gw_ref, rank_ref, cnt_ref, carry_ref):
    tm = h_ref.shape[0]
    neg = -jnp.inf

    @pl.when(pl.program_id(0) == 0)
    def _():
        carry_ref[...] = jnp.zeros_like(carry_ref)

    aff = jax.nn.sigmoid(_dot_nt(wr_ref[...], h_ref[...].astype(BF16)))
    sel = aff + bias_ref[...]

    giota = lax.broadcasted_iota(jnp.int32, (GROUP_SIZE, tm), 0)
    scores = []
    for g in range(N_GROUPS):
        slab = sel[g * GROUP_SIZE:(g + 1) * GROUP_SIZE, :]
        m1 = jnp.max(slab, axis=0, keepdims=True)
        first = jnp.min(jnp.where(slab == m1, giota, GROUP_SIZE), axis=0, keepdims=True)
        m2 = jnp.max(jnp.where(giota == first, neg, slab), axis=0, keepdims=True)
        scores.append(m1 + m2)
    gs = jnp.concatenate(scores, axis=0)

    grow = lax.broadcasted_iota(jnp.int32, (N_GROUPS, tm), 0)
    gsel = jnp.zeros((N_GROUPS, tm), jnp.bool_)
    work = gs
    for _ in range(TOPK_GROUPS):
        m = jnp.max(work, axis=0, keepdims=True)
        first = jnp.min(jnp.where(work == m, grow, N_GROUPS), axis=0, keepdims=True)
        hit = grow == first
        gsel = jnp.logical_or(gsel, hit)
        work = jnp.where(hit, neg, work)

    masked = jnp.concatenate(
        [jnp.where(gsel[g:g + 1, :], sel[g * GROUP_SIZE:(g + 1) * GROUP_SIZE, :], neg) for g in range(N_GROUPS)],
        axis=0)

    erow = lax.broadcasted_iota(jnp.int32, (N_EXPERTS, tm), 0)
    ids, ws = [], []
    chosen = jnp.zeros((N_EXPERTS, tm), jnp.bool_)
    for _ in range(TOP_K):
        m = jnp.max(masked, axis=0, keepdims=True)
        first = jnp.min(jnp.where(masked == m, erow, N_EXPERTS), axis=0, keepdims=True)
        hit = erow == first
        ids.append(first)
        ws.append(jnp.sum(jnp.where(hit, aff, 0.0), axis=0, keepdims=True))
        masked = jnp.where(hit, neg, masked)
        chosen = jnp.logical_or(chosen, hit)
    w = jnp.concatenate(ws, axis=0)
    idx_ref[...] = jnp.concatenate(ids, axis=0)
    gw_ref[...] = w / jnp.sum(w, axis=0, keepdims=True) * ROUTED_SCALE

    chosen_f = jnp.where(chosen, 1.0, 0.0)
    srow = lax.broadcasted_iota(jnp.int32, (tm, tm), 0)
    scol = lax.broadcasted_iota(jnp.int32, (tm, tm), 1)
    before = jnp.where(srow < scol, 1.0, 0.0).astype(BF16)
    prefix = _dot(chosen_f.astype(BF16), before) + carry_ref[...]
    ranks = [jnp.sum(jnp.where(erow == ids[k], prefix, 0.0), axis=0, keepdims=True) for k in range(TOP_K)]
    rank_ref[...] = jnp.concatenate(ranks, axis=0).astype(jnp.int32)
    carry_ref[...] = carry_ref[...] + jnp.sum(chosen_f, axis=1, keepdims=True)
    cnt_ref[...] = carry_ref[...]


def _route(h1, w_router_t, bias_col):
    t, d = h1.shape
    tm = min(ROUTE_TM, t)
    vmem = 2 * (tm * d * 4 + N_EXPERTS * d * 2 + 3 * TOP_K * tm * 4) + tm * d * 2 + 16 * N_EXPERTS * tm * 4 + tm * tm * 8
    return pl.pallas_call(
        _route_kernel,
        grid=(t // tm,),
        in_specs=[
            pl.BlockSpec((tm, d), lambda i: (i, 0)),
            pl.BlockSpec((N_EXPERTS, d), lambda i: (0, 0)),
            pl.BlockSpec((N_EXPERTS, 1), lambda i: (0, 0)),
        ],
        out_specs=[pl.BlockSpec((TOP_K, tm), lambda i: (0, i)), pl.BlockSpec((TOP_K, tm), lambda i: (0, i)),
                   pl.BlockSpec((TOP_K, tm), lambda i: (0, i)), pl.BlockSpec((N_EXPERTS, 1), lambda i: (0, 0))],
        out_shape=[jax.ShapeDtypeStruct((TOP_K, t), jnp.int32), jax.ShapeDtypeStruct((TOP_K, t), F32),
                   jax.ShapeDtypeStruct((TOP_K, t), jnp.int32), jax.ShapeDtypeStruct((N_EXPERTS, 1), F32)],
        scratch_shapes=[pltpu.VMEM((N_EXPERTS, 1), F32)],
        compiler_params=_params(("arbitrary",), vmem),
        name="router",
    )(h1, w_router_t, bias_col)


def _dest_kernel(idx_ref, rank_ref, ps_ref, o_ref):
    tm = idx_ref.shape[1]
    erow = lax.broadcasted_iota(jnp.int32, (N_EXPERTS, tm), 0)
    ps = ps_ref[...]
    rows = []
    for k in range(TOP_K):
        base = jnp.sum(jnp.where(erow == idx_ref[k:k + 1, :], ps, 0.0), axis=0, keepdims=True)
        rows.append(base.astype(jnp.int32) + rank_ref[k:k + 1, :])
    o_ref[...] = jnp.concatenate(rows, axis=0)


def _slot_dest(idx_t, rank_t, pstart_col):
    t = idx_t.shape[1]
    tm = min(ROUTE_TM, t)
    vmem = 2 * (3 * TOP_K * tm * 4) + 6 * N_EXPERTS * tm * 4
    return pl.pallas_call(
        _dest_kernel,
        grid=(t // tm,),
        in_specs=[pl.BlockSpec((TOP_K, tm), lambda i: (0, i)), pl.BlockSpec((TOP_K, tm), lambda i: (0, i)),
                  pl.BlockSpec((N_EXPERTS, 1), lambda i: (0, 0))],
        out_specs=pl.BlockSpec((TOP_K, tm), lambda i: (0, i)),
        out_shape=jax.ShapeDtypeStruct((TOP_K, t), jnp.int32),
        compiler_params=_params(("parallel",), vmem),
        name="slot_dest",
    )(idx_t, rank_t, pstart_col)


LANES = 128
U32 = jnp.uint32
HIGH_HALF = np.uint32(0xFFFF0000)
N_DMA_PRIORITIES = 2


def _tile_rows(d):
    return d // (2 * LANES)


def _pack_tiles(tile_ref, value, base=0):
    n, d = value.shape
    sub = _tile_rows(d)
    bits = lax.bitcast_convert_type(value.astype(BF16).astype(F32), U32)
    for c in range(sub):
        low = bits[:, c * LANES:(c + 1) * LANES] >> 16
        high = bits[:, (c + sub) * LANES:(c + sub + 1) * LANES] & HIGH_HALF
        tile_ref[pl.ds(base + c, n, stride=sub), :] = low | high


def _unpack_tiles(tile_ref, n, d, base=0):
    sub = _tile_rows(d)
    words = [tile_ref[pl.ds(base + c, n, stride=sub), :] for c in range(sub)]
    lows = [lax.bitcast_convert_type(u << 16, F32) for u in words]
    highs = [lax.bitcast_convert_type(u & HIGH_HALF, F32) for u in words]
    return jnp.concatenate(lows + highs, axis=1)


def _dispatch_kernel(pend_ref, padded_ref, nu_ref, dest_ref, h_ref, xs_ref, tile_ref, zero_ref, sem, zsem):
    tm, d = h_ref.shape
    sub = _tile_rows(d)
    zrows = zero_ref.shape[0]
    n_blocks = xs_ref.shape[0] // zrows

    @pl.when(pl.program_id(0) == 0)
    def _():
        zero_ref[...] = jnp.zeros_like(zero_ref)

        def zero_block(first_row):
            return pltpu.make_async_copy(zero_ref, xs_ref.at[pl.ds(pl.multiple_of(first_row, 8), zrows), :], zsem)

        def issue(e, carry):
            @pl.when(padded_ref[e] > 0)
            def _():
                zero_block(pend_ref[e] * sub - zrows).start()
            return carry

        def drain(e, carry):
            @pl.when(padded_ref[e] > 0)
            def _():
                zero_block(pend_ref[e] * sub - zrows).wait()
            return carry

        def issue_tail(blk, carry):
            zero_block(blk * zrows).start()
            return carry

        def drain_tail(blk, carry):
            zero_block(blk * zrows).wait()
            return carry

        lax.fori_loop(0, N_EXPERTS, issue, 0)
        lax.fori_loop(nu_ref[0], n_blocks, issue_tail, 0)
        lax.fori_loop(0, N_EXPERTS, drain, 0)
        lax.fori_loop(nu_ref[0], n_blocks, drain_tail, 0)

    i = pl.program_id(0)
    slot = lax.rem(i, 2)
    _pack_tiles(tile_ref.at[slot], h_ref[...])

    def row_copy(tok, dst_row):
        return pltpu.make_async_copy(tile_ref.at[slot, pl.ds(pl.multiple_of(tok * sub, sub), sub), :],
                                     xs_ref.at[pl.ds(pl.multiple_of(dst_row * sub, sub), sub), :], sem.at[slot])

    def issue(tok, carry):
        for k in range(TOP_K):
            row_copy(tok, dest_ref[tok * TOP_K + k]).start(priority=k % N_DMA_PRIORITIES)
        return carry

    lax.fori_loop(0, tm, issue, 0)

    def wait_tile(s):
        for _ in range(TOP_K):
            pltpu.make_async_copy(tile_ref.at[s], xs_ref.at[pl.ds(0, tm * sub), :], sem.at[s]).wait()

    @pl.when(i > 0)
    def _():
        wait_tile(1 - slot)

    @pl.when(i == pl.num_programs(0) - 1)
    def _():
        wait_tile(slot)


def _dispatch(h1, dest_flat, pend, padded, n_used, n_rows_pad):
    t, d = h1.shape
    sub = _tile_rows(d)
    tm = min(DISPATCH_TM, t)
    vmem = 2 * (tm * d * 4) + tm * d * 2 + MOE_BM * d * 2 + 4 * tm * d * 4
    return pl.pallas_call(
        _dispatch_kernel,
        grid_spec=pltpu.PrefetchScalarGridSpec(
            num_scalar_prefetch=3,
            grid=(t // tm,),
            in_specs=[
                pl.BlockSpec((tm * TOP_K,), lambda i, pe, pa, nu: (i,), memory_space=pltpu.SMEM),
                pl.BlockSpec((tm, d), lambda i, pe, pa, nu: (i, 0)),
            ],
            out_specs=pl.BlockSpec(memory_space=pl.ANY),
            scratch_shapes=[pltpu.VMEM((2, tm * sub, LANES), U32), pltpu.VMEM((MOE_BM * sub, LANES), U32),
                            pltpu.SemaphoreType.DMA((2,)), pltpu.SemaphoreType.DMA(())],
        ),
        out_shape=jax.ShapeDtypeStruct((n_rows_pad * sub, LANES), U32),
        compiler_params=_params(("arbitrary",), vmem),
        name="dispatch_rows",
    )(pend, padded, n_used, dest_flat, h1)


def _expert_kernel(ps_ref, pd_ref, nu_ref, wg_ref, wu_ref, wd_ref, x_hbm, y_hbm,
                   xbuf, ybuf, wgb_ref, wub_ref, wdb_ref, xsem, ysem):
    e = pl.program_id(0)
    d = wg_ref.shape[1]
    slots = xbuf.shape[0]
    ahead = slots - 1
    rows = xbuf.shape[1]
    bm = rows // _tile_rows(d)
    n_blocks = y_hbm.shape[0] // rows
    n_used = nu_ref[0]
    first = ps_ref[e] // bm
    nb = pd_ref[e] // bm

    def x_copy(g, slot):
        return pltpu.make_async_copy(x_hbm.at[pl.ds(pl.multiple_of(g * rows, rows), rows), :], xbuf.at[slot],
                                     xsem.at[slot])

    def y_copy(g, slot):
        return pltpu.make_async_copy(ybuf.at[slot], y_hbm.at[pl.ds(pl.multiple_of(g * rows, rows), rows), :],
                                     ysem.at[slot])

    @pl.when(e == 0)
    def _():
        for g0 in range(ahead):
            @pl.when(g0 < n_used)
            def _():
                x_copy(g0, g0).start()

    @pl.when(nb > 0)
    def _():
        wgb_ref[...] = wg_ref[0].astype(BF16)
        wub_ref[...] = wu_ref[0].astype(BF16)
        wdb_ref[...] = wd_ref[0].astype(BF16)

        def block(j, carry):
            g = first + j
            slot = lax.rem(g, slots)
            x_copy(g, slot).wait()

            @pl.when(g + ahead < n_used)
            def _():
                x_copy(g + ahead, lax.rem(g + ahead, slots)).start()

            @pl.when(g >= slots)
            def _():
                y_copy(g - slots, slot).wait()

            xb = _unpack_tiles(xbuf.at[slot], bm, d).astype(BF16)
            hb = (_silu(_dot(xb, wgb_ref[...])) * _dot(xb, wub_ref[...])).astype(BF16)
            _pack_tiles(ybuf.at[slot], _dot(hb, wdb_ref[...]))
            y_copy(g, slot).start()
            return carry

        lax.fori_loop(0, nb, block, 0)

    @pl.when(e == pl.num_programs(0) - 1)
    def _():
        def drain_used(g, carry):
            y_copy(g, lax.rem(g, slots)).wait()
            return carry

        lax.fori_loop(jnp.maximum(n_used - slots, 0), n_used, drain_used, 0)
        ybuf[0] = jnp.zeros_like(ybuf[0])

        def issue(g, carry):
            y_copy(g, 0).start()
            return carry

        def drain(g, carry):
            y_copy(g, 0).wait()
            return carry

        lax.fori_loop(n_used, n_blocks, issue, 0)
        lax.fori_loop(n_used, n_blocks, drain, 0)


def _experts(x_sorted, pstart, padded, n_used, w_gate, w_up, w_down):
    n_exp, d, ff = w_gate.shape
    sub = _tile_rows(d)
    bm = MOE_BM
    vmem = 2 * (3 * d * ff * 4) + 3 * d * ff * 2 + 2 * EXPERT_SLOTS * bm * d * 2 + 6 * bm * ff * 4 + 6 * bm * d * 4
    return pl.pallas_call(
        _expert_kernel,
        grid_spec=pltpu.PrefetchScalarGridSpec(
            num_scalar_prefetch=3,
            grid=(n_exp,),
            in_specs=[
                pl.BlockSpec((1, d, ff), lambda e, ps, pd, nu: (e, 0, 0)),
                pl.BlockSpec((1, d, ff), lambda e, ps, pd, nu: (e, 0, 0)),
                pl.BlockSpec((1, ff, d), lambda e, ps, pd, nu: (e, 0, 0)),
                pl.BlockSpec(memory_space=pl.ANY),
            ],
            out_specs=pl.BlockSpec(memory_space=pl.ANY),
            scratch_shapes=[pltpu.VMEM((EXPERT_SLOTS, bm * sub, LANES), U32),
                            pltpu.VMEM((EXPERT_SLOTS, bm * sub, LANES), U32),
                            pltpu.VMEM((d, ff), BF16), pltpu.VMEM((d, ff), BF16), pltpu.VMEM((ff, d), BF16),
                            pltpu.SemaphoreType.DMA((EXPERT_SLOTS,)), pltpu.SemaphoreType.DMA((EXPERT_SLOTS,))],
        ),
        out_shape=jax.ShapeDtypeStruct(x_sorted.shape, U32),
        compiler_params=_params(("arbitrary",), vmem),
        name="expert_mlp",
    )(pstart, padded, n_used, w_gate, w_up, w_down, x_sorted)


def _final_kernel(dest_ref, dest_next_ref, h_ref, gw_ref, y_ref, wg_ref, wu_ref, wd_ref, g_ref, b_ref, o_ref,
                  rows_ref, sem):
    i = pl.program_id(0)
    tm, d = h_ref.shape
    sub = _tile_rows(d)
    slot = lax.rem(i, 2)

    def issue_tile(dref, s):
        def row_copy(tok, k, src_row):
            return pltpu.make_async_copy(
                y_ref.at[pl.ds(pl.multiple_of(src_row * sub, sub), sub), :],
                rows_ref.at[s, pl.ds(pl.multiple_of((k * tm + tok) * sub, sub), sub), :], sem.at[s])

        def issue(tok, carry):
            for k in range(TOP_K):
                row_copy(tok, k, dref[tok * TOP_K + k]).start(priority=k % N_DMA_PRIORITIES)
            return carry

        lax.fori_loop(0, tm, issue, 0, unroll=COMBINE_ISSUE_UNROLL)

    @pl.when(i == 0)
    def _():
        issue_tile(dest_ref, 0)

    @pl.when(i + 1 < pl.num_programs(0))
    def _():
        issue_tile(dest_next_ref, 1 - slot)

    h = h_ref[...]
    hb = h.astype(BF16)
    shared = _dot((_silu(_dot(hb, wg_ref[...])) * _dot(hb, wu_ref[...])).astype(BF16), wd_ref[...])

    pltpu.make_async_copy(y_ref.at[pl.ds(0, TOP_K * tm * sub), :], rows_ref.at[slot], sem.at[slot]).wait()

    rows = rows_ref.at[slot]
    gw = gw_ref[...]
    routed = gw[:, 0:1] * _unpack_tiles(rows, tm, d)
    for k in range(1, TOP_K):
        routed = routed + gw[:, k:k + 1] * _unpack_tiles(rows, tm, d, base=k * tm * sub)
    o_ref[...] = _layer_norm(DEEPNORM_ALPHA * h + (routed + shared), g_ref[...], b_ref[...])


def _final(h1, dest_flat, gw, y_sorted, w_sg, w_su, w_sd, g2, b2):
    t, d = h1.shape
    ff = w_sg.shape[1]
    sub = _tile_rows(d)
    tm = min(FINAL_TM, t)
    nt = t // tm
    vmem = 2 * (2 * tm * d * 4 + tm * LANES * 4 + 3 * d * ff * 2) + 2 * TOP_K * tm * d * 2 + 8 * tm * d * 4

    def full(shape):
        return pl.BlockSpec(shape, lambda i: tuple(0 for _ in shape))

    return pl.pallas_call(
        _final_kernel,
        grid=(nt,),
        in_specs=[
            pl.BlockSpec((tm * TOP_K,), lambda i: (i,), memory_space=pltpu.SMEM),
            pl.BlockSpec((tm * TOP_K,), lambda i: (jnp.minimum(i + 1, nt - 1),), memory_space=pltpu.SMEM),
            pl.BlockSpec((tm, d), lambda i: (i, 0)),
            pl.BlockSpec((tm, TOP_K), lambda i: (i, 0)),
            pl.BlockSpec(memory_space=pl.ANY),
            full((d, ff)), full((d, ff)), full((ff, d)), full((1, d)), full((1, d)),
        ],
        out_specs=pl.BlockSpec((tm, d), lambda i: (i, 0)),
        out_shape=jax.ShapeDtypeStruct((t, d), F32),
        scratch_shapes=[pltpu.VMEM((2, TOP_K * tm * sub, LANES), U32), pltpu.SemaphoreType.DMA((2,))],
        compiler_params=_params(("arbitrary",), vmem),
        name="combine_final",
    )(dest_flat, dest_flat, h1, gw, y_sorted, w_sg, w_su, w_sd, g2, b2)


def _lower_bound(p):
    return jnp.cumsum(jax.nn.softmax(p.astype(F32), axis=0), axis=0)[0:1]


def _block_plan(counts, n_rows):
    bm = MOE_BM
    counts = counts.reshape(-1).astype(jnp.int32)
    padded = (counts + bm - 1) // bm * bm
    pend = jnp.cumsum(padded).astype(jnp.int32)
    pstart = pend - padded
    n_blocks = n_rows // bm + N_EXPERTS
    n_used = pend[-1:] // bm
    return pstart, pend, padded, n_used.astype(jnp.int32), n_blocks


def kernel(x, mem, ln_in_g, ln_in_b, ln_mem_g, ln_mem_b, hgrn_lb_fwd, hgrn_lb_bwd, w_in, b_gate, hgrn_norm_g, w_mem_kv, w_fourier_o, b_fourier_o, w_hgrn_o, w_xattn_o, w_out, ln1_g, ln1_b, w_router, router_bias, w_exp_gate, w_exp_up, w_exp_down, w_sh_gate, w_sh_up, w_sh_down, ln2_g, ln2_b):
    bsz, s, d = x.shape
    t = bsz * s
    l = 0
    row = lambda v: v.reshape(1, -1).astype(F32)

    n_gate = N_BRANCHES * d
    n_rest = w_in.shape[2] - n_gate
    w_in_r = jnp.concatenate([w_in[l][:, n_rest:], w_in[l][:, :n_rest]], axis=1).astype(BF16)
    gate_blocks = n_gate // INPROJ_TN
    cb_fourier = gate_blocks
    cb_hq, cb_hi, cb_zf, cb_zb, cb_hg, cb_xq = (gate_blocks + 1 + n for n in range(6))

    x2 = x.reshape(t, d)
    proj = _inproj(x2, row(ln_in_g), row(ln_in_b), w_in_r)
    kv = _mem_kv(mem, row(ln_mem_g), row(ln_mem_b), w_mem_kv[l].astype(BF16))

    fm = _fourier(proj, bsz, s, cb_fourier * (INPROJ_TN // FOURIER_GROUP_DIM))

    lb_f = _lower_bound(hgrn_lb_fwd)
    lb_b = _lower_bound(hgrn_lb_bwd)
    o_fwd = _hgrn_sweep(proj, lb_f, (cb_hq, cb_hi, cb_zf), bsz, s, reverse=False)
    ho = _hgrn_sweep(proj, lb_b, (cb_hq, cb_hi, cb_zb, cb_hg), bsz, s, reverse=True,
                     o_fwd=o_fwd, norm_g=row(hgrn_norm_g[l]))

    h1 = _merge(x2, row(ln_in_g), row(ln_in_b), proj, 0, cb_xq, fm, ho, kv,
                w_xattn_o[l].astype(BF16), w_fourier_o[l].astype(BF16), row(b_fourier_o[l]),
                w_hgrn_o[l].astype(BF16), row(b_gate[l]), w_out[l].astype(BF16), row(ln1_g[l]), row(ln1_b[l]), s)

    idx_t, gw_t, rank_t, counts = _route(h1, w_router[l].T.astype(BF16),
                                         router_bias[l].reshape(N_EXPERTS, 1).astype(F32))
    pstart, pend, padded, n_used, n_blocks = _block_plan(counts, t * TOP_K)
    dest_t = _slot_dest(idx_t, rank_t, pstart.astype(F32).reshape(N_EXPERTS, 1))
    dest_flat = dest_t.T.reshape(-1)

    x_sorted = _dispatch(h1, dest_flat, pend, padded, n_used, n_blocks * MOE_BM)
    y_sorted = _experts(x_sorted, pstart, padded, n_used, w_exp_gate[l], w_exp_up[l], w_exp_down[l])
    out = _final(h1, dest_flat, gw_t.T, y_sorted, w_sh_gate[l].astype(BF16), w_sh_up[l].astype(BF16),
                 w_sh_down[l].astype(BF16), row(ln2_g[l]), row(ln2_b[l]))
    return out.reshape(bsz, s, d)
```

```python
import functools
import math

import numpy as np
import jax
import jax.numpy as jnp
from jax import lax
from jax.experimental import pallas as pl
from jax.experimental.pallas import tpu as pltpu

F32 = jnp.float32
BF16 = jnp.bfloat16

LN_EPS = 1e-5
DEPTH = 1
DEEPNORM_ALPHA = (2 * DEPTH) ** 0.25
FOURIER_GROUP_DIM = 128
FOURIER_W = 512
HGRN_HEADS = 4
HGRN_HEAD_DIM = 128
HGRN_W = HGRN_HEADS * HGRN_HEAD_DIM
HGRN_CHUNK = 64
XATTN_HEADS = 4
XATTN_HEAD_DIM = 128
XATTN_W = XATTN_HEADS * XATTN_HEAD_DIM
N_BRANCHES = 3
N_EXPERTS = 256
TOP_K = 8
N_GROUPS = 8
TOPK_GROUPS = 4
GROUP_SIZE = N_EXPERTS // N_GROUPS
ROUTED_SCALE = 2.5

V7X_SCOPED_VMEM_CAP_BYTES = 60000 * 1024

INPROJ_TM = 2048
INPROJ_TN = 512
FOURIER_P = 128
FOURIER_UNROLL = 8
HGRN_ROWS = 512
MERGE_TM = 256
ROUTE_TM = 512
MOE_BM = 256
EXPERT_SLOTS = 4
DISPATCH_TM = 256
FINAL_TM = 256
COMBINE_GROUP = 32


def _params(semantics, vmem_bytes):
    return pltpu.CompilerParams(dimension_semantics=semantics,
                                vmem_limit_bytes=int(min(vmem_bytes, V7X_SCOPED_VMEM_CAP_BYTES)))


def _layer_norm(x, g, b):
    mu = jnp.mean(x, axis=-1, keepdims=True)
    xc = x - mu
    var = jnp.mean(xc * xc, axis=-1, keepdims=True)
    return xc * lax.rsqrt(var + LN_EPS) * g + b


def _silu(x):
    return x * jax.nn.sigmoid(x)


def _dot(a, b):
    return jnp.dot(a, b, preferred_element_type=F32)


def _dot_nt(a, b):
    return lax.dot_general(a, b, (((1,), (1,)), ((), ())), preferred_element_type=F32)


def _dot_tn(a, b):
    return lax.dot_general(a, b, (((0,), (0,)), ((), ())), preferred_element_type=F32)


def _inproj_kernel(x_ref, g_ref, b_ref, w_ref, o_ref, hn_ref):
    @pl.when(pl.program_id(1) == 0)
    def _():
        hn_ref[...] = _layer_norm(x_ref[...], g_ref[...], b_ref[...]).astype(BF16)

    o_ref[...] = _dot(hn_ref[...], w_ref[...])


def _inproj(x2, g, b, w):
    t, d = x2.shape
    n = w.shape[1]
    tm, tn = min(INPROJ_TM, t), INPROJ_TN
    vmem = 2 * (tm * d * 4 + d * tn * 2 + tm * tn * 4) + tm * d * 2 + 3 * tm * d * 4
    return pl.pallas_call(
        _inproj_kernel,
        grid=(t // tm, n // tn),
        in_specs=[
            pl.BlockSpec((tm, d), lambda i, j: (i, 0)),
            pl.BlockSpec((1, d), lambda i, j: (0, 0)),
            pl.BlockSpec((1, d), lambda i, j: (0, 0)),
            pl.BlockSpec((d, tn), lambda i, j: (0, j)),
        ],
        out_specs=pl.BlockSpec((tm, tn), lambda i, j: (i, j)),
        out_shape=jax.ShapeDtypeStruct((t, n), F32),
        scratch_shapes=[pltpu.VMEM((tm, d), BF16)],
        compiler_params=_params(("parallel", "arbitrary"), vmem),
        name="ln_inproj",
    )(x2, g, b, w)


def _mem_kv_kernel(m_ref, g_ref, b_ref, w_ref, o_ref):
    mn = _layer_norm(m_ref[0], g_ref[...], b_ref[...]).astype(BF16)
    o_ref[0] = _dot(mn, w_ref[...]).astype(BF16)


def _mem_kv(mem, g, b, w):
    bsz, m, d = mem.shape
    n = w.shape[1]
    vmem = 2 * (m * d * 4 + d * n * 2 + m * n * 2) + 4 * m * d * 4
    return pl.pallas_call(
        _mem_kv_kernel,
        grid=(bsz,),
        in_specs=[
            pl.BlockSpec((1, m, d), lambda i: (i, 0, 0)),
            pl.BlockSpec((1, d), lambda i: (0, 0)),
            pl.BlockSpec((1, d), lambda i: (0, 0)),
            pl.BlockSpec((d, n), lambda i: (0, 0)),
        ],
        out_specs=pl.BlockSpec((1, m, n), lambda i: (i, 0, 0)),
        out_shape=jax.ShapeDtypeStruct((bsz, m, n), BF16),
        compiler_params=_params(("parallel",), vmem),
        name="mem_kv",
    )(mem, g, b, w)


def _fourier_tables(s, c, p):
    q = s // p
    ki = np.arange(p, dtype=np.int64)
    i = np.arange(p, dtype=np.int64)
    j = np.arange(q, dtype=np.int64)
    ph = (ki[None, :, None] * (q * i[None, None, :] + j[:, None, None])) % s
    ang = 2.0 * np.pi * ph.astype(np.float64) / s
    m1 = np.concatenate([np.cos(ang), -np.sin(ang)], axis=1)
    kj = np.arange(q, dtype=np.int64)
    a2 = 2.0 * np.pi * ((kj[:, None] * j[None, :]) % q).astype(np.float64) / q
    m2 = np.block([[np.cos(a2), np.sin(a2)], [-np.sin(a2), np.cos(a2)]])
    cc = np.arange(c, dtype=np.int64)
    a3 = 2.0 * np.pi * ((cc[:, None] * cc[None, :]) % c).astype(np.float64) / c
    mc = np.concatenate([np.cos(a3), np.sin(a3)], axis=0) / math.sqrt(s * c)
    return (jnp.asarray(m1, dtype=BF16), jnp.asarray(m2, dtype=BF16), jnp.asarray(mc, dtype=BF16))


SUBLANES = 8


def _fourier_pitch(q):
    return 2 * q + SUBLANES if (2 * q // SUBLANES) % 2 == 0 else 2 * q


def _fourier_kernel(x_ref, m1_ref, m2_ref, mc_ref, o_ref, ys_ref, *, p, q):
    grp = FOURIER_UNROLL
    pitch = _fourier_pitch(q)

    def stage1(jg, carry):
        js = [jg * grp + u for u in range(grp)]
        xs = [x_ref[pl.ds(j, p, stride=q), :].astype(BF16) for j in js]
        ys = [_dot(m1_ref[j], xj) for j, xj in zip(js, xs)]
        for j, y in zip(js, ys):
            ys_ref[pl.ds(j, p, stride=pitch), :] = y[:p]
            ys_ref[pl.ds(q + j, p, stride=pitch), :] = y[p:]
        return carry

    lax.fori_loop(0, q // grp, stage1, 0)

    def stage2(kg, carry):
        kis = [kg * grp + u for u in range(grp)]
        ys = [ys_ref[pl.ds(pl.multiple_of(ki * pitch, SUBLANES), 2 * q), :].astype(BF16) for ki in kis]
        zs = [_dot(m2_ref[...], y) for y in ys]
        zcs = [jnp.concatenate([z[:q], z[q:]], axis=1).astype(BF16) for z in zs]
        outs = [_dot(zc, mc_ref[...]) for zc in zcs]
        for ki, o in zip(kis, outs):
            o_ref[pl.ds(ki, q, stride=p), :] = o
        return carry

    lax.fori_loop(0, p // grp, stage2, 0)


def _fourier(proj, bsz, s, col_block0):
    c = FOURIER_GROUP_DIM
    groups = FOURIER_W // c
    p = min(FOURIER_P, s // 8)
    q = s // p
    m1, m2, mc = _fourier_tables(s, c, p)
    vmem = 2 * (s * c * 4 + m1.size * 2 + m2.size * 2 + mc.size * 2 + s * c * 4) + 2 * s * c * 4 + 8 * p * c * 4
    return pl.pallas_call(
        functools.partial(_fourier_kernel, p=p, q=q),
        grid=(bsz, groups),
        in_specs=[
            pl.BlockSpec((s, c), lambda b, g: (b, col_block0 + g)),
            pl.BlockSpec((q, 2 * p, p), lambda b, g: (0, 0, 0)),
            pl.BlockSpec((2 * q, 2 * q), lambda b, g: (0, 0)),
            pl.BlockSpec((2 * c, c), lambda b, g: (0, 0)),
        ],
        out_specs=pl.BlockSpec((s, c), lambda b, g: (b, g)),
        out_shape=jax.ShapeDtypeStruct((bsz * s, FOURIER_W), F32),
        scratch_shapes=[pltpu.VMEM((p * _fourier_pitch(q), c), F32)],
        compiler_params=_params(("parallel", "parallel"), vmem),
        name="fourier_mix",
    )(proj, m1, m2, mc)


def _cumsum_dot(ones_mask, x):
    n = x.shape[1]
    hi = x.astype(BF16)
    r1 = x - hi.astype(F32)
    mid = r1.astype(BF16)
    lo = (r1 - mid.astype(F32)).astype(BF16)
    parts = _dot(ones_mask, jnp.concatenate([hi, mid, lo], axis=1))
    return parts[:, 0:n] + parts[:, n:2 * n] + parts[:, 2 * n:3 * n]


def _hgrn_kernel(*refs, reverse, finalize, n_chunks):
    if finalize:
        q_ref, i_ref, z_ref, lb_ref, of_ref, g_ref, ng_ref, o_ref, st_ref = refs
    else:
        q_ref, i_ref, z_ref, lb_ref, o_ref, st_ref = refs

    @pl.when(pl.program_id(1) == 0)
    def _():
        st_ref[...] = jnp.zeros_like(st_ref)

    ck = HGRN_CHUNK
    hd = HGRN_HEAD_DIM
    row = lax.broadcasted_iota(jnp.int32, (ck, ck), 0)
    col = lax.broadcasted_iota(jnp.int32, (ck, ck), 1)
    tri = (col >= row) if reverse else (col <= row)
    cum = jnp.where(tri, 1.0, 0.0).astype(BF16)
    lb = lb_ref[...]

    order = list(range(n_chunks - 1, -1, -1) if reverse else range(n_chunks))
    heads = [slice(h * hd, (h + 1) * hd) for h in range(HGRN_HEADS)]
    rows = {c: pl.ds(c * ck, ck) for c in order}

    q_dec, k_inv, k_end, decay, vb = {}, {}, {}, {}, {}
    for c in order:
        f = lb + (1.0 - lb) * jax.nn.sigmoid(z_ref[rows[c], :])
        k = 1.0 - f
        a = _cumsum_dot(cum, jnp.log(f))
        a_end = a[0:1, :] if reverse else a[ck - 1:ck, :]
        q_dec[c] = (_silu(q_ref[rows[c], :]) * jnp.exp(a)).astype(BF16)
        k_inv[c] = (k * jnp.exp(-a)).astype(BF16)
        k_end[c] = (k * jnp.exp(a_end - a)).astype(BF16)
        decay[c] = jnp.exp(a_end)
        vb[c] = i_ref[rows[c], :].astype(BF16)

    scores = {(c, h): jnp.where(tri, _dot_nt(q_dec[c][:, sl], k_inv[c][:, sl]), 0.0).astype(BF16)
              for c in order for h, sl in enumerate(heads)}
    update = {(c, h): _dot_tn(vb[c][:, sl], k_end[c][:, sl]) for c in order for h, sl in enumerate(heads)}
    o_intra = {(c, h): _dot(scores[c, h], vb[c][:, sl]) for c in order for h, sl in enumerate(heads)}

    state_in = {}
    for h, sl in enumerate(heads):
        st = st_ref[h]
        for c in order:
            state_in[c, h] = st.astype(BF16)
            st = st * decay[c][:, sl] + update[c, h]
        st_ref[h] = st

    o_inter = {(c, h): _dot_nt(q_dec[c][:, sl], state_in[c, h]) for c in order for h, sl in enumerate(heads)}

    for c in order:
        outs = []
        for h, sl in enumerate(heads):
            o_h = o_intra[c, h] + o_inter[c, h]
            if finalize:
                o_h = o_h + of_ref[rows[c], sl]
                ms = jnp.mean(o_h * o_h, axis=-1, keepdims=True)
                o_h = o_h * lax.rsqrt(ms + LN_EPS) * ng_ref[:, sl]
            outs.append(o_h)
        o = jnp.concatenate(outs, axis=1)
        if finalize:
            o_ref[rows[c], :] = (o * _silu(g_ref[rows[c], :])).astype(o_ref.dtype)
        else:
            o_ref[rows[c], :] = o


def _hgrn_sweep(proj, lb, cols, bsz, s, *, reverse, o_fwd=None, norm_g=None):
    w = HGRN_W
    rb = min(HGRN_ROWS, s)
    nb = s // rb
    finalize = o_fwd is not None

    def rmap(b, n):
        return b * nb + (nb - 1 - n if reverse else n)

    def cmap(cb):
        return lambda b, n: (rmap(b, n), cb)

    in_specs = [pl.BlockSpec((rb, w), cmap(cols[0])), pl.BlockSpec((rb, w), cmap(cols[1])),
                pl.BlockSpec((rb, w), cmap(cols[2])), pl.BlockSpec((1, w), lambda b, n: (0, 0))]
    args = [proj, proj, proj, lb]
    if finalize:
        in_specs += [pl.BlockSpec((rb, w), cmap(0)), pl.BlockSpec((rb, w), cmap(cols[3])),
                     pl.BlockSpec((1, w), lambda b, n: (0, 0))]
        args += [o_fwd, proj, norm_g]
    vmem = 2 * (6 * rb * w * 4) + HGRN_HEADS * HGRN_HEAD_DIM * HGRN_HEAD_DIM * 4 + 24 * HGRN_CHUNK * w * 4
    return pl.pallas_call(
        functools.partial(_hgrn_kernel, reverse=reverse, finalize=finalize, n_chunks=rb // HGRN_CHUNK),
        grid=(bsz, nb),
        in_specs=in_specs,
        out_specs=pl.BlockSpec((rb, w), cmap(0)),
        out_shape=jax.ShapeDtypeStruct((bsz * s, w), BF16 if finalize else F32),
        scratch_shapes=[pltpu.VMEM((HGRN_HEADS, HGRN_HEAD_DIM, HGRN_HEAD_DIM), F32)],
        compiler_params=_params(("parallel", "arbitrary"), vmem),
        name="hgrn_bwd" if reverse else "hgrn_fwd",
    )(*args)


def _merge_kernel(x_ref, lg_ref, lbi_ref, gl_ref, xq_ref, fm_ref, ho_ref, kv_ref, wx_ref, wf_ref, bf_ref,
                  wh_ref, bg_ref, wo_ref, g1_ref, b1_ref, o_ref):
    d = x_ref.shape[1]
    h = _layer_norm(x_ref[...], lg_ref[...], lbi_ref[...])

    kv = kv_ref[0]
    xq = xq_ref[...].astype(BF16)
    key_sl = [slice(hh * XATTN_HEAD_DIM, (hh + 1) * XATTN_HEAD_DIM) for hh in range(XATTN_HEADS)]
    val_sl = [slice(XATTN_W + hh * XATTN_HEAD_DIM, XATTN_W + (hh + 1) * XATTN_HEAD_DIM) for hh in range(XATTN_HEADS)]
    logits = [_dot_nt(xq[:, sl], kv[:, sl]) for sl in key_sl]
    y_f = _dot(fm_ref[...].astype(BF16), wf_ref[...]) + bf_ref[...]
    y_h = _dot(ho_ref[...], wh_ref[...])
    probs = []
    for sc in logits:
        sc = sc * (XATTN_HEAD_DIM ** -0.5)
        sc = sc - jnp.max(sc, axis=-1, keepdims=True)
        e = jnp.exp(sc)
        probs.append((e / jnp.sum(e, axis=-1, keepdims=True)).astype(BF16))
    att = jnp.concatenate([_dot(pr, kv[:, sl]) for pr, sl in zip(probs, val_sl)], axis=1).astype(BF16)

    y_x = _dot(att, wx_ref[...])
    gate = jax.nn.sigmoid(gl_ref[...] + bg_ref[...])
    merged = gate[:, 0:d] * y_f + gate[:, d:2 * d] * y_h + gate[:, 2 * d:3 * d] * y_x
    y = _dot(merged.astype(BF16), wo_ref[...])
    o_ref[...] = _layer_norm(DEEPNORM_ALPHA * h + y, g1_ref[...], b1_ref[...])


def _merge(x2, ln_g, ln_b, proj, gate_cb, xq_cb, fm, ho, kv, w_xo, w_fo, b_fo, w_ho, b_gate, w_out, g1, b1, s):
    t, d = x2.shape
    tm = min(MERGE_TM, s)
    per_b = s // tm
    m, kvw = kv.shape[1], kv.shape[2]
    gw = N_BRANCHES * d

    def full(shape):
        return pl.BlockSpec(shape, lambda i: tuple(0 for _ in shape))

    vmem = (2 * (tm * d * 4 + tm * gw * 4 + tm * XATTN_W * 4 + tm * FOURIER_W * 4 + tm * HGRN_W * 2 + m * kvw * 2
                 + 3 * XATTN_W * d * 2 + d * d * 2 + tm * d * 4) + 10 * tm * d * 4 + 2 * tm * gw * 4)
    return pl.pallas_call(
        _merge_kernel,
        grid=(t // tm,),
        in_specs=[
            pl.BlockSpec((tm, d), lambda i: (i, 0)),
            full((1, d)), full((1, d)),
            pl.BlockSpec((tm, gw), lambda i: (i, gate_cb)),
            pl.BlockSpec((tm, XATTN_W), lambda i: (i, xq_cb)),
            pl.BlockSpec((tm, FOURIER_W), lambda i: (i, 0)),
            pl.BlockSpec((tm, HGRN_W), lambda i: (i, 0)),
            pl.BlockSpec((1, m, kvw), lambda i: (i // per_b, 0, 0)),
            full((XATTN_W, d)), full((FOURIER_W, d)), full((1, d)), full((HGRN_W, d)), full((1, gw)),
            full((d, d)), full((1, d)), full((1, d)),
        ],
        out_specs=pl.BlockSpec((tm, d), lambda i: (i, 0)),
        out_shape=jax.ShapeDtypeStruct((t, d), F32),
        compiler_params=_params(("parallel",), vmem),
        name="merge_out",
    )(x2, ln_g, ln_b, proj, proj, fm, ho, kv, w_xo, w_fo, b_fo, w_ho, b_gate, w_out, g1, b1)


def _route_kernel(h_ref, wr_ref, bias_ref, idx_ref, gw_ref, rank_ref, cnt_ref, carry_ref):
    tm = h_ref.shape[0]
    neg = -jnp.inf

    @pl.when(pl.program_id(0) == 0)
    def _():
        carry_ref[...] = jnp.zeros_like(carry_ref)

    aff = jax.nn.sigmoid(_dot_nt(wr_ref[...], h_ref[...].astype(BF16)))
    sel = aff + bias_ref[...]

    giota = lax.broadcasted_iota(jnp.int32, (GROUP_SIZE, tm), 0)
    scores = []
    for g in range(N_GROUPS):
        slab = sel[g * GROUP_SIZE:(g + 1) * GROUP_SIZE, :]
        m1 = jnp.max(slab, axis=0, keepdims=True)
        first = jnp.min(jnp.where(slab == m1, giota, GROUP_SIZE), axis=0, keepdims=True)
        m2 = jnp.max(jnp.where(giota == first, neg, slab), axis=0, keepdims=True)
        scores.append(m1 + m2)
    gs = jnp.concatenate(scores, axis=0)

    grow = lax.broadcasted_iota(jnp.int32, (N_GROUPS, tm), 0)
    gsel = jnp.zeros((N_GROUPS, tm), jnp.bool_)
    work = gs
    for _ in range(TOPK_GROUPS):
        m = jnp.max(work, axis=0, keepdims=True)
        first = jnp.min(jnp.where(work == m, grow, N_GROUPS), axis=0, keepdims=True)
        hit = grow == first
        gsel = jnp.logical_or(gsel, hit)
        work = jnp.where(hit, neg, work)

    masked = jnp.concatenate(
        [jnp.where(gsel[g:g + 1, :], sel[g * GROUP_SIZE:(g + 1) * GROUP_SIZE, :], neg) for g in range(N_GROUPS)],
        axis=0)

    erow = lax.broadcasted_iota(jnp.int32, (N_EXPERTS, tm), 0)
    ids, ws = [], []
    chosen = jnp.zeros((N_EXPERTS, tm), jnp.bool_)
    for _ in range(TOP_K):
        m = jnp.max(masked, axis=0, keepdims=True)
        first = jnp.min(jnp.where(masked == m, erow, N_EXPERTS), axis=0, keepdims=True)
        hit = erow == first
        ids.append(first)
        ws.append(jnp.sum(jnp.where(hit, aff, 0.0), axis=0, keepdims=True))
        masked = jnp.where(hit, neg, masked)
        chosen = jnp.logical_or(chosen, hit)
    w = jnp.concatenate(ws, axis=0)
    idx_ref[...] = jnp.concatenate(ids, axis=0)
    gw_ref[...] = w / jnp.sum(w, axis=0, keepdims=True) * ROUTED_SCALE

    chosen_f = jnp.where(chosen, 1.0, 0.0)
    srow = lax.broadcasted_iota(jnp.int32, (tm, tm), 0)
    scol = lax.broadcasted_iota(jnp.int32, (tm, tm), 1)
    before = jnp.where(srow < scol, 1.0, 0.0).astype(BF16)
    prefix = _dot(chosen_f.astype(BF16), before) + carry_ref[...]
    ranks = [jnp.sum(jnp.where(erow == ids[k], prefix, 0.0), axis=0, keepdims=True) for k in range(TOP_K)]
    rank_ref[...] = jnp.concatenate(ranks, axis=0).astype(jnp.int32)
    carry_ref[...] = carry_ref[...] + jnp.sum(chosen_f, axis=1, keepdims=True)
    cnt_ref[...] = carry_ref[...]


def _route(h1, w_router_t, bias_col):
    t, d = h1.shape
    tm = min(ROUTE_TM, t)
    vmem = 2 * (tm * d * 4 + N_EXPERTS * d * 2 + 3 * TOP_K * tm * 4) + tm * d * 2 + 16 * N_EXPERTS * tm * 4 + tm * tm * 8
    return pl.pallas_call(
        _route_kernel,
        grid=(t // tm,),
        in_specs=[
            pl.BlockSpec((tm, d), lambda i: (i, 0)),
            pl.BlockSpec((N_EXPERTS, d), lambda i: (0, 0)),
            pl.BlockSpec((N_EXPERTS, 1), lambda i: (0, 0)),
        ],
        out_specs=[pl.BlockSpec((TOP_K, tm), lambda i: (0, i)), pl.BlockSpec((TOP_K, tm), lambda i: (0, i)),
                   pl.BlockSpec((TOP_K, tm), lambda i: (0, i)), pl.BlockSpec((N_EXPERTS, 1), lambda i: (0, 0))],
        out_shape=[jax.ShapeDtypeStruct((TOP_K, t), jnp.int32), jax.ShapeDtypeStruct((TOP_K, t), F32),
                   jax.ShapeDtypeStruct((TOP_K, t), jnp.int32), jax.ShapeDtypeStruct((N_EXPERTS, 1), F32)],
        scratch_shapes=[pltpu.VMEM((N_EXPERTS, 1), F32)],
        compiler_params=_params(("arbitrary",), vmem),
        name="router",
    )(h1, w_router_t, bias_col)


def _dest_kernel(idx_ref, rank_ref, ps_ref, o_ref):
    tm = idx_ref.shape[1]
    erow = lax.broadcasted_iota(jnp.int32, (N_EXPERTS, tm), 0)
    ps = ps_ref[...]
    rows = []
    for k in range(TOP_K):
        base = jnp.sum(jnp.where(erow == idx_ref[k:k + 1, :], ps, 0.0), axis=0, keepdims=True)
        rows.append(base.astype(jnp.int32) + rank_ref[k:k + 1, :])
    o_ref[...] = jnp.concatenate(rows, axis=0)


def _slot_dest(idx_t, rank_t, pstart_col):
    t = idx_t.shape[1]
    tm = min(ROUTE_TM, t)
    vmem = 2 * (3 * TOP_K * tm * 4) + 6 * N_EXPERTS * tm * 4
    return pl.pallas_call(
        _dest_kernel,
        grid=(t // tm,),
        in_specs=[pl.BlockSpec((TOP_K, tm), lambda i: (0, i)), pl.BlockSpec((TOP_K, tm), lambda i: (0, i)),
                  pl.BlockSpec((N_EXPERTS, 1), lambda i: (0, 0))],
        out_specs=pl.BlockSpec((TOP_K, tm), lambda i: (0, i)),
        out_shape=jax.ShapeDtypeStruct((TOP_K, t), jnp.int32),
        compiler_params=_params(("parallel",), vmem),
        name="slot_dest",
    )(idx_t, rank_t, pstart_col)


LANES = 128
U32 = jnp.uint32
HIGH_HALF = np.uint32(0xFFFF0000)
N_DMA_PRIORITIES = 2


def _tile_rows(d):
    return d // (2 * LANES)


def _pack_tiles(tile_ref, value, base=0):
    n, d = value.shape
    sub = _tile_rows(d)
    bits = lax.bitcast_convert_type(value.astype(BF16).astype(F32), U32)
    for c in range(sub):
        low = bits[:, c * LANES:(c + 1) * LANES] >> 16
        high = bits[:, (c + sub) * LANES:(c + sub + 1) * LANES] & HIGH_HALF
        tile_ref[pl.ds(base + c, n, stride=sub), :] = low | high


def _unpack_tiles(tile_ref, n, d, base=0):
    sub = _tile_rows(d)
    words = [tile_ref[pl.ds(base + c, n, stride=sub), :] for c in range(sub)]
    lows = [lax.bitcast_convert_type(u << 16, F32) for u in words]
    highs = [lax.bitcast_convert_type(u & HIGH_HALF, F32) for u in words]
    return jnp.concatenate(lows + highs, axis=1)


def _dispatch_kernel(pend_ref, padded_ref, nu_ref, dest_ref, h_ref, xs_ref, tile_ref, zero_ref, sem, zsem):
    tm, d = h_ref.shape
    sub = _tile_rows(d)
    zrows = zero_ref.shape[0]
    n_blocks = xs_ref.shape[0] // zrows

    @pl.when(pl.program_id(0) == 0)
    def _():
        zero_ref[...] = jnp.zeros_like(zero_ref)

        def zero_block(first_row):
            return pltpu.make_async_copy(zero_ref, xs_ref.at[pl.ds(pl.multiple_of(first_row, 8), zrows), :], zsem)

        def issue(e, carry):
            @pl.when(padded_ref[e] > 0)
            def _():
                zero_block(pend_ref[e] * sub - zrows).start()
            return carry

        def drain(e, carry):
            @pl.when(padded_ref[e] > 0)
            def _():
                zero_block(pend_ref[e] * sub - zrows).wait()
            return carry

        def issue_tail(blk, carry):
            zero_block(blk * zrows).start()
            return carry

        def drain_tail(blk, carry):
            zero_block(blk * zrows).wait()
            return carry

        lax.fori_loop(0, N_EXPERTS, issue, 0)
        lax.fori_loop(nu_ref[0], n_blocks, issue_tail, 0)
        lax.fori_loop(0, N_EXPERTS, drain, 0)
        lax.fori_loop(nu_ref[0], n_blocks, drain_tail, 0)

    i = pl.program_id(0)
    slot = lax.rem(i, 2)
    _pack_tiles(tile_ref.at[slot], h_ref[...])

    def row_copy(tok, dst_row):
        return pltpu.make_async_copy(tile_ref.at[slot, pl.ds(pl.multiple_of(tok * sub, sub), sub), :],
                                     xs_ref.at[pl.ds(pl.multiple_of(dst_row * sub, sub), sub), :], sem.at[slot])

    def issue(tok, carry):
        for k in range(TOP_K):
            row_copy(tok, dest_ref[tok * TOP_K + k]).start(priority=k % N_DMA_PRIORITIES)
        return carry

    lax.fori_loop(0, tm, issue, 0)

    def wait_tile(s):
        for _ in range(TOP_K):
            pltpu.make_async_copy(tile_ref.at[s], xs_ref.at[pl.ds(0, tm * sub), :], sem.at[s]).wait()

    @pl.when(i > 0)
    def _():
        wait_tile(1 - slot)

    @pl.when(i == pl.num_programs(0) - 1)
    def _():
        wait_tile(slot)


def _dispatch(h1, dest_flat, pend, padded, n_used, n_rows_pad):
    t, d = h1.shape
    sub = _tile_rows(d)
    tm = min(DISPATCH_TM, t)
    vmem = 2 * (tm * d * 4) + tm * d * 2 + MOE_BM * d * 2 + 4 * tm * d * 4
    return pl.pallas_call(
        _dispatch_kernel,
        grid_spec=pltpu.PrefetchScalarGridSpec(
            num_scalar_prefetch=3,
            grid=(t // tm,),
            in_specs=[
                pl.BlockSpec((tm * TOP_K,), lambda i, pe, pa, nu: (i,), memory_space=pltpu.SMEM),
                pl.BlockSpec((tm, d), lambda i, pe, pa, nu: (i, 0)),
            ],
            out_specs=pl.BlockSpec(memory_space=pl.ANY),
            scratch_shapes=[pltpu.VMEM((2, tm * sub, LANES), U32), pltpu.VMEM((MOE_BM * sub, LANES), U32),
                            pltpu.SemaphoreType.DMA((2,)), pltpu.SemaphoreType.DMA(())],
        ),
        out_shape=jax.ShapeDtypeStruct((n_rows_pad * sub, LANES), U32),
        compiler_params=_params(("arbitrary",), vmem),
        name="dispatch_rows",
    )(pend, padded, n_used, dest_flat, h1)


def _expert_kernel(ps_ref, pd_ref, nu_ref, wg_ref, wu_ref, wd_ref, x_hbm, y_hbm,
                   xbuf, ybuf, wgb_ref, wub_ref, wdb_ref, xsem, ysem):
    e = pl.program_id(0)
    d = wg_ref.shape[1]
    slots = xbuf.shape[0]
    ahead = slots - 1
    rows = xbuf.shape[1]
    bm = rows // _tile_rows(d)
    n_blocks = y_hbm.shape[0] // rows
    n_used = nu_ref[0]
    first = ps_ref[e] // bm
    nb = pd_ref[e] // bm

    def x_copy(g, slot):
        return pltpu.make_async_copy(x_hbm.at[pl.ds(pl.multiple_of(g * rows, rows), rows), :], xbuf.at[slot],
                                     xsem.at[slot])

    def y_copy(g, slot):
        return pltpu.make_async_copy(ybuf.at[slot], y_hbm.at[pl.ds(pl.multiple_of(g * rows, rows), rows), :],
                                     ysem.at[slot])

    @pl.when(e == 0)
    def _():
        for g0 in range(ahead):
            @pl.when(g0 < n_used)
            def _():
                x_copy(g0, g0).start()

    @pl.when(nb > 0)
    def _():
        wgb_ref[...] = wg_ref[0].astype(BF16)
        wub_ref[...] = wu_ref[0].astype(BF16)
        wdb_ref[...] = wd_ref[0].astype(BF16)

        def block(j, carry):
            g = first + j
            slot = lax.rem(g, slots)
            x_copy(g, slot).wait()

            @pl.when(g + ahead < n_used)
            def _():
                x_copy(g + ahead, lax.rem(g + ahead, slots)).start()

            @pl.when(g >= slots)
            def _():
                y_copy(g - slots, slot).wait()

            xb = _unpack_tiles(xbuf.at[slot], bm, d).astype(BF16)
            hb = (_silu(_dot(xb, wgb_ref[...])) * _dot(xb, wub_ref[...])).astype(BF16)
            _pack_tiles(ybuf.at[slot], _dot(hb, wdb_ref[...]))
            y_copy(g, slot).start()
            return carry

        lax.fori_loop(0, nb, block, 0)

    @pl.when(e == pl.num_programs(0) - 1)
    def _():
        def drain_used(g, carry):
            y_copy(g, lax.rem(g, slots)).wait()
            return carry

        lax.fori_loop(jnp.maximum(n_used - slots, 0), n_used, drain_used, 0)
        ybuf[0] = jnp.zeros_like(ybuf[0])

        def issue(g, carry):
            y_copy(g, 0).start()
            return carry

        def drain(g, carry):
            y_copy(g, 0).wait()
            return carry

        lax.fori_loop(n_used, n_blocks, issue, 0)
        lax.fori_loop(n_used, n_blocks, drain, 0)


def _experts(x_sorted, pstart, padded, n_used, w_gate, w_up, w_down):
    n_exp, d, ff = w_gate.shape
    sub = _tile_rows(d)
    bm = MOE_BM
    vmem = 2 * (3 * d * ff * 4) + 3 * d * ff * 2 + 2 * EXPERT_SLOTS * bm * d * 2 + 6 * bm * ff * 4 + 6 * bm * d * 4
    return pl.pallas_call(
        _expert_kernel,
        grid_spec=pltpu.PrefetchScalarGridSpec(
            num_scalar_prefetch=3,
            grid=(n_exp,),
            in_specs=[
                pl.BlockSpec((1, d, ff), lambda e, ps, pd, nu: (e, 0, 0)),
                pl.BlockSpec((1, d, ff), lambda e, ps, pd, nu: (e, 0, 0)),
                pl.BlockSpec((1, ff, d), lambda e, ps, pd, nu: (e, 0, 0)),
                pl.BlockSpec(memory_space=pl.ANY),
            ],
            out_specs=pl.BlockSpec(memory_space=pl.ANY),
            scratch_shapes=[pltpu.VMEM((EXPERT_SLOTS, bm * sub, LANES), U32),
                            pltpu.VMEM((EXPERT_SLOTS, bm * sub, LANES), U32),
                            pltpu.VMEM((d, ff), BF16), pltpu.VMEM((d, ff), BF16), pltpu.VMEM((ff, d), BF16),
                            pltpu.SemaphoreType.DMA((EXPERT_SLOTS,)), pltpu.SemaphoreType.DMA((EXPERT_SLOTS,))],
        ),
        out_shape=jax.ShapeDtypeStruct(x_sorted.shape, U32),
        compiler_params=_params(("arbitrary",), vmem),
        name="expert_mlp",
    )(pstart, padded, n_used, w_gate, w_up, w_down, x_sorted)


def _final_kernel(dest_ref, dest_next_ref, h_ref, gw_ref, y_ref, wg_ref, wu_ref, wd_ref, g_ref, b_ref, o_ref,
                  rows_a, rows_b, shared_ref, sem):
    i = pl.program_id(0)
    tm, d = h_ref.shape
    sub = _tile_rows(d)
    grp = COMBINE_GROUP

    def issue_tokens(dref, buf, s, tok0):
        for u in range(grp):
            for k in range(TOP_K):
                src_row = dref[(tok0 + u) * TOP_K + k]
                pltpu.make_async_copy(
                    y_ref.at[pl.ds(pl.multiple_of(src_row * sub, sub), sub), :],
                    buf.at[pl.ds(pl.multiple_of((k * tm + tok0 + u) * sub, sub), sub), :],
                    sem.at[s]).start(priority=k % N_DMA_PRIORITIES)

    def wait_tile(buf, s):
        pltpu.make_async_copy(y_ref.at[pl.ds(0, TOP_K * tm * sub), :], buf, sem.at[s]).wait()

    @pl.when(i == 0)
    def _():
        def first(j, carry):
            issue_tokens(dest_ref, rows_a, 0, j * grp)
            return carry

        lax.fori_loop(0, tm // grp, first, 0)

    hb = h_ref[...].astype(BF16)
    shared_ref[...] = _dot((_silu(_dot(hb, wg_ref[...])) * _dot(hb, wu_ref[...])).astype(BF16), wd_ref[...])
    ln_g = g_ref[...]
    ln_b = b_ref[...]

    def reduce_tile(cur, cur_s, nxt, nxt_s):
        wait_tile(cur, cur_s)

        def reduce_group(j, carry):
            tok0 = pl.multiple_of(j * grp, grp)
            issue_tokens(dest_next_ref, nxt, nxt_s, tok0)
            tok = pl.ds(tok0, grp)
            gw = gw_ref[tok, :]
            routed = gw[:, 0:1] * _unpack_tiles(cur, grp, d, base=tok0 * sub)
            for k in range(1, TOP_K):
                routed = routed + gw[:, k:k + 1] * _unpack_tiles(cur, grp, d, base=(k * tm + tok0) * sub)
            o_ref[tok, :] = _layer_norm(DEEPNORM_ALPHA * h_ref[tok, :] + (routed + shared_ref[tok, :]), ln_g, ln_b)
            return carry

        lax.fori_loop(0, tm // grp, reduce_group, 0)

        @pl.when(i == pl.num_programs(0) - 1)
        def _():
            wait_tile(nxt, nxt_s)

    @pl.when(lax.rem(i, 2) == 0)
    def _():
        reduce_tile(rows_a, 0, rows_b, 1)

    @pl.when(lax.rem(i, 2) == 1)
    def _():
        reduce_tile(rows_b, 1, rows_a, 0)


def _final(h1, dest_flat, gw, y_sorted, w_sg, w_su, w_sd, g2, b2):
    t, d = h1.shape
    ff = w_sg.shape[1]
    sub = _tile_rows(d)
    tm = min(FINAL_TM, t)
    nt = t // tm
    vmem = 2 * (2 * tm * d * 4 + tm * LANES * 4 + 3 * d * ff * 2) + 2 * TOP_K * tm * d * 2 + 8 * tm * d * 4

    def full(shape):
        return pl.BlockSpec(shape, lambda i: tuple(0 for _ in shape))

    return pl.pallas_call(
        _final_kernel,
        grid=(nt,),
        in_specs=[
            pl.BlockSpec((tm * TOP_K,), lambda i: (i,), memory_space=pltpu.SMEM),
            pl.BlockSpec((tm * TOP_K,), lambda i: (jnp.minimum(i + 1, nt - 1),), memory_space=pltpu.SMEM),
            pl.BlockSpec((tm, d), lambda i: (i, 0)),
            pl.BlockSpec((tm, TOP_K), lambda i: (i, 0)),
            pl.BlockSpec(memory_space=pl.ANY),
            full((d, ff)), full((d, ff)), full((ff, d)), full((1, d)), full((1, d)),
        ],
        out_specs=pl.BlockSpec((tm, d), lambda i: (i, 0)),
        out_shape=jax.ShapeDtypeStruct((t, d), F32),
        scratch_shapes=[pltpu.VMEM((TOP_K * tm * sub, LANES), U32), pltpu.VMEM((TOP_K * tm * sub, LANES), U32),
                        pltpu.VMEM((tm, d), F32), pltpu.SemaphoreType.DMA((2,))],
        compiler_params=_params(("arbitrary",), vmem),
        name="combine_final",
    )(dest_flat, dest_flat, h1, gw, y_sorted, w_sg, w_su, w_sd, g2, b2)


def _lower_bound(p):
    return jnp.cumsum(jax.nn.softmax(p.astype(F32), axis=0), axis=0)[0:1]


def _block_plan(counts, n_rows):
    bm = MOE_BM
    counts = counts.reshape(-1).astype(jnp.int32)
    padded = (counts + bm - 1) // bm * bm
    pend = jnp.cumsum(padded).astype(jnp.int32)
    pstart = pend - padded
    n_blocks = n_rows // bm + N_EXPERTS
    n_used = pend[-1:] // bm
    return pstart, pend, padded, n_used.astype(jnp.int32), n_blocks


def kernel(x, mem, ln_in_g, ln_in_b, ln_mem_g, ln_mem_b, hgrn_lb_fwd, hgrn_lb_bwd, w_in, b_gate, hgrn_norm_g, w_mem_kv, w_fourier_o, b_fourier_o, w_hgrn_o, w_xattn_o, w_out, ln1_g, ln1_b, w_router, router_bias, w_exp_gate, w_exp_up, w_exp_down, w_sh_gate, w_sh_up, w_sh_down, ln2_g, ln2_b):
    bsz, s, d = x.shape
    t = bsz * s
    l = 0
    row = lambda v: v.reshape(1, -1).astype(F32)

    n_gate = N_BRANCHES * d
    n_rest = w_in.shape[2] - n_gate
    w_in_r = jnp.concatenate([w_in[l][:, n_rest:], w_in[l][:, :n_rest]], axis=1).astype(BF16)
    gate_blocks = n_gate // INPROJ_TN
    cb_fourier = gate_blocks
    cb_hq, cb_hi, cb_zf, cb_zb, cb_hg, cb_xq = (gate_blocks + 1 + n for n in range(6))

    x2 = x.reshape(t, d)
    proj = _inproj(x2, row(ln_in_g), row(ln_in_b), w_in_r)
    kv = _mem_kv(mem, row(ln_mem_g), row(ln_mem_b), w_mem_kv[l].astype(BF16))

    fm = _fourier(proj, bsz, s, cb_fourier * (INPROJ_TN // FOURIER_GROUP_DIM))

    lb_f = _lower_bound(hgrn_lb_fwd)
    lb_b = _lower_bound(hgrn_lb_bwd)
    o_fwd = _hgrn_sweep(proj, lb_f, (cb_hq, cb_hi, cb_zf), bsz, s, reverse=False)
    ho = _hgrn_sweep(proj, lb_b, (cb_hq, cb_hi, cb_zb, cb_hg), bsz, s, reverse=True,
                     o_fwd=o_fwd, norm_g=row(hgrn_norm_g[l]))

    h1 = _merge(x2, row(ln_in_g), row(ln_in_b), proj, 0, cb_xq, fm, ho, kv,
                w_xattn_o[l].astype(BF16), w_fourier_o[l].astype(BF16), row(b_fourier_o[l]),
                w_hgrn_o[l].astype(BF16), row(b_gate[l]), w_out[l].astype(BF16), row(ln1_g[l]), row(ln1_b[l]), s)

    idx_t, gw_t, rank_t, counts = _route(h1, w_router[l].T.astype(BF16),
                                         router_bias[l].reshape(N_EXPERTS, 1).astype(F32))
    pstart, pend, padded, n_used, n_blocks = _block_plan(counts, t * TOP_K)
    dest_t = _slot_dest(idx_t, rank_t, pstart.astype(F32).reshape(N_EXPERTS, 1))
    dest_flat = dest_t.T.reshape(-1)

    x_sorted = _dispatch(h1, dest_flat, pend, padded, n_used, n_blocks * MOE_BM)
    y_sorted = _experts(x_sorted, pstart, padded, n_used, w_exp_gate[l], w_exp_up[l], w_exp_down[l])
    out = _final(h1, dest_flat, gw_t.T, y_sorted, w_sh_gate[l].astype(BF16), w_sh_up[l].astype(BF16),
                 w_sh_down[l].astype(BF16), row(ln2_g[l]), row(ln2_b[l]))
    return out.reshape(bsz, s, d)
```

```python
import functools
import math

import numpy as np
import jax
import jax.numpy as jnp
from jax import lax
from jax.experimental import pallas as pl
from jax.experimental.pallas import tpu as pltpu

F32 = jnp.float32
BF16 = jnp.bfloat16

LN_EPS = 1e-5
DEPTH = 1
DEEPNORM_ALPHA = (2 * DEPTH) ** 0.25
FOURIER_GROUP_DIM = 128
FOURIER_W = 512
HGRN_HEADS = 4
HGRN_HEAD_DIM = 128
HGRN_W = HGRN_HEADS * HGRN_HEAD_DIM
HGRN_CHUNK = 64
XATTN_HEADS = 4
XATTN_HEAD_DIM = 128
XATTN_W = XATTN_HEADS * XATTN_HEAD_DIM
N_BRANCHES = 3
N_EXPERTS = 256
TOP_K = 8
N_GROUPS = 8
TOPK_GROUPS = 4
GROUP_SIZE = N_EXPERTS // N_GROUPS
ROUTED_SCALE = 2.5

V7X_SCOPED_VMEM_CAP_BYTES = 60000 * 1024

INPROJ_TM = 2048
INPROJ_TN = 512
FOURIER_P = 128
FOURIER_UNROLL = 8
HGRN_ROWS = 512
MERGE_TM = 512
ROUTE_TM = 512
MOE_BM = 256
EXPERT_SLOTS = 8
DISPATCH_TM = 256
FINAL_TM = 256
COMBINE_GROUP = 32


def _params(semantics, vmem_bytes):
    return pltpu.CompilerParams(dimension_semantics=semantics,
                                vmem_limit_bytes=int(min(vmem_bytes, V7X_SCOPED_VMEM_CAP_BYTES)))


def _layer_norm(x, g, b):
    mu = jnp.mean(x, axis=-1, keepdims=True)
    xc = x - mu
    var = jnp.mean(xc * xc, axis=-1, keepdims=True)
    return xc * lax.rsqrt(var + LN_EPS) * g + b


def _silu(x):
    return x * jax.nn.sigmoid(x)


def _dot(a, b):
    return jnp.dot(a, b, preferred_element_type=F32)


def _dot_nt(a, b):
    return lax.dot_general(a, b, (((1,), (1,)), ((), ())), preferred_element_type=F32)


def _dot_tn(a, b):
    return lax.dot_general(a, b, (((0,), (0,)), ((), ())), preferred_element_type=F32)


def _inproj_kernel(x_ref, g_ref, b_ref, w_ref, o_ref, hn_ref):
    @pl.when(pl.program_id(1) == 0)
    def _():
        hn_ref[...] = _layer_norm(x_ref[...], g_ref[...], b_ref[...]).astype(BF16)

    o_ref[...] = _dot(hn_ref[...], w_ref[...])


def _inproj(x2, g, b, w):
    t, d = x2.shape
    n = w.shape[1]
    tm, tn = min(INPROJ_TM, t), INPROJ_TN
    vmem = 2 * (tm * d * 4 + d * tn * 2 + tm * tn * 4) + tm * d * 2 + 3 * tm * d * 4
    return pl.pallas_call(
        _inproj_kernel,
        grid=(t // tm, n // tn),
        in_specs=[
            pl.BlockSpec((tm, d), lambda i, j: (i, 0)),
            pl.BlockSpec((1, d), lambda i, j: (0, 0)),
            pl.BlockSpec((1, d), lambda i, j: (0, 0)),
            pl.BlockSpec((d, tn), lambda i, j: (0, j)),
        ],
        out_specs=pl.BlockSpec((tm, tn), lambda i, j: (i, j)),
        out_shape=jax.ShapeDtypeStruct((t, n), F32),
        scratch_shapes=[pltpu.VMEM((tm, d), BF16)],
        compiler_params=_params(("parallel", "arbitrary"), vmem),
        name="ln_inproj",
    )(x2, g, b, w)


def _mem_kv_kernel(m_ref, g_ref, b_ref, w_ref, o_ref):
    mn = _layer_norm(m_ref[0], g_ref[...], b_ref[...]).astype(BF16)
    o_ref[0] = _dot(mn, w_ref[...]).astype(BF16)


def _mem_kv(mem, g, b, w):
    bsz, m, d = mem.shape
    n = w.shape[1]
    vmem = 2 * (m * d * 4 + d * n * 2 + m * n * 2) + 4 * m * d * 4
    return pl.pallas_call(
        _mem_kv_kernel,
        grid=(bsz,),
        in_specs=[
            pl.BlockSpec((1, m, d), lambda i: (i, 0, 0)),
            pl.BlockSpec((1, d), lambda i: (0, 0)),
            pl.BlockSpec((1, d), lambda i: (0, 0)),
            pl.BlockSpec((d, n), lambda i: (0, 0)),
        ],
        out_specs=pl.BlockSpec((1, m, n), lambda i: (i, 0, 0)),
        out_shape=jax.ShapeDtypeStruct((bsz, m, n), BF16),
        compiler_params=_params(("parallel",), vmem),
        name="mem_kv",
    )(mem, g, b, w)


def _fourier_tables(s, c, p):
    q = s // p
    ki = np.arange(p, dtype=np.int64)
    i = np.arange(p, dtype=np.int64)
    j = np.arange(q, dtype=np.int64)
    ph = (ki[None, :, None] * (q * i[None, None, :] + j[:, None, None])) % s
    ang = 2.0 * np.pi * ph.astype(np.float64) / s
    m1 = np.concatenate([np.cos(ang), -np.sin(ang)], axis=1)
    kj = np.arange(q, dtype=np.int64)
    a2 = 2.0 * np.pi * ((kj[:, None] * j[None, :]) % q).astype(np.float64) / q
    m2 = np.block([[np.cos(a2), np.sin(a2)], [-np.sin(a2), np.cos(a2)]])
    cc = np.arange(c, dtype=np.int64)
    a3 = 2.0 * np.pi * ((cc[:, None] * cc[None, :]) % c).astype(np.float64) / c
    mc = np.concatenate([np.cos(a3), np.sin(a3)], axis=0) / math.sqrt(s * c)
    return (jnp.asarray(m1, dtype=BF16), jnp.asarray(m2, dtype=BF16), jnp.asarray(mc, dtype=BF16))


SUBLANES = 8


def _fourier_pitch(q):
    return 2 * q + SUBLANES if (2 * q // SUBLANES) % 2 == 0 else 2 * q


def _fourier_kernel(x_ref, m1_ref, m2_ref, mc_ref, o_ref, ys_ref, *, p, q):
    grp = FOURIER_UNROLL
    pitch = _fourier_pitch(q)

    def stage1(jg, carry):
        js = [jg * grp + u for u in range(grp)]
        xs = [x_ref[pl.ds(j, p, stride=q), :].astype(BF16) for j in js]
        ys = [_dot(m1_ref[j], xj) for j, xj in zip(js, xs)]
        for j, y in zip(js, ys):
            ys_ref[pl.ds(j, p, stride=pitch), :] = y[:p]
            ys_ref[pl.ds(q + j, p, stride=pitch), :] = y[p:]
        return carry

    lax.fori_loop(0, q // grp, stage1, 0)

    def stage2(kg, carry):
        kis = [kg * grp + u for u in range(grp)]
        ys = [ys_ref[pl.ds(pl.multiple_of(ki * pitch, SUBLANES), 2 * q), :].astype(BF16) for ki in kis]
        zs = [_dot(m2_ref[...], y) for y in ys]
        zcs = [jnp.concatenate([z[:q], z[q:]], axis=1).astype(BF16) for z in zs]
        outs = [_dot(zc, mc_ref[...]) for zc in zcs]
        for ki, o in zip(kis, outs):
            o_ref[pl.ds(ki, q, stride=p), :] = o
        return carry

    lax.fori_loop(0, p // grp, stage2, 0)


def _fourier(proj, bsz, s, col_block0):
    c = FOURIER_GROUP_DIM
    groups = FOURIER_W // c
    p = min(FOURIER_P, s // 8)
    q = s // p
    m1, m2, mc = _fourier_tables(s, c, p)
    vmem = 2 * (s * c * 4 + m1.size * 2 + m2.size * 2 + mc.size * 2 + s * c * 4) + 2 * s * c * 4 + 8 * p * c * 4
    return pl.pallas_call(
        functools.partial(_fourier_kernel, p=p, q=q),
        grid=(bsz, groups),
        in_specs=[
            pl.BlockSpec((s, c), lambda b, g: (b, col_block0 + g)),
            pl.BlockSpec((q, 2 * p, p), lambda b, g: (0, 0, 0)),
            pl.BlockSpec((2 * q, 2 * q), lambda b, g: (0, 0)),
            pl.BlockSpec((2 * c, c), lambda b, g: (0, 0)),
        ],
        out_specs=pl.BlockSpec((s, c), lambda b, g: (b, g)),
        out_shape=jax.ShapeDtypeStruct((bsz * s, FOURIER_W), F32),
        scratch_shapes=[pltpu.VMEM((p * _fourier_pitch(q), c), F32)],
        compiler_params=_params(("parallel", "parallel"), vmem),
        name="fourier_mix",
    )(proj, m1, m2, mc)


def _cumsum_dot(ones_mask, x):
    n = x.shape[1]
    hi = x.astype(BF16)
    r1 = x - hi.astype(F32)
    mid = r1.astype(BF16)
    lo = (r1 - mid.astype(F32)).astype(BF16)
    parts = _dot(ones_mask, jnp.concatenate([hi, mid, lo], axis=1))
    return parts[:, 0:n] + parts[:, n:2 * n] + parts[:, 2 * n:3 * n]


def _hgrn_kernel(*refs, reverse, finalize, n_chunks):
    if finalize:
        q_ref, i_ref, z_ref, lb_ref, of_ref, g_ref, ng_ref, o_ref, st_ref = refs
    else:
        q_ref, i_ref, z_ref, lb_ref, o_ref, st_ref = refs

    @pl.when(pl.program_id(1) == 0)
    def _():
        st_ref[...] = jnp.zeros_like(st_ref)

    ck = HGRN_CHUNK
    hd = HGRN_HEAD_DIM
    row = lax.broadcasted_iota(jnp.int32, (ck, ck), 0)
    col = lax.broadcasted_iota(jnp.int32, (ck, ck), 1)
    tri = (col >= row) if reverse else (col <= row)
    cum = jnp.where(tri, 1.0, 0.0).astype(BF16)
    lb = lb_ref[...]

    order = list(range(n_chunks - 1, -1, -1) if reverse else range(n_chunks))
    heads = [slice(h * hd, (h + 1) * hd) for h in range(HGRN_HEADS)]
    rows = {c: pl.ds(c * ck, ck) for c in order}

    q_dec, k_inv, k_end, decay, vb = {}, {}, {}, {}, {}
    for c in order:
        f = lb + (1.0 - lb) * jax.nn.sigmoid(z_ref[rows[c], :])
        k = 1.0 - f
        a = _cumsum_dot(cum, jnp.log(f))
        a_end = a[0:1, :] if reverse else a[ck - 1:ck, :]
        q_dec[c] = (_silu(q_ref[rows[c], :]) * jnp.exp(a)).astype(BF16)
        k_inv[c] = (k * jnp.exp(-a)).astype(BF16)
        k_end[c] = (k * jnp.exp(a_end - a)).astype(BF16)
        decay[c] = jnp.exp(a_end)
        vb[c] = i_ref[rows[c], :].astype(BF16)

    scores = {(c, h): jnp.where(tri, _dot_nt(q_dec[c][:, sl], k_inv[c][:, sl]), 0.0).astype(BF16)
              for c in order for h, sl in enumerate(heads)}
    update = {(c, h): _dot_tn(vb[c][:, sl], k_end[c][:, sl]) for c in order for h, sl in enumerate(heads)}
    o_intra = {(c, h): _dot(scores[c, h], vb[c][:, sl]) for c in order for h, sl in enumerate(heads)}

    state_in = {}
    for h, sl in enumerate(heads):
        st = st_ref[h]
        for c in order:
            state_in[c, h] = st.astype(BF16)
            st = st * decay[c][:, sl] + update[c, h]
        st_ref[h] = st

    o_inter = {(c, h): _dot_nt(q_dec[c][:, sl], state_in[c, h]) for c in order for h, sl in enumerate(heads)}

    for c in order:
        outs = []
        for h, sl in enumerate(heads):
            o_h = o_intra[c, h] + o_inter[c, h]
            if finalize:
                o_h = o_h + of_ref[rows[c], sl]
                ms = jnp.mean(o_h * o_h, axis=-1, keepdims=True)
                o_h = o_h * lax.rsqrt(ms + LN_EPS) * ng_ref[:, sl]
            outs.append(o_h)
        o = jnp.concatenate(outs, axis=1)
        if finalize:
            o_ref[rows[c], :] = (o * _silu(g_ref[rows[c], :])).astype(o_ref.dtype)
        else:
            o_ref[rows[c], :] = o


def _hgrn_sweep(proj, lb, cols, bsz, s, *, reverse, o_fwd=None, norm_g=None):
    w = HGRN_W
    rb = min(HGRN_ROWS, s)
    nb = s // rb
    finalize = o_fwd is not None

    def rmap(b, n):
        return b * nb + (nb - 1 - n if reverse else n)

    def cmap(cb):
        return lambda b, n: (rmap(b, n), cb)

    in_specs = [pl.BlockSpec((rb, w), cmap(cols[0])), pl.BlockSpec((rb, w), cmap(cols[1])),
                pl.BlockSpec((rb, w), cmap(cols[2])), pl.BlockSpec((1, w), lambda b, n: (0, 0))]
    args = [proj, proj, proj, lb]
    if finalize:
        in_specs += [pl.BlockSpec((rb, w), cmap(0)), pl.BlockSpec((rb, w), cmap(cols[3])),
                     pl.BlockSpec((1, w), lambda b, n: (0, 0))]
        args += [o_fwd, proj, norm_g]
    vmem = 2 * (6 * rb * w * 4) + HGRN_HEADS * HGRN_HEAD_DIM * HGRN_HEAD_DIM * 4 + 24 * HGRN_CHUNK * w * 4
    return pl.pallas_call(
        functools.partial(_hgrn_kernel, reverse=reverse, finalize=finalize, n_chunks=rb // HGRN_CHUNK),
        grid=(bsz, nb),
        in_specs=in_specs,
        out_specs=pl.BlockSpec((rb, w), cmap(0)),
        out_shape=jax.ShapeDtypeStruct((bsz * s, w), BF16 if finalize else F32),
        scratch_shapes=[pltpu.VMEM((HGRN_HEADS, HGRN_HEAD_DIM, HGRN_HEAD_DIM), F32)],
        compiler_params=_params(("parallel", "arbitrary"), vmem),
        name="hgrn_bwd" if reverse else "hgrn_fwd",
    )(*args)


def _merge_kernel(x_ref, lg_ref, lbi_ref, gl_ref, xq_ref, fm_ref, ho_ref, kv_ref, wx_ref, wf_ref, bf_ref,
                  wh_ref, bg_ref, wo_ref, g1_ref, b1_ref, o_ref):
    d = x_ref.shape[1]
    h = _layer_norm(x_ref[...], lg_ref[...], lbi_ref[...])

    kv = kv_ref[0]
    xq = xq_ref[...].astype(BF16)
    key_sl = [slice(hh * XATTN_HEAD_DIM, (hh + 1) * XATTN_HEAD_DIM) for hh in range(XATTN_HEADS)]
    val_sl = [slice(XATTN_W + hh * XATTN_HEAD_DIM, XATTN_W + (hh + 1) * XATTN_HEAD_DIM) for hh in range(XATTN_HEADS)]
    logits = [_dot_nt(xq[:, sl], kv[:, sl]) for sl in key_sl]
    y_f = _dot(fm_ref[...].astype(BF16), wf_ref[...]) + bf_ref[...]
    y_h = _dot(ho_ref[...], wh_ref[...])
    probs = []
    for sc in logits:
        sc = sc * (XATTN_HEAD_DIM ** -0.5)
        sc = sc - jnp.max(sc, axis=-1, keepdims=True)
        e = jnp.exp(sc)
        probs.append((e / jnp.sum(e, axis=-1, keepdims=True)).astype(BF16))
    att = jnp.concatenate([_dot(pr, kv[:, sl]) for pr, sl in zip(probs, val_sl)], axis=1).astype(BF16)

    y_x = _dot(att, wx_ref[...])
    gate = jax.nn.sigmoid(gl_ref[...] + bg_ref[...])
    merged = gate[:, 0:d] * y_f + gate[:, d:2 * d] * y_h + gate[:, 2 * d:3 * d] * y_x
    y = _dot(merged.astype(BF16), wo_ref[...])
    o_ref[...] = _layer_norm(DEEPNORM_ALPHA * h + y, g1_ref[...], b1_ref[...])


def _merge(x2, ln_g, ln_b, proj, gate_cb, xq_cb, fm, ho, kv, w_xo, w_fo, b_fo, w_ho, b_gate, w_out, g1, b1, s):
    t, d = x2.shape
    tm = min(MERGE_TM, s)
    per_b = s // tm
    m, kvw = kv.shape[1], kv.shape[2]
    gw = N_BRANCHES * d

    def full(shape):
        return pl.BlockSpec(shape, lambda i: tuple(0 for _ in shape))

    vmem = (2 * (tm * d * 4 + tm * gw * 4 + tm * XATTN_W * 4 + tm * FOURIER_W * 4 + tm * HGRN_W * 2 + m * kvw * 2
                 + 3 * XATTN_W * d * 2 + d * d * 2 + tm * d * 4) + 10 * tm * d * 4 + 2 * tm * gw * 4)
    return pl.pallas_call(
        _merge_kernel,
        grid=(t // tm,),
        in_specs=[
            pl.BlockSpec((tm, d), lambda i: (i, 0)),
            full((1, d)), full((1, d)),
            pl.BlockSpec((tm, gw), lambda i: (i, gate_cb)),
            pl.BlockSpec((tm, XATTN_W), lambda i: (i, xq_cb)),
            pl.BlockSpec((tm, FOURIER_W), lambda i: (i, 0)),
            pl.BlockSpec((tm, HGRN_W), lambda i: (i, 0)),
            pl.BlockSpec((1, m, kvw), lambda i: (i // per_b, 0, 0)),
            full((XATTN_W, d)), full((FOURIER_W, d)), full((1, d)), full((HGRN_W, d)), full((1, gw)),
            full((d, d)), full((1, d)), full((1, d)),
        ],
        out_specs=pl.BlockSpec((tm, d), lambda i: (i, 0)),
        out_shape=jax.ShapeDtypeStruct((t, d), F32),
        compiler_params=_params(("parallel",), vmem),
        name="merge_out",
    )(x2, ln_g, ln_b, proj, proj, fm, ho, kv, w_xo, w_fo, b_fo, w_ho, b_gate, w_out, g1, b1)


def _route_kernel(h_ref, wr_ref, bias_ref, idx_ref, gw_ref, rank_ref, cnt_ref, carry_ref):
    tm = h_ref.shape[0]
    neg = -jnp.inf

    @pl.when(pl.program_id(0) == 0)
    def _():
        carry_ref[...] = jnp.zeros_like(carry_ref)

    aff = jax.nn.sigmoid(_dot_nt(wr_ref[...], h_ref[...].astype(BF16)))
    sel = aff + bias_ref[...]

    giota = lax.broadcasted_iota(jnp.int32, (GROUP_SIZE, tm), 0)
    scores = []
    for g in range(N_GROUPS):
        slab = sel[g * GROUP_SIZE:(g + 1) * GROUP_SIZE, :]
        m1 = jnp.max(slab, axis=0, keepdims=True)
        first = jnp.min(jnp.where(slab == m1, giota, GROUP_SIZE), axis=0, keepdims=True)
        m2 = jnp.max(jnp.where(giota == first, neg, slab), axis=0, keepdims=True)
        scores.append(m1 + m2)
    gs = jnp.concatenate(scores, axis=0)

    grow = lax.broadcasted_iota(jnp.int32, (N_GROUPS, tm), 0)
    gsel = jnp.zeros((N_GROUPS, tm), jnp.bool_)
    work = gs
    for _ in range(TOPK_GROUPS):
        m = jnp.max(work, axis=0, keepdims=True)
        first = jnp.min(jnp.where(work == m, grow, N_GROUPS), axis=0, keepdims=True)
        hit = grow == first
        gsel = jnp.logical_or(gsel, hit)
        work = jnp.where(hit, neg, work)

    masked = jnp.concatenate(
        [jnp.where(gsel[g:g + 1, :], sel[g * GROUP_SIZE:(g + 1) * GROUP_SIZE, :], neg) for g in range(N_GROUPS)],
        axis=0)

    erow = lax.broadcasted_iota(jnp.int32, (N_EXPERTS, tm), 0)
    ids, ws = [], []
    chosen = jnp.zeros((N_EXPERTS, tm), jnp.bool_)
    for _ in range(TOP_K):
        m = jnp.max(masked, axis=0, keepdims=True)
        first = jnp.min(jnp.where(masked == m, erow, N_EXPERTS), axis=0, keepdims=True)
        hit = erow == first
        ids.append(first)
        ws.append(jnp.sum(jnp.where(hit, aff, 0.0), axis=0, keepdims=True))
        masked = jnp.where(hit, neg, masked)
        chosen = jnp.logical_or(chosen, hit)
    w = jnp.concatenate(ws, axis=0)
    idx_ref[...] = jnp.concatenate(ids, axis=0)
    gw_ref[...] = w / jnp.sum(w, axis=0, keepdims=True) * ROUTED_SCALE

    chosen_f = jnp.where(chosen, 1.0, 0.0)
    srow = lax.broadcasted_iota(jnp.int32, (tm, tm), 0)
    scol = lax.broadcasted_iota(jnp.int32, (tm, tm), 1)
    before = jnp.where(srow < scol, 1.0, 0.0).astype(BF16)
    prefix = _dot(chosen_f.astype(BF16), before) + carry_ref[...]
    ranks = [jnp.sum(jnp.where(erow == ids[k], prefix, 0.0), axis=0, keepdims=True) for k in range(TOP_K)]
    rank_ref[...] = jnp.concatenate(ranks, axis=0).astype(jnp.int32)
    carry_ref[...] = carry_ref[...] + jnp.sum(chosen_f, axis=1, keepdims=True)
    cnt_ref[...] = carry_ref[...]


def _route(h1, w_router_t, bias_col):
    t, d = h1.shape
    tm = min(ROUTE_TM, t)
    vmem = 2 * (tm * d * 4 + N_EXPERTS * d * 2 + 3 * TOP_K * tm * 4) + tm * d * 2 + 16 * N_EXPERTS * tm * 4 + tm * tm * 8
    return pl.pallas_call(
        _route_kernel,
        grid=(t // tm,),
        in_specs=[
            pl.BlockSpec((tm, d), lambda i: (i, 0)),
            pl.BlockSpec((N_EXPERTS, d), lambda i: (0, 0)),
            pl.BlockSpec((N_EXPERTS, 1), lambda i: (0, 0)),
        ],
        out_specs=[pl.BlockSpec((TOP_K, tm), lambda i: (0, i)), pl.BlockSpec((TOP_K, tm), lambda i: (0, i)),
                   pl.BlockSpec((TOP_K, tm), lambda i: (0, i)), pl.BlockSpec((N_EXPERTS, 1), lambda i: (0, 0))],
        out_shape=[jax.ShapeDtypeStruct((TOP_K, t), jnp.int32), jax.ShapeDtypeStruct((TOP_K, t), F32),
                   jax.ShapeDtypeStruct((TOP_K, t), jnp.int32), jax.ShapeDtypeStruct((N_EXPERTS, 1), F32)],
        scratch_shapes=[pltpu.VMEM((N_EXPERTS, 1), F32)],
        compiler_params=_params(("arbitrary",), vmem),
        name="router",
    )(h1, w_router_t, bias_col)


def _dest_kernel(idx_ref, rank_ref, ps_ref, o_ref):
    tm = idx_ref.shape[1]
    erow = lax.broadcasted_iota(jnp.int32, (N_EXPERTS, tm), 0)
    ps = ps_ref[...]
    rows = []
    for k in range(TOP_K):
        base = jnp.sum(jnp.where(erow == idx_ref[k:k + 1, :], ps, 0.0), axis=0, keepdims=True)
        rows.append(base.astype(jnp.int32) + rank_ref[k:k + 1, :])
    o_ref[...] = jnp.concatenate(rows, axis=0)


def _slot_dest(idx_t, rank_t, pstart_col):
    t = idx_t.shape[1]
    tm = min(ROUTE_TM, t)
    vmem = 2 * (3 * TOP_K * tm * 4) + 6 * N_EXPERTS * tm * 4
    return pl.pallas_call(
        _dest_kernel,
        grid=(t // tm,),
        in_specs=[pl.BlockSpec((TOP_K, tm), lambda i: (0, i)), pl.BlockSpec((TOP_K, tm), lambda i: (0, i)),
                  pl.BlockSpec((N_EXPERTS, 1), lambda i: (0, 0))],
        out_specs=pl.BlockSpec((TOP_K, tm), lambda i: (0, i)),
        out_shape=jax.ShapeDtypeStruct((TOP_K, t), jnp.int32),
        compiler_params=_params(("parallel",), vmem),
        name="slot_dest",
    )(idx_t, rank_t, pstart_col)


LANES = 128
U32 = jnp.uint32
HIGH_HALF = np.uint32(0xFFFF0000)
N_DMA_PRIORITIES = 2


def _tile_rows(d):
    return d // (2 * LANES)


def _pack_tiles(tile_ref, value, base=0):
    n, d = value.shape
    sub = _tile_rows(d)
    bits = lax.bitcast_convert_type(value.astype(BF16).astype(F32), U32)
    for c in range(sub):
        low = bits[:, c * LANES:(c + 1) * LANES] >> 16
        high = bits[:, (c + sub) * LANES:(c + sub + 1) * LANES] & HIGH_HALF
        tile_ref[pl.ds(base + c, n, stride=sub), :] = low | high


def _unpack_tiles(tile_ref, n, d, base=0):
    sub = _tile_rows(d)
    words = [tile_ref[pl.ds(base + c, n, stride=sub), :] for c in range(sub)]
    lows = [lax.bitcast_convert_type(u << 16, F32) for u in words]
    highs = [lax.bitcast_convert_type(u & HIGH_HALF, F32) for u in words]
    return jnp.concatenate(lows + highs, axis=1)


def _dispatch_kernel(pend_ref, padded_ref, nu_ref, dest_ref, h_ref, xs_ref, tile_ref, zero_ref, sem, zsem):
    tm, d = h_ref.shape
    sub = _tile_rows(d)
    zrows = zero_ref.shape[0]
    n_blocks = xs_ref.shape[0] // zrows

    @pl.when(pl.program_id(0) == 0)
    def _():
        zero_ref[...] = jnp.zeros_like(zero_ref)

        def zero_block(first_row):
            return pltpu.make_async_copy(zero_ref, xs_ref.at[pl.ds(pl.multiple_of(first_row, 8), zrows), :], zsem)

        def issue(e, carry):
            @pl.when(padded_ref[e] > 0)
            def _():
                zero_block(pend_ref[e] * sub - zrows).start()
            return carry

        def drain(e, carry):
            @pl.when(padded_ref[e] > 0)
            def _():
                zero_block(pend_ref[e] * sub - zrows).wait()
            return carry

        def issue_tail(blk, carry):
            zero_block(blk * zrows).start()
            return carry

        def drain_tail(blk, carry):
            zero_block(blk * zrows).wait()
            return carry

        lax.fori_loop(0, N_EXPERTS, issue, 0)
        lax.fori_loop(nu_ref[0], n_blocks, issue_tail, 0)
        lax.fori_loop(0, N_EXPERTS, drain, 0)
        lax.fori_loop(nu_ref[0], n_blocks, drain_tail, 0)

    i = pl.program_id(0)
    slot = lax.rem(i, 2)
    _pack_tiles(tile_ref.at[slot], h_ref[...])

    def row_copy(tok, dst_row):
        return pltpu.make_async_copy(tile_ref.at[slot, pl.ds(pl.multiple_of(tok * sub, sub), sub), :],
                                     xs_ref.at[pl.ds(pl.multiple_of(dst_row * sub, sub), sub), :], sem.at[slot])

    def issue(tok, carry):
        for k in range(TOP_K):
            row_copy(tok, dest_ref[tok * TOP_K + k]).start(priority=k % N_DMA_PRIORITIES)
        return carry

    lax.fori_loop(0, tm, issue, 0)

    def wait_tile(s):
        for _ in range(TOP_K):
            pltpu.make_async_copy(tile_ref.at[s], xs_ref.at[pl.ds(0, tm * sub), :], sem.at[s]).wait()

    @pl.when(i > 0)
    def _():
        wait_tile(1 - slot)

    @pl.when(i == pl.num_programs(0) - 1)
    def _():
        wait_tile(slot)


def _dispatch(h1, dest_flat, pend, padded, n_used, n_rows_pad):
    t, d = h1.shape
    sub = _tile_rows(d)
    tm = min(DISPATCH_TM, t)
    vmem = 2 * (tm * d * 4) + tm * d * 2 + MOE_BM * d * 2 + 4 * tm * d * 4
    return pl.pallas_call(
        _dispatch_kernel,
        grid_spec=pltpu.PrefetchScalarGridSpec(
            num_scalar_prefetch=3,
            grid=(t // tm,),
            in_specs=[
                pl.BlockSpec((tm * TOP_K,), lambda i, pe, pa, nu: (i,), memory_space=pltpu.SMEM),
                pl.BlockSpec((tm, d), lambda i, pe, pa, nu: (i, 0)),
            ],
            out_specs=pl.BlockSpec(memory_space=pl.ANY),
            scratch_shapes=[pltpu.VMEM((2, tm * sub, LANES), U32), pltpu.VMEM((MOE_BM * sub, LANES), U32),
                            pltpu.SemaphoreType.DMA((2,)), pltpu.SemaphoreType.DMA(())],
        ),
        out_shape=jax.ShapeDtypeStruct((n_rows_pad * sub, LANES), U32),
        compiler_params=_params(("arbitrary",), vmem),
        name="dispatch_rows",
    )(pend, padded, n_used, dest_flat, h1)


def _expert_kernel(ps_ref, pd_ref, nu_ref, wg_ref, wu_ref, wd_ref, x_hbm, y_hbm,
                   xbuf, ybuf, wgb_ref, wub_ref, wdb_ref, xsem, ysem):
    e = pl.program_id(0)
    d = wg_ref.shape[1]
    slots = xbuf.shape[0]
    ahead = slots - 2
    rows = xbuf.shape[1]
    bm = rows // _tile_rows(d)
    n_blocks = y_hbm.shape[0] // rows
    n_used = nu_ref[0]
    first = ps_ref[e] // bm
    nb = pd_ref[e] // bm

    def x_copy(g, slot):
        return pltpu.make_async_copy(x_hbm.at[pl.ds(pl.multiple_of(g * rows, rows), rows), :], xbuf.at[slot],
                                     xsem.at[slot])

    def y_copy(g, slot):
        return pltpu.make_async_copy(ybuf.at[slot], y_hbm.at[pl.ds(pl.multiple_of(g * rows, rows), rows), :],
                                     ysem.at[slot])

    @pl.when(e == 0)
    def _():
        for g0 in range(ahead):
            @pl.when(g0 < n_used)
            def _():
                x_copy(g0, g0).start()

    @pl.when(nb > 0)
    def _():
        wgb_ref[...] = wg_ref[0].astype(BF16)
        wub_ref[...] = wu_ref[0].astype(BF16)
        wdb_ref[...] = wd_ref[0].astype(BF16)

        def acquire(g):
            slot = lax.rem(g, slots)
            x_copy(g, slot).wait()

            @pl.when(g + ahead < n_used)
            def _():
                x_copy(g + ahead, lax.rem(g + ahead, slots)).start()

            @pl.when(g >= slots)
            def _():
                y_copy(g - slots, slot).wait()

            return slot

        def compute(block_slots):
            xs = [_unpack_tiles(xbuf.at[s], bm, d).astype(BF16) for s in block_slots]
            gates = [_dot(xb, wgb_ref[...]) for xb in xs]
            ups = [_dot(xb, wub_ref[...]) for xb in xs]
            hs = [(_silu(hg) * hu).astype(BF16) for hg, hu in zip(gates, ups)]
            ys = [_dot(hb, wdb_ref[...]) for hb in hs]
            for s, y in zip(block_slots, ys):
                _pack_tiles(ybuf.at[s], y)

        def block_pair(jj, carry):
            g = first + 2 * jj
            s0 = acquire(g)
            s1 = acquire(g + 1)
            compute([s0, s1])
            y_copy(g, s0).start()
            y_copy(g + 1, s1).start()
            return carry

        lax.fori_loop(0, nb // 2, block_pair, 0)

        @pl.when(lax.rem(nb, 2) == 1)
        def _():
            g = first + nb - 1
            s0 = acquire(g)
            compute([s0])
            y_copy(g, s0).start()

    @pl.when(e == pl.num_programs(0) - 1)
    def _():
        def drain_used(g, carry):
            y_copy(g, lax.rem(g, slots)).wait()
            return carry

        lax.fori_loop(jnp.maximum(n_used - slots, 0), n_used, drain_used, 0)
        ybuf[0] = jnp.zeros_like(ybuf[0])

        def issue(g, carry):
            y_copy(g, 0).start()
            return carry

        def drain(g, carry):
            y_copy(g, 0).wait()
            return carry

        lax.fori_loop(n_used, n_blocks, issue, 0)
        lax.fori_loop(n_used, n_blocks, drain, 0)


def _experts(x_sorted, pstart, padded, n_used, w_gate, w_up, w_down):
    n_exp, d, ff = w_gate.shape
    sub = _tile_rows(d)
    bm = MOE_BM
    vmem = 2 * (3 * d * ff * 4) + 3 * d * ff * 2 + 2 * EXPERT_SLOTS * bm * d * 2 + 6 * bm * ff * 4 + 6 * bm * d * 4
    return pl.pallas_call(
        _expert_kernel,
        grid_spec=pltpu.PrefetchScalarGridSpec(
            num_scalar_prefetch=3,
            grid=(n_exp,),
            in_specs=[
                pl.BlockSpec((1, d, ff), lambda e, ps, pd, nu: (e, 0, 0)),
                pl.BlockSpec((1, d, ff), lambda e, ps, pd, nu: (e, 0, 0)),
                pl.BlockSpec((1, ff, d), lambda e, ps, pd, nu: (e, 0, 0)),
                pl.BlockSpec(memory_space=pl.ANY),
            ],
            out_specs=pl.BlockSpec(memory_space=pl.ANY),
            scratch_shapes=[pltpu.VMEM((EXPERT_SLOTS, bm * sub, LANES), U32),
                            pltpu.VMEM((EXPERT_SLOTS, bm * sub, LANES), U32),
                            pltpu.VMEM((d, ff), BF16), pltpu.VMEM((d, ff), BF16), pltpu.VMEM((ff, d), BF16),
                            pltpu.SemaphoreType.DMA((EXPERT_SLOTS,)), pltpu.SemaphoreType.DMA((EXPERT_SLOTS,))],
        ),
        out_shape=jax.ShapeDtypeStruct(x_sorted.shape, U32),
        compiler_params=_params(("arbitrary",), vmem),
        name="expert_mlp",
    )(pstart, padded, n_used, w_gate, w_up, w_down, x_sorted)


def _final_kernel(dest_ref, dest_next_ref, h_ref, gw_ref, y_ref, wg_ref, wu_ref, wd_ref, g_ref, b_ref, o_ref,
                  rows_a, rows_b, shared_ref, sem):
    i = pl.program_id(0)
    tm, d = h_ref.shape
    sub = _tile_rows(d)
    grp = COMBINE_GROUP

    def issue_tokens(dref, buf, s, tok0):
        for u in range(grp):
            for k in range(TOP_K):
                src_row = dref[(tok0 + u) * TOP_K + k]
                pltpu.make_async_copy(
                    y_ref.at[pl.ds(pl.multiple_of(src_row * sub, sub), sub), :],
                    buf.at[pl.ds(pl.multiple_of((k * tm + tok0 + u) * sub, sub), sub), :],
                    sem.at[s]).start(priority=k % N_DMA_PRIORITIES)

    def wait_tile(buf, s):
        pltpu.make_async_copy(y_ref.at[pl.ds(0, TOP_K * tm * sub), :], buf, sem.at[s]).wait()

    @pl.when(i == 0)
    def _():
        def first(j, carry):
            issue_tokens(dest_ref, rows_a, 0, j * grp)
            return carry

        lax.fori_loop(0, tm // grp, first, 0)

    hb = h_ref[...].astype(BF16)
    shared_ref[...] = _dot((_silu(_dot(hb, wg_ref[...])) * _dot(hb, wu_ref[...])).astype(BF16), wd_ref[...])
    ln_g = g_ref[...]
    ln_b = b_ref[...]

    def reduce_tile(cur, cur_s, nxt, nxt_s):
        wait_tile(cur, cur_s)

        def reduce_group(j, carry):
            tok0 = pl.multiple_of(j * grp, grp)
            issue_tokens(dest_next_ref, nxt, nxt_s, tok0)
            tok = pl.ds(tok0, grp)
            gw = gw_ref[tok, :]
            routed = gw[:, 0:1] * _unpack_tiles(cur, grp, d, base=tok0 * sub)
            for k in range(1, TOP_K):
                routed = routed + gw[:, k:k + 1] * _unpack_tiles(cur, grp, d, base=(k * tm + tok0) * sub)
            o_ref[tok, :] = _layer_norm(DEEPNORM_ALPHA * h_ref[tok, :] + (routed + shared_ref[tok, :]), ln_g, ln_b)
            return carry

        lax.fori_loop(0, tm // grp, reduce_group, 0)

        @pl.when(i == pl.num_programs(0) - 1)
        def _():
            wait_tile(nxt, nxt_s)

    @pl.when(lax.rem(i, 2) == 0)
    def _():
        reduce_tile(rows_a, 0, rows_b, 1)

    @pl.when(lax.rem(i, 2) == 1)
    def _():
        reduce_tile(rows_b, 1, rows_a, 0)


def _final(h1, dest_flat, gw, y_sorted, w_sg, w_su, w_sd, g2, b2):
    t, d = h1.shape
    ff = w_sg.shape[1]
    sub = _tile_rows(d)
    tm = min(FINAL_TM, t)
    nt = t // tm
    vmem = 2 * (2 * tm * d * 4 + tm * LANES * 4 + 3 * d * ff * 2) + 2 * TOP_K * tm * d * 2 + 8 * tm * d * 4

    def full(shape):
        return pl.BlockSpec(shape, lambda i: tuple(0 for _ in shape))

    return pl.pallas_call(
        _final_kernel,
        grid=(nt,),
        in_specs=[
            pl.BlockSpec((tm * TOP_K,), lambda i: (i,), memory_space=pltpu.SMEM),
            pl.BlockSpec((tm * TOP_K,), lambda i: (jnp.minimum(i + 1, nt - 1),), memory_space=pltpu.SMEM),
            pl.BlockSpec((tm, d), lambda i: (i, 0)),
            pl.BlockSpec((tm, TOP_K), lambda i: (i, 0)),
            pl.BlockSpec(memory_space=pl.ANY),
            full((d, ff)), full((d, ff)), full((ff, d)), full((1, d)), full((1, d)),
        ],
        out_specs=pl.BlockSpec((tm, d), lambda i: (i, 0)),
        out_shape=jax.ShapeDtypeStruct((t, d), F32),
        scratch_shapes=[pltpu.VMEM((TOP_K * tm * sub, LANES), U32), pltpu.VMEM((TOP_K * tm * sub, LANES), U32),
                        pltpu.VMEM((tm, d), F32), pltpu.SemaphoreType.DMA((2,))],
        compiler_params=_params(("arbitrary",), vmem),
        name="combine_final",
    )(dest_flat, dest_flat, h1, gw, y_sorted, w_sg, w_su, w_sd, g2, b2)


def _lower_bound(p):
    return jnp.cumsum(jax.nn.softmax(p.astype(F32), axis=0), axis=0)[0:1]


def _block_plan(counts, n_rows):
    bm = MOE_BM
    counts = counts.reshape(-1).astype(jnp.int32)
    padded = (counts + bm - 1) // bm * bm
    pend = jnp.cumsum(padded).astype(jnp.int32)
    pstart = pend - padded
    n_blocks = n_rows // bm + N_EXPERTS
    n_used = pend[-1:] // bm
    return pstart, pend, padded, n_used.astype(jnp.int32), n_blocks


def kernel(x, mem, ln_in_g, ln_in_b, ln_mem_g, ln_mem_b, hgrn_lb_fwd, hgrn_lb_bwd, w_in, b_gate, hgrn_norm_g, w_mem_kv, w_fourier_o, b_fourier_o, w_hgrn_o, w_xattn_o, w_out, ln1_g, ln1_b, w_router, router_bias, w_exp_gate, w_exp_up, w_exp_down, w_sh_gate, w_sh_up, w_sh_down, ln2_g, ln2_b):
    bsz, s, d = x.shape
    t = bsz * s
    l = 0
    row = lambda v: v.reshape(1, -1).astype(F32)

    n_gate = N_BRANCHES * d
    n_rest = w_in.shape[2] - n_gate
    w_in_r = jnp.concatenate([w_in[l][:, n_rest:], w_in[l][:, :n_rest]], axis=1).astype(BF16)
    gate_blocks = n_gate // INPROJ_TN
    cb_fourier = gate_blocks
    cb_hq, cb_hi, cb_zf, cb_zb, cb_hg, cb_xq = (gate_blocks + 1 + n for n in range(6))

    x2 = x.reshape(t, d)
    proj = _inproj(x2, row(ln_in_g), row(ln_in_b), w_in_r)
    kv = _mem_kv(mem, row(ln_mem_g), row(ln_mem_b), w_mem_kv[l].astype(BF16))

    fm = _fourier(proj, bsz, s, cb_fourier * (INPROJ_TN // FOURIER_GROUP_DIM))

    lb_f = _lower_bound(hgrn_lb_fwd)
    lb_b = _lower_bound(hgrn_lb_bwd)
    o_fwd = _hgrn_sweep(proj, lb_f, (cb_hq, cb_hi, cb_zf), bsz, s, reverse=False)
    ho = _hgrn_sweep(proj, lb_b, (cb_hq, cb_hi, cb_zb, cb_hg), bsz, s, reverse=True,
                     o_fwd=o_fwd, norm_g=row(hgrn_norm_g[l]))

    h1 = _merge(x2, row(ln_in_g), row(ln_in_b), proj, 0, cb_xq, fm, ho, kv,
                w_xattn_o[l].astype(BF16), w_fourier_o[l].astype(BF16), row(b_fourier_o[l]),
                w_hgrn_o[l].astype(BF16), row(b_gate[l]), w_out[l].astype(BF16), row(ln1_g[l]), row(ln1_b[l]), s)

    idx_t, gw_t, rank_t, counts = _route(h1, w_router[l].T.astype(BF16),
                                         router_bias[l].reshape(N_EXPERTS, 1).astype(F32))
    pstart, pend, padded, n_used, n_blocks = _block_plan(counts, t * TOP_K)
    dest_t = _slot_dest(idx_t, rank_t, pstart.astype(F32).reshape(N_EXPERTS, 1))
    dest_flat = dest_t.T.reshape(-1)

    x_sorted = _dispatch(h1, dest_flat, pend, padded, n_used, n_blocks * MOE_BM)
    y_sorted = _experts(x_sorted, pstart, padded, n_used, w_exp_gate[l], w_exp_up[l], w_exp_down[l])
    out = _final(h1, dest_flat, gw_t.T, y_sorted, w_sh_gate[l].astype(BF16), w_sh_up[l].astype(BF16),
                 w_sh_down[l].astype(BF16), row(ln2_g[l]), row(ln2_b[l]))
    return out.reshape(bsz, s, d)
```

```python
import functools
import math

import numpy as np
import jax
import jax.numpy as jnp
from jax import lax
from jax.experimental import pallas as pl
from jax.experimental.pallas import tpu as pltpu

F32 = jnp.float32
BF16 = jnp.bfloat16

LN_EPS = 1e-5
DEPTH = 1
DEEPNORM_ALPHA = (2 * DEPTH) ** 0.25
FOURIER_GROUP_DIM = 128
FOURIER_W = 512
HGRN_HEADS = 4
HGRN_HEAD_DIM = 128
HGRN_W = HGRN_HEADS * HGRN_HEAD_DIM
HGRN_CHUNK = 64
XATTN_HEADS = 4
XATTN_HEAD_DIM = 128
XATTN_W = XATTN_HEADS * XATTN_HEAD_DIM
N_BRANCHES = 3
N_EXPERTS = 256
TOP_K = 8
N_GROUPS = 8
TOPK_GROUPS = 4
GROUP_SIZE = N_EXPERTS // N_GROUPS
ROUTED_SCALE = 2.5

V7X_SCOPED_VMEM_CAP_BYTES = 60000 * 1024

INPROJ_TM = 2048
INPROJ_TN = 512
FOURIER_P = 128
FOURIER_UNROLL = 8
HGRN_ROWS = 512
MERGE_TM = 512
ROUTE_TM = 512
MOE_BM = 256
EXPERT_SLOTS = 8
EXPERT_GROUP = 4
DISPATCH_TM = 256
FINAL_TM = 256
COMBINE_GROUP = 32


def _params(semantics, vmem_bytes):
    return pltpu.CompilerParams(dimension_semantics=semantics,
                                vmem_limit_bytes=int(min(vmem_bytes, V7X_SCOPED_VMEM_CAP_BYTES)))


def _layer_norm(x, g, b):
    mu = jnp.mean(x, axis=-1, keepdims=True)
    xc = x - mu
    var = jnp.mean(xc * xc, axis=-1, keepdims=True)
    return xc * lax.rsqrt(var + LN_EPS) * g + b


def _silu(x):
    return x * jax.nn.sigmoid(x)


def _dot(a, b):
    return jnp.dot(a, b, preferred_element_type=F32)


def _dot_nt(a, b):
    return lax.dot_general(a, b, (((1,), (1,)), ((), ())), preferred_element_type=F32)


def _dot_tn(a, b):
    return lax.dot_general(a, b, (((0,), (0,)), ((), ())), preferred_element_type=F32)


def _inproj_kernel(x_ref, g_ref, b_ref, w_ref, o_ref, hn_ref):
    @pl.when(pl.program_id(1) == 0)
    def _():
        hn_ref[...] = _layer_norm(x_ref[...], g_ref[...], b_ref[...]).astype(BF16)

    o_ref[...] = _dot(hn_ref[...], w_ref[...])


def _inproj(x2, g, b, w):
    t, d = x2.shape
    n = w.shape[1]
    tm, tn = min(INPROJ_TM, t), INPROJ_TN
    vmem = 2 * (tm * d * 4 + d * tn * 2 + tm * tn * 4) + tm * d * 2 + 3 * tm * d * 4
    return pl.pallas_call(
        _inproj_kernel,
        grid=(t // tm, n // tn),
        in_specs=[
            pl.BlockSpec((tm, d), lambda i, j: (i, 0)),
            pl.BlockSpec((1, d), lambda i, j: (0, 0)),
            pl.BlockSpec((1, d), lambda i, j: (0, 0)),
            pl.BlockSpec((d, tn), lambda i, j: (0, j)),
        ],
        out_specs=pl.BlockSpec((tm, tn), lambda i, j: (i, j)),
        out_shape=jax.ShapeDtypeStruct((t, n), F32),
        scratch_shapes=[pltpu.VMEM((tm, d), BF16)],
        compiler_params=_params(("parallel", "arbitrary"), vmem),
        name="ln_inproj",
    )(x2, g, b, w)


def _mem_kv_kernel(m_ref, g_ref, b_ref, w_ref, o_ref):
    mn = _layer_norm(m_ref[0], g_ref[...], b_ref[...]).astype(BF16)
    o_ref[0] = _dot(mn, w_ref[...]).astype(BF16)


def _mem_kv(mem, g, b, w):
    bsz, m, d = mem.shape
    n = w.shape[1]
    vmem = 2 * (m * d * 4 + d * n * 2 + m * n * 2) + 4 * m * d * 4
    return pl.pallas_call(
        _mem_kv_kernel,
        grid=(bsz,),
        in_specs=[
            pl.BlockSpec((1, m, d), lambda i: (i, 0, 0)),
            pl.BlockSpec((1, d), lambda i: (0, 0)),
            pl.BlockSpec((1, d), lambda i: (0, 0)),
            pl.BlockSpec((d, n), lambda i: (0, 0)),
        ],
        out_specs=pl.BlockSpec((1, m, n), lambda i: (i, 0, 0)),
        out_shape=jax.ShapeDtypeStruct((bsz, m, n), BF16),
        compiler_params=_params(("parallel",), vmem),
        name="mem_kv",
    )(mem, g, b, w)


def _fourier_tables(s, c, p):
    q = s // p
    ki = np.arange(p, dtype=np.int64)
    i = np.arange(p, dtype=np.int64)
    j = np.arange(q, dtype=np.int64)
    ph = (ki[None, :, None] * (q * i[None, None, :] + j[:, None, None])) % s
    ang = 2.0 * np.pi * ph.astype(np.float64) / s
    m1 = np.concatenate([np.cos(ang), -np.sin(ang)], axis=1)
    kj = np.arange(q, dtype=np.int64)
    a2 = 2.0 * np.pi * ((kj[:, None] * j[None, :]) % q).astype(np.float64) / q
    m2 = np.block([[np.cos(a2), np.sin(a2)], [-np.sin(a2), np.cos(a2)]])
    cc = np.arange(c, dtype=np.int64)
    a3 = 2.0 * np.pi * ((cc[:, None] * cc[None, :]) % c).astype(np.float64) / c
    mc = np.concatenate([np.cos(a3), np.sin(a3)], axis=0) / math.sqrt(s * c)
    return (jnp.asarray(m1, dtype=BF16), jnp.asarray(m2, dtype=BF16), jnp.asarray(mc, dtype=BF16))


SUBLANES = 8


def _fourier_pitch(q):
    return 2 * q + SUBLANES if (2 * q // SUBLANES) % 2 == 0 else 2 * q


def _fourier_kernel(x_ref, m1_ref, m2_ref, mc_ref, o_ref, ys_ref, *, p, q):
    grp = FOURIER_UNROLL
    pitch = _fourier_pitch(q)

    def stage1(jg, carry):
        js = [jg * grp + u for u in range(grp)]
        xs = [x_ref[pl.ds(j, p, stride=q), :].astype(BF16) for j in js]
        ys = [_dot(m1_ref[j], xj) for j, xj in zip(js, xs)]
        for j, y in zip(js, ys):
            ys_ref[pl.ds(j, p, stride=pitch), :] = y[:p]
            ys_ref[pl.ds(q + j, p, stride=pitch), :] = y[p:]
        return carry

    lax.fori_loop(0, q // grp, stage1, 0)

    def stage2(kg, carry):
        kis = [kg * grp + u for u in range(grp)]
        ys = [ys_ref[pl.ds(pl.multiple_of(ki * pitch, SUBLANES), 2 * q), :].astype(BF16) for ki in kis]
        zs = [_dot(m2_ref[...], y) for y in ys]
        zcs = [jnp.concatenate([z[:q], z[q:]], axis=1).astype(BF16) for z in zs]
        outs = [_dot(zc, mc_ref[...]) for zc in zcs]
        for ki, o in zip(kis, outs):
            o_ref[pl.ds(ki, q, stride=p), :] = o
        return carry

    lax.fori_loop(0, p // grp, stage2, 0)


def _fourier(proj, bsz, s, col_block0):
    c = FOURIER_GROUP_DIM
    groups = FOURIER_W // c
    p = min(FOURIER_P, s // 8)
    q = s // p
    m1, m2, mc = _fourier_tables(s, c, p)
    vmem = 2 * (s * c * 4 + m1.size * 2 + m2.size * 2 + mc.size * 2 + s * c * 4) + 2 * s * c * 4 + 8 * p * c * 4
    return pl.pallas_call(
        functools.partial(_fourier_kernel, p=p, q=q),
        grid=(bsz, groups),
        in_specs=[
            pl.BlockSpec((s, c), lambda b, g: (b, col_block0 + g)),
            pl.BlockSpec((q, 2 * p, p), lambda b, g: (0, 0, 0)),
            pl.BlockSpec((2 * q, 2 * q), lambda b, g: (0, 0)),
            pl.BlockSpec((2 * c, c), lambda b, g: (0, 0)),
        ],
        out_specs=pl.BlockSpec((s, c), lambda b, g: (b, g)),
        out_shape=jax.ShapeDtypeStruct((bsz * s, FOURIER_W), F32),
        scratch_shapes=[pltpu.VMEM((p * _fourier_pitch(q), c), F32)],
        compiler_params=_params(("parallel", "parallel"), vmem),
        name="fourier_mix",
    )(proj, m1, m2, mc)


def _cumsum_dot(ones_mask, x):
    n = x.shape[1]
    hi = x.astype(BF16)
    r1 = x - hi.astype(F32)
    mid = r1.astype(BF16)
    lo = (r1 - mid.astype(F32)).astype(BF16)
    parts = _dot(ones_mask, jnp.concatenate([hi, mid, lo], axis=1))
    return parts[:, 0:n] + parts[:, n:2 * n] + parts[:, 2 * n:3 * n]


def _hgrn_kernel(*refs, reverse, finalize, n_chunks):
    if finalize:
        q_ref, i_ref, z_ref, lb_ref, of_ref, g_ref, ng_ref, o_ref, st_ref = refs
    else:
        q_ref, i_ref, z_ref, lb_ref, o_ref, st_ref = refs

    @pl.when(pl.program_id(1) == 0)
    def _():
        st_ref[...] = jnp.zeros_like(st_ref)

    ck = HGRN_CHUNK
    hd = HGRN_HEAD_DIM
    row = lax.broadcasted_iota(jnp.int32, (ck, ck), 0)
    col = lax.broadcasted_iota(jnp.int32, (ck, ck), 1)
    tri = (col >= row) if reverse else (col <= row)
    cum = jnp.where(tri, 1.0, 0.0).astype(BF16)
    lb = lb_ref[...]

    order = list(range(n_chunks - 1, -1, -1) if reverse else range(n_chunks))
    heads = [slice(h * hd, (h + 1) * hd) for h in range(HGRN_HEADS)]
    rows = {c: pl.ds(c * ck, ck) for c in order}

    q_dec, k_inv, k_end, decay, vb = {}, {}, {}, {}, {}
    for c in order:
        f = lb + (1.0 - lb) * jax.nn.sigmoid(z_ref[rows[c], :])
        k = 1.0 - f
        a = _cumsum_dot(cum, jnp.log(f))
        a_end = a[0:1, :] if reverse else a[ck - 1:ck, :]
        q_dec[c] = (_silu(q_ref[rows[c], :]) * jnp.exp(a)).astype(BF16)
        k_inv[c] = (k * jnp.exp(-a)).astype(BF16)
        k_end[c] = (k * jnp.exp(a_end - a)).astype(BF16)
        decay[c] = jnp.exp(a_end)
        vb[c] = i_ref[rows[c], :].astype(BF16)

    scores = {(c, h): jnp.where(tri, _dot_nt(q_dec[c][:, sl], k_inv[c][:, sl]), 0.0).astype(BF16)
              for c in order for h, sl in enumerate(heads)}
    update = {(c, h): _dot_tn(vb[c][:, sl], k_end[c][:, sl]) for c in order for h, sl in enumerate(heads)}
    o_intra = {(c, h): _dot(scores[c, h], vb[c][:, sl]) for c in order for h, sl in enumerate(heads)}

    state_in = {}
    for h, sl in enumerate(heads):
        st = st_ref[h]
        for c in order:
            state_in[c, h] = st.astype(BF16)
            st = st * decay[c][:, sl] + update[c, h]
        st_ref[h] = st

    o_inter = {(c, h): _dot_nt(q_dec[c][:, sl], state_in[c, h]) for c in order for h, sl in enumerate(heads)}

    for c in order:
        outs = []
        for h, sl in enumerate(heads):
            o_h = o_intra[c, h] + o_inter[c, h]
            if finalize:
                o_h = o_h + of_ref[rows[c], sl]
                ms = jnp.mean(o_h * o_h, axis=-1, keepdims=True)
                o_h = o_h * lax.rsqrt(ms + LN_EPS) * ng_ref[:, sl]
            outs.append(o_h)
        o = jnp.concatenate(outs, axis=1)
        if finalize:
            o_ref[rows[c], :] = (o * _silu(g_ref[rows[c], :])).astype(o_ref.dtype)
        else:
            o_ref[rows[c], :] = o


def _hgrn_sweep(proj, lb, cols, bsz, s, *, reverse, o_fwd=None, norm_g=None):
    w = HGRN_W
    rb = min(HGRN_ROWS, s)
    nb = s // rb
    finalize = o_fwd is not None

    def rmap(b, n):
        return b * nb + (nb - 1 - n if reverse else n)

    def cmap(cb):
        return lambda b, n: (rmap(b, n), cb)

    in_specs = [pl.BlockSpec((rb, w), cmap(cols[0])), pl.BlockSpec((rb, w), cmap(cols[1])),
                pl.BlockSpec((rb, w), cmap(cols[2])), pl.BlockSpec((1, w), lambda b, n: (0, 0))]
    args = [proj, proj, proj, lb]
    if finalize:
        in_specs += [pl.BlockSpec((rb, w), cmap(0)), pl.BlockSpec((rb, w), cmap(cols[3])),
                     pl.BlockSpec((1, w), lambda b, n: (0, 0))]
        args += [o_fwd, proj, norm_g]
    vmem = 2 * (6 * rb * w * 4) + HGRN_HEADS * HGRN_HEAD_DIM * HGRN_HEAD_DIM * 4 + 24 * HGRN_CHUNK * w * 4
    return pl.pallas_call(
        functools.partial(_hgrn_kernel, reverse=reverse, finalize=finalize, n_chunks=rb // HGRN_CHUNK),
        grid=(bsz, nb),
        in_specs=in_specs,
        out_specs=pl.BlockSpec((rb, w), cmap(0)),
        out_shape=jax.ShapeDtypeStruct((bsz * s, w), BF16 if finalize else F32),
        scratch_shapes=[pltpu.VMEM((HGRN_HEADS, HGRN_HEAD_DIM, HGRN_HEAD_DIM), F32)],
        compiler_params=_params(("parallel", "arbitrary"), vmem),
        name="hgrn_bwd" if reverse else "hgrn_fwd",
    )(*args)


def _merge_kernel(x_ref, lg_ref, lbi_ref, gl_ref, xq_ref, fm_ref, ho_ref, kv_ref, wx_ref, wf_ref, bf_ref,
                  wh_ref, bg_ref, wo_ref, g1_ref, b1_ref, o_ref):
    d = x_ref.shape[1]
    h = _layer_norm(x_ref[...], lg_ref[...], lbi_ref[...])

    kv = kv_ref[0]
    xq = xq_ref[...].astype(BF16)
    key_sl = [slice(hh * XATTN_HEAD_DIM, (hh + 1) * XATTN_HEAD_DIM) for hh in range(XATTN_HEADS)]
    val_sl = [slice(XATTN_W + hh * XATTN_HEAD_DIM, XATTN_W + (hh + 1) * XATTN_HEAD_DIM) for hh in range(XATTN_HEADS)]
    logits = [_dot_nt(xq[:, sl], kv[:, sl]) for sl in key_sl]
    y_f = _dot(fm_ref[...].astype(BF16), wf_ref[...]) + bf_ref[...]
    y_h = _dot(ho_ref[...], wh_ref[...])
    probs = []
    for sc in logits:
        sc = sc * (XATTN_HEAD_DIM ** -0.5)
        sc = sc - jnp.max(sc, axis=-1, keepdims=True)
        e = jnp.exp(sc)
        probs.append((e / jnp.sum(e, axis=-1, keepdims=True)).astype(BF16))
    att = jnp.concatenate([_dot(pr, kv[:, sl]) for pr, sl in zip(probs, val_sl)], axis=1).astype(BF16)

    y_x = _dot(att, wx_ref[...])
    gate = jax.nn.sigmoid(gl_ref[...] + bg_ref[...])
    merged = gate[:, 0:d] * y_f + gate[:, d:2 * d] * y_h + gate[:, 2 * d:3 * d] * y_x
    y = _dot(merged.astype(BF16), wo_ref[...])
    o_ref[...] = _layer_norm(DEEPNORM_ALPHA * h + y, g1_ref[...], b1_ref[...])


def _merge(x2, ln_g, ln_b, proj, gate_cb, xq_cb, fm, ho, kv, w_xo, w_fo, b_fo, w_ho, b_gate, w_out, g1, b1, s):
    t, d = x2.shape
    tm = min(MERGE_TM, s)
    per_b = s // tm
    m, kvw = kv.shape[1], kv.shape[2]
    gw = N_BRANCHES * d

    def full(shape):
        return pl.BlockSpec(shape, lambda i: tuple(0 for _ in shape))

    vmem = (2 * (tm * d * 4 + tm * gw * 4 + tm * XATTN_W * 4 + tm * FOURIER_W * 4 + tm * HGRN_W * 2 + m * kvw * 2
                 + 3 * XATTN_W * d * 2 + d * d * 2 + tm * d * 4) + 10 * tm * d * 4 + 2 * tm * gw * 4)
    return pl.pallas_call(
        _merge_kernel,
        grid=(t // tm,),
        in_specs=[
            pl.BlockSpec((tm, d), lambda i: (i, 0)),
            full((1, d)), full((1, d)),
            pl.BlockSpec((tm, gw), lambda i: (i, gate_cb)),
            pl.BlockSpec((tm, XATTN_W), lambda i: (i, xq_cb)),
            pl.BlockSpec((tm, FOURIER_W), lambda i: (i, 0)),
            pl.BlockSpec((tm, HGRN_W), lambda i: (i, 0)),
            pl.BlockSpec((1, m, kvw), lambda i: (i // per_b, 0, 0)),
            full((XATTN_W, d)), full((FOURIER_W, d)), full((1, d)), full((HGRN_W, d)), full((1, gw)),
            full((d, d)), full((1, d)), full((1, d)),
        ],
        out_specs=pl.BlockSpec((tm, d), lambda i: (i, 0)),
        out_shape=jax.ShapeDtypeStruct((t, d), F32),
        compiler_params=_params(("parallel",), vmem),
        name="merge_out",
    )(x2, ln_g, ln_b, proj, proj, fm, ho, kv, w_xo, w_fo, b_fo, w_ho, b_gate, w_out, g1, b1)


def _route_kernel(h_ref, wr_ref, bias_ref, idx_ref, gw_ref, rank_ref, cnt_ref, carry_ref):
    tm = h_ref.shape[0]
    neg = -jnp.inf

    @pl.when(pl.program_id(0) == 0)
    def _():
        carry_ref[...] = jnp.zeros_like(carry_ref)

    aff = jax.nn.sigmoid(_dot_nt(wr_ref[...], h_ref[...].astype(BF16)))
    sel = aff + bias_ref[...]

    giota = lax.broadcasted_iota(jnp.int32, (GROUP_SIZE, tm), 0)
    scores = []
    for g in range(N_GROUPS):
        slab = sel[g * GROUP_SIZE:(g + 1) * GROUP_SIZE, :]
        m1 = jnp.max(slab, axis=0, keepdims=True)
        first = jnp.min(jnp.where(slab == m1, giota, GROUP_SIZE), axis=0, keepdims=True)
        m2 = jnp.max(jnp.where(giota == first, neg, slab), axis=0, keepdims=True)
        scores.append(m1 + m2)
    gs = jnp.concatenate(scores, axis=0)

    grow = lax.broadcasted_iota(jnp.int32, (N_GROUPS, tm), 0)
    gsel = jnp.zeros((N_GROUPS, tm), jnp.bool_)
    work = gs
    for _ in range(TOPK_GROUPS):
        m = jnp.max(work, axis=0, keepdims=True)
        first = jnp.min(jnp.where(work == m, grow, N_GROUPS), axis=0, keepdims=True)
        hit = grow == first
        gsel = jnp.logical_or(gsel, hit)
        work = jnp.where(hit, neg, work)

    masked = jnp.concatenate(
        [jnp.where(gsel[g:g + 1, :], sel[g * GROUP_SIZE:(g + 1) * GROUP_SIZE, :], neg) for g in range(N_GROUPS)],
        axis=0)

    erow = lax.broadcasted_iota(jnp.int32, (N_EXPERTS, tm), 0)
    ids, ws = [], []
    chosen = jnp.zeros((N_EXPERTS, tm), jnp.bool_)
    for _ in range(TOP_K):
        m = jnp.max(masked, axis=0, keepdims=True)
        first = jnp.min(jnp.where(masked == m, erow, N_EXPERTS), axis=0, keepdims=True)
        hit = erow == first
        ids.append(first)
        ws.append(jnp.sum(jnp.where(hit, aff, 0.0), axis=0, keepdims=True))
        masked = jnp.where(hit, neg, masked)
        chosen = jnp.logical_or(chosen, hit)
    w = jnp.concatenate(ws, axis=0)
    idx_ref[...] = jnp.concatenate(ids, axis=0)
    gw_ref[...] = w / jnp.sum(w, axis=0, keepdims=True) * ROUTED_SCALE

    chosen_f = jnp.where(chosen, 1.0, 0.0)
    srow = lax.broadcasted_iota(jnp.int32, (tm, tm), 0)
    scol = lax.broadcasted_iota(jnp.int32, (tm, tm), 1)
    before = jnp.where(srow < scol, 1.0, 0.0).astype(BF16)
    prefix = _dot(chosen_f.astype(BF16), before) + carry_ref[...]
    ranks = [jnp.sum(jnp.where(erow == ids[k], prefix, 0.0), axis=0, keepdims=True) for k in range(TOP_K)]
    rank_ref[...] = jnp.concatenate(ranks, axis=0).astype(jnp.int32)
    carry_ref[...] = carry_ref[...] + jnp.sum(chosen_f, axis=1, keepdims=True)
    cnt_ref[...] = carry_ref[...]


def _route(h1, w_router_t, bias_col):
    t, d = h1.shape
    tm = min(ROUTE_TM, t)
    vmem = 2 * (tm * d * 4 + N_EXPERTS * d * 2 + 3 * TOP_K * tm * 4) + tm * d * 2 + 16 * N_EXPERTS * tm * 4 + tm * tm * 8
    return pl.pallas_call(
        _route_kernel,
        grid=(t // tm,),
        in_specs=[
            pl.BlockSpec((tm, d), lambda i: (i, 0)),
            pl.BlockSpec((N_EXPERTS, d), lambda i: (0, 0)),
            pl.BlockSpec((N_EXPERTS, 1), lambda i: (0, 0)),
        ],
        out_specs=[pl.BlockSpec((TOP_K, tm), lambda i: (0, i)), pl.BlockSpec((TOP_K, tm), lambda i: (0, i)),
                   pl.BlockSpec((TOP_K, tm), lambda i: (0, i)), pl.BlockSpec((N_EXPERTS, 1), lambda i: (0, 0))],
        out_shape=[jax.ShapeDtypeStruct((TOP_K, t), jnp.int32), jax.ShapeDtypeStruct((TOP_K, t), F32),
                   jax.ShapeDtypeStruct((TOP_K, t), jnp.int32), jax.ShapeDtypeStruct((N_EXPERTS, 1), F32)],
        scratch_shapes=[pltpu.VMEM((N_EXPERTS, 1), F32)],
        compiler_params=_params(("arbitrary",), vmem),
        name="router",
    )(h1, w_router_t, bias_col)


def _dest_kernel(idx_ref, rank_ref, ps_ref, o_ref):
    tm = idx_ref.shape[1]
    erow = lax.broadcasted_iota(jnp.int32, (N_EXPERTS, tm), 0)
    ps = ps_ref[...]
    rows = []
    for k in range(TOP_K):
        base = jnp.sum(jnp.where(erow == idx_ref[k:k + 1, :], ps, 0.0), axis=0, keepdims=True)
        rows.append(base.astype(jnp.int32) + rank_ref[k:k + 1, :])
    o_ref[...] = jnp.concatenate(rows, axis=0)


def _slot_dest(idx_t, rank_t, pstart_col):
    t = idx_t.shape[1]
    tm = min(ROUTE_TM, t)
    vmem = 2 * (3 * TOP_K * tm * 4) + 6 * N_EXPERTS * tm * 4
    return pl.pallas_call(
        _dest_kernel,
        grid=(t // tm,),
        in_specs=[pl.BlockSpec((TOP_K, tm), lambda i: (0, i)), pl.BlockSpec((TOP_K, tm), lambda i: (0, i)),
                  pl.BlockSpec((N_EXPERTS, 1), lambda i: (0, 0))],
        out_specs=pl.BlockSpec((TOP_K, tm), lambda i: (0, i)),
        out_shape=jax.ShapeDtypeStruct((TOP_K, t), jnp.int32),
        compiler_params=_params(("parallel",), vmem),
        name="slot_dest",
    )(idx_t, rank_t, pstart_col)


LANES = 128
U32 = jnp.uint32
HIGH_HALF = np.uint32(0xFFFF0000)
N_DMA_PRIORITIES = 2


def _tile_rows(d):
    return d // (2 * LANES)


def _pack_tiles(tile_ref, value, base=0):
    n, d = value.shape
    sub = _tile_rows(d)
    bits = lax.bitcast_convert_type(value.astype(BF16).astype(F32), U32)
    for c in range(sub):
        low = bits[:, c * LANES:(c + 1) * LANES] >> 16
        high = bits[:, (c + sub) * LANES:(c + sub + 1) * LANES] & HIGH_HALF
        tile_ref[pl.ds(base + c, n, stride=sub), :] = low | high


def _unpack_tiles(tile_ref, n, d, base=0):
    sub = _tile_rows(d)
    words = [tile_ref[pl.ds(base + c, n, stride=sub), :] for c in range(sub)]
    lows = [lax.bitcast_convert_type(u << 16, F32) for u in words]
    highs = [lax.bitcast_convert_type(u & HIGH_HALF, F32) for u in words]
    return jnp.concatenate(lows + highs, axis=1)


def _dispatch_kernel(pend_ref, padded_ref, nu_ref, dest_ref, h_ref, xs_ref, tile_ref, zero_ref, sem, zsem):
    tm, d = h_ref.shape
    sub = _tile_rows(d)
    zrows = zero_ref.shape[0]
    n_blocks = xs_ref.shape[0] // zrows

    @pl.when(pl.program_id(0) == 0)
    def _():
        zero_ref[...] = jnp.zeros_like(zero_ref)

        def zero_block(first_row):
            return pltpu.make_async_copy(zero_ref, xs_ref.at[pl.ds(pl.multiple_of(first_row, 8), zrows), :], zsem)

        def issue(e, carry):
            @pl.when(padded_ref[e] > 0)
            def _():
                zero_block(pend_ref[e] * sub - zrows).start()
            return carry

        def drain(e, carry):
            @pl.when(padded_ref[e] > 0)
            def _():
                zero_block(pend_ref[e] * sub - zrows).wait()
            return carry

        def issue_tail(blk, carry):
            zero_block(blk * zrows).start()
            return carry

        def drain_tail(blk, carry):
            zero_block(blk * zrows).wait()
            return carry

        lax.fori_loop(0, N_EXPERTS, issue, 0)
        lax.fori_loop(nu_ref[0], n_blocks, issue_tail, 0)
        lax.fori_loop(0, N_EXPERTS, drain, 0)
        lax.fori_loop(nu_ref[0], n_blocks, drain_tail, 0)

    i = pl.program_id(0)
    slot = lax.rem(i, 2)
    _pack_tiles(tile_ref.at[slot], h_ref[...])

    def row_copy(tok, dst_row):
        return pltpu.make_async_copy(tile_ref.at[slot, pl.ds(pl.multiple_of(tok * sub, sub), sub), :],
                                     xs_ref.at[pl.ds(pl.multiple_of(dst_row * sub, sub), sub), :], sem.at[slot])

    def issue(tok, carry):
        for k in range(TOP_K):
            row_copy(tok, dest_ref[tok * TOP_K + k]).start(priority=k % N_DMA_PRIORITIES)
        return carry

    lax.fori_loop(0, tm, issue, 0)

    def wait_tile(s):
        for _ in range(TOP_K):
            pltpu.make_async_copy(tile_ref.at[s], xs_ref.at[pl.ds(0, tm * sub), :], sem.at[s]).wait()

    @pl.when(i > 0)
    def _():
        wait_tile(1 - slot)

    @pl.when(i == pl.num_programs(0) - 1)
    def _():
        wait_tile(slot)


def _dispatch(h1, dest_flat, pend, padded, n_used, n_rows_pad):
    t, d = h1.shape
    sub = _tile_rows(d)
    tm = min(DISPATCH_TM, t)
    vmem = 2 * (tm * d * 4) + tm * d * 2 + MOE_BM * d * 2 + 4 * tm * d * 4
    return pl.pallas_call(
        _dispatch_kernel,
        grid_spec=pltpu.PrefetchScalarGridSpec(
            num_scalar_prefetch=3,
            grid=(t // tm,),
            in_specs=[
                pl.BlockSpec((tm * TOP_K,), lambda i, pe, pa, nu: (i,), memory_space=pltpu.SMEM),
                pl.BlockSpec((tm, d), lambda i, pe, pa, nu: (i, 0)),
            ],
            out_specs=pl.BlockSpec(memory_space=pl.ANY),
            scratch_shapes=[pltpu.VMEM((2, tm * sub, LANES), U32), pltpu.VMEM((MOE_BM * sub, LANES), U32),
                            pltpu.SemaphoreType.DMA((2,)), pltpu.SemaphoreType.DMA(())],
        ),
        out_shape=jax.ShapeDtypeStruct((n_rows_pad * sub, LANES), U32),
        compiler_params=_params(("arbitrary",), vmem),
        name="dispatch_rows",
    )(pend, padded, n_used, dest_flat, h1)


def _expert_kernel(ps_ref, pd_ref, nu_ref, wg_ref, wu_ref, wd_ref, x_hbm, y_hbm,
                   xbuf, ybuf, wgb_ref, wub_ref, wdb_ref, xsem, ysem):
    e = pl.program_id(0)
    d = wg_ref.shape[1]
    slots = xbuf.shape[0]
    group = EXPERT_GROUP
    ahead = slots - group
    rows = xbuf.shape[1]
    bm = rows // _tile_rows(d)
    n_blocks = y_hbm.shape[0] // rows
    n_used = nu_ref[0]
    first = ps_ref[e] // bm
    nb = pd_ref[e] // bm

    def x_copy(g, slot):
        return pltpu.make_async_copy(x_hbm.at[pl.ds(pl.multiple_of(g * rows, rows), rows), :], xbuf.at[slot],
                                     xsem.at[slot])

    def y_copy(g, slot):
        return pltpu.make_async_copy(ybuf.at[slot], y_hbm.at[pl.ds(pl.multiple_of(g * rows, rows), rows), :],
                                     ysem.at[slot])

    @pl.when(e == 0)
    def _():
        for g0 in range(ahead):
            @pl.when(g0 < n_used)
            def _():
                x_copy(g0, g0).start()

    @pl.when(nb > 0)
    def _():
        wgb_ref[...] = wg_ref[0].astype(BF16)
        wub_ref[...] = wu_ref[0].astype(BF16)
        wdb_ref[...] = wd_ref[0].astype(BF16)

        def acquire(g):
            slot = lax.rem(g, slots)
            x_copy(g, slot).wait()

            @pl.when(g + ahead < n_used)
            def _():
                x_copy(g + ahead, lax.rem(g + ahead, slots)).start()

            @pl.when(g >= slots)
            def _():
                y_copy(g - slots, slot).wait()

            return slot

        def compute(block_slots):
            xs = [_unpack_tiles(xbuf.at[s], bm, d).astype(BF16) for s in block_slots]
            gates = [_dot(xb, wgb_ref[...]) for xb in xs]
            ups = [_dot(xb, wub_ref[...]) for xb in xs]
            hs = [(_silu(hg) * hu).astype(BF16) for hg, hu in zip(gates, ups)]
            ys = [_dot(hb, wdb_ref[...]) for hb in hs]
            for s, y in zip(block_slots, ys):
                _pack_tiles(ybuf.at[s], y)

        def run_group(g, n):
            block_slots = [acquire(g + u) for u in range(n)]
            compute(block_slots)
            for u, s in enumerate(block_slots):
                y_copy(g + u, s).start()

        def full_group(jj, carry):
            run_group(first + group * jj, group)
            return carry

        lax.fori_loop(0, nb // group, full_group, 0)
        done = nb - lax.rem(nb, group)
        size = group // 2
        while size >= 1:
            @pl.when(lax.rem(nb, 2 * size) >= size)
            def _(size=size, done=done):
                run_group(first + done, size)

            done = done + jnp.where(lax.rem(nb, 2 * size) >= size, size, 0)
            size //= 2

    @pl.when(e == pl.num_programs(0) - 1)
    def _():
        def drain_used(g, carry):
            y_copy(g, lax.rem(g, slots)).wait()
            return carry

        lax.fori_loop(jnp.maximum(n_used - slots, 0), n_used, drain_used, 0)
        ybuf[0] = jnp.zeros_like(ybuf[0])

        def issue(g, carry):
            y_copy(g, 0).start()
            return carry

        def drain(g, carry):
            y_copy(g, 0).wait()
            return carry

        lax.fori_loop(n_used, n_blocks, issue, 0)
        lax.fori_loop(n_used, n_blocks, drain, 0)


def _experts(x_sorted, pstart, padded, n_used, w_gate, w_up, w_down):
    n_exp, d, ff = w_gate.shape
    sub = _tile_rows(d)
    bm = MOE_BM
    vmem = 2 * (3 * d * ff * 4) + 3 * d * ff * 2 + 2 * EXPERT_SLOTS * bm * d * 2 + 6 * bm * ff * 4 + 6 * bm * d * 4
    return pl.pallas_call(
        _expert_kernel,
        grid_spec=pltpu.PrefetchScalarGridSpec(
            num_scalar_prefetch=3,
            grid=(n_exp,),
            in_specs=[
                pl.BlockSpec((1, d, ff), lambda e, ps, pd, nu: (e, 0, 0)),
                pl.BlockSpec((1, d, ff), lambda e, ps, pd, nu: (e, 0, 0)),
                pl.BlockSpec((1, ff, d), lambda e, ps, pd, nu: (e, 0, 0)),
                pl.BlockSpec(memory_space=pl.ANY),
            ],
            out_specs=pl.BlockSpec(memory_space=pl.ANY),
            scratch_shapes=[pltpu.VMEM((EXPERT_SLOTS, bm * sub, LANES), U32),
                            pltpu.VMEM((EXPERT_SLOTS, bm * sub, LANES), U32),
                            pltpu.VMEM((d, ff), BF16), pltpu.VMEM((d, ff), BF16), pltpu.VMEM((ff, d), BF16),
                            pltpu.SemaphoreType.DMA((EXPERT_SLOTS,)), pltpu.SemaphoreType.DMA((EXPERT_SLOTS,))],
        ),
        out_shape=jax.ShapeDtypeStruct(x_sorted.shape, U32),
        compiler_params=_params(("arbitrary",), vmem),
        name="expert_mlp",
    )(pstart, padded, n_used, w_gate, w_up, w_down, x_sorted)


def _final_kernel(dest_ref, dest_next_ref, h_ref, gw_ref, y_ref, wg_ref, wu_ref, wd_ref, g_ref, b_ref, o_ref,
                  rows_a, rows_b, shared_ref, sem):
    i = pl.program_id(0)
    tm, d = h_ref.shape
    sub = _tile_rows(d)
    grp = COMBINE_GROUP

    def issue_tokens(dref, buf, s, tok0):
        for u in range(grp):
            for k in range(TOP_K):
                src_row = dref[(tok0 + u) * TOP_K + k]
                pltpu.make_async_copy(
                    y_ref.at[pl.ds(pl.multiple_of(src_row * sub, sub), sub), :],
                    buf.at[pl.ds(pl.multiple_of((k * tm + tok0 + u) * sub, sub), sub), :],
                    sem.at[s]).start(priority=k % N_DMA_PRIORITIES)

    def wait_tile(buf, s):
        pltpu.make_async_copy(y_ref.at[pl.ds(0, TOP_K * tm * sub), :], buf, sem.at[s]).wait()

    @pl.when(i == 0)
    def _():
        def first(j, carry):
            issue_tokens(dest_ref, rows_a, 0, j * grp)
            return carry

        lax.fori_loop(0, tm // grp, first, 0)

    hb = h_ref[...].astype(BF16)
    shared_ref[...] = _dot((_silu(_dot(hb, wg_ref[...])) * _dot(hb, wu_ref[...])).astype(BF16), wd_ref[...])
    ln_g = g_ref[...]
    ln_b = b_ref[...]

    def reduce_tile(cur, cur_s, nxt, nxt_s):
        wait_tile(cur, cur_s)

        def reduce_group(j, carry):
            tok0 = j * grp if isinstance(j, int) else pl.multiple_of(j * grp, grp)
            issue_tokens(dest_next_ref, nxt, nxt_s, tok0)
            tok = pl.ds(tok0, grp)
            gw = gw_ref[tok, :]
            routed = gw[:, 0:1] * _unpack_tiles(cur, grp, d, base=tok0 * sub)
            for k in range(1, TOP_K):
                routed = routed + gw[:, k:k + 1] * _unpack_tiles(cur, grp, d, base=(k * tm + tok0) * sub)
            o_ref[tok, :] = _layer_norm(DEEPNORM_ALPHA * h_ref[tok, :] + (routed + shared_ref[tok, :]), ln_g, ln_b)
            return carry

        for j in range(tm // grp):
            reduce_group(j, 0)

        @pl.when(i == pl.num_programs(0) - 1)
        def _():
            wait_tile(nxt, nxt_s)

    @pl.when(lax.rem(i, 2) == 0)
    def _():
        reduce_tile(rows_a, 0, rows_b, 1)

    @pl.when(lax.rem(i, 2) == 1)
    def _():
        reduce_tile(rows_b, 1, rows_a, 0)


def _final(h1, dest_flat, gw, y_sorted, w_sg, w_su, w_sd, g2, b2):
    t, d = h1.shape
    ff = w_sg.shape[1]
    sub = _tile_rows(d)
    tm = min(FINAL_TM, t)
    nt = t // tm
    vmem = 2 * (2 * tm * d * 4 + tm * LANES * 4 + 3 * d * ff * 2) + 2 * TOP_K * tm * d * 2 + 8 * tm * d * 4

    def full(shape):
        return pl.BlockSpec(shape, lambda i: tuple(0 for _ in shape))

    return pl.pallas_call(
        _final_kernel,
        grid=(nt,),
        in_specs=[
            pl.BlockSpec((tm * TOP_K,), lambda i: (i,), memory_space=pltpu.SMEM),
            pl.BlockSpec((tm * TOP_K,), lambda i: (jnp.minimum(i + 1, nt - 1),), memory_space=pltpu.SMEM),
            pl.BlockSpec((tm, d), lambda i: (i, 0)),
            pl.BlockSpec((tm, TOP_K), lambda i: (i, 0)),
            pl.BlockSpec(memory_space=pl.ANY),
            full((d, ff)), full((d, ff)), full((ff, d)), full((1, d)), full((1, d)),
        ],
        out_specs=pl.BlockSpec((tm, d), lambda i: (i, 0)),
        out_shape=jax.ShapeDtypeStruct((t, d), F32),
        scratch_shapes=[pltpu.VMEM((TOP_K * tm * sub, LANES), U32), pltpu.VMEM((TOP_K * tm * sub, LANES), U32),
                        pltpu.VMEM((tm, d), F32), pltpu.SemaphoreType.DMA((2,))],
        compiler_params=_params(("arbitrary",), vmem),
        name="combine_final",
    )(dest_flat, dest_flat, h1, gw, y_sorted, w_sg, w_su, w_sd, g2, b2)


def _lower_bound(p):
    return jnp.cumsum(jax.nn.softmax(p.astype(F32), axis=0), axis=0)[0:1]


def _block_plan(counts, n_rows):
    bm = MOE_BM
    counts = counts.reshape(-1).astype(jnp.int32)
    padded = (counts + bm - 1) // bm * bm
    pend = jnp.cumsum(padded).astype(jnp.int32)
    pstart = pend - padded
    n_blocks = n_rows // bm + N_EXPERTS
    n_used = pend[-1:] // bm
    return pstart, pend, padded, n_used.astype(jnp.int32), n_blocks


def kernel(x, mem, ln_in_g, ln_in_b, ln_mem_g, ln_mem_b, hgrn_lb_fwd, hgrn_lb_bwd, w_in, b_gate, hgrn_norm_g, w_mem_kv, w_fourier_o, b_fourier_o, w_hgrn_o, w_xattn_o, w_out, ln1_g, ln1_b, w_router, router_bias, w_exp_gate, w_exp_up, w_exp_down, w_sh_gate, w_sh_up, w_sh_down, ln2_g, ln2_b):
    bsz, s, d = x.shape
    t = bsz * s
    l = 0
    row = lambda v: v.reshape(1, -1).astype(F32)

    n_gate = N_BRANCHES * d
    n_rest = w_in.shape[2] - n_gate
    w_in_r = jnp.concatenate([w_in[l][:, n_rest:], w_in[l][:, :n_rest]], axis=1).astype(BF16)
    gate_blocks = n_gate // INPROJ_TN
    cb_fourier = gate_blocks
    cb_hq, cb_hi, cb_zf, cb_zb, cb_hg, cb_xq = (gate_blocks + 1 + n for n in range(6))

    x2 = x.reshape(t, d)
    proj = _inproj(x2, row(ln_in_g), row(ln_in_b), w_in_r)
    kv = _mem_kv(mem, row(ln_mem_g), row(ln_mem_b), w_mem_kv[l].astype(BF16))

    fm = _fourier(proj, bsz, s, cb_fourier * (INPROJ_TN // FOURIER_GROUP_DIM))

    lb_f = _lower_bound(hgrn_lb_fwd)
    lb_b = _lower_bound(hgrn_lb_bwd)
    o_fwd = _hgrn_sweep(proj, lb_f, (cb_hq, cb_hi, cb_zf), bsz, s, reverse=False)
    ho = _hgrn_sweep(proj, lb_b, (cb_hq, cb_hi, cb_zb, cb_hg), bsz, s, reverse=True,
                     o_fwd=o_fwd, norm_g=row(hgrn_norm_g[l]))

    h1 = _merge(x2, row(ln_in_g), row(ln_in_b), proj, 0, cb_xq, fm, ho, kv,
                w_xattn_o[l].astype(BF16), w_fourier_o[l].astype(BF16), row(b_fourier_o[l]),
                w_hgrn_o[l].astype(BF16), row(b_gate[l]), w_out[l].astype(BF16), row(ln1_g[l]), row(ln1_b[l]), s)

    idx_t, gw_t, rank_t, counts = _route(h1, w_router[l].T.astype(BF16),
                                         router_bias[l].reshape(N_EXPERTS, 1).astype(F32))
    pstart, pend, padded, n_used, n_blocks = _block_plan(counts, t * TOP_K)
    dest_t = _slot_dest(idx_t, rank_t, pstart.astype(F32).reshape(N_EXPERTS, 1))
    dest_flat = dest_t.T.reshape(-1)

    x_sorted = _dispatch(h1, dest_flat, pend, padded, n_used, n_blocks * MOE_BM)
    y_sorted = _experts(x_sorted, pstart, padded, n_used, w_exp_gate[l], w_exp_up[l], w_exp_down[l])
    out = _final(h1, dest_flat, gw_t.T, y_sorted, w_sh_gate[l].astype(BF16), w_sh_up[l].astype(BF16),
                 w_sh_down[l].astype(BF16), row(ln2_g[l]), row(ln2_b[l]))
    return out.reshape(bsz, s, d)
```

```python
import functools
import math

import numpy as np
import jax
import jax.numpy as jnp
from jax import lax
from jax.experimental import pallas as pl
from jax.experimental.pallas import tpu as pltpu

F32 = jnp.float32
BF16 = jnp.bfloat16

LN_EPS = 1e-5
DEPTH = 1
DEEPNORM_ALPHA = (2 * DEPTH) ** 0.25
FOURIER_GROUP_DIM = 128
FOURIER_W = 512
HGRN_HEADS = 4
HGRN_HEAD_DIM = 128
HGRN_W = HGRN_HEADS * HGRN_HEAD_DIM
HGRN_CHUNK = 64
XATTN_HEADS = 4
XATTN_HEAD_DIM = 128
XATTN_W = XATTN_HEADS * XATTN_HEAD_DIM
N_BRANCHES = 3
N_EXPERTS = 256
TOP_K = 8
N_GROUPS = 8
TOPK_GROUPS = 4
GROUP_SIZE = N_EXPERTS // N_GROUPS
ROUTED_SCALE = 2.5

V7X_SCOPED_VMEM_CAP_BYTES = 60000 * 1024

INPROJ_TM = 2048
INPROJ_TN = 512
FOURIER_P = 128
FOURIER_UNROLL = 8
HGRN_ROWS = 512
MERGE_TM = 512
ROUTE_TM = 512
MOE_BM = 256
EXPERT_SLOTS = 8
EXPERT_GROUP = 2
DISPATCH_TM = 256
FINAL_TM = 256
COMBINE_GROUP = 32


def _params(semantics, vmem_bytes):
    return pltpu.CompilerParams(dimension_semantics=semantics,
                                vmem_limit_bytes=int(min(vmem_bytes, V7X_SCOPED_VMEM_CAP_BYTES)))


def _layer_norm(x, g, b):
    mu = jnp.mean(x, axis=-1, keepdims=True)
    xc = x - mu
    var = jnp.mean(xc * xc, axis=-1, keepdims=True)
    return xc * lax.rsqrt(var + LN_EPS) * g + b


def _silu(x):
    return x * jax.nn.sigmoid(x)


def _dot(a, b):
    return jnp.dot(a, b, preferred_element_type=F32)


def _dot_nt(a, b):
    return lax.dot_general(a, b, (((1,), (1,)), ((), ())), preferred_element_type=F32)


def _dot_tn(a, b):
    return lax.dot_general(a, b, (((0,), (0,)), ((), ())), preferred_element_type=F32)


def _inproj_kernel(x_ref, g_ref, b_ref, w_ref, o_ref, hn_ref):
    @pl.when(pl.program_id(1) == 0)
    def _():
        hn_ref[...] = _layer_norm(x_ref[...], g_ref[...], b_ref[...]).astype(BF16)

    o_ref[...] = _dot(hn_ref[...], w_ref[...])


def _inproj(x2, g, b, w):
    t, d = x2.shape
    n = w.shape[1]
    tm, tn = min(INPROJ_TM, t), INPROJ_TN
    vmem = 2 * (tm * d * 4 + d * tn * 2 + tm * tn * 4) + tm * d * 2 + 3 * tm * d * 4
    return pl.pallas_call(
        _inproj_kernel,
        grid=(t // tm, n // tn),
        in_specs=[
            pl.BlockSpec((tm, d), lambda i, j: (i, 0)),
            pl.BlockSpec((1, d), lambda i, j: (0, 0)),
            pl.BlockSpec((1, d), lambda i, j: (0, 0)),
            pl.BlockSpec((d, tn), lambda i, j: (0, j)),
        ],
        out_specs=pl.BlockSpec((tm, tn), lambda i, j: (i, j)),
        out_shape=jax.ShapeDtypeStruct((t, n), F32),
        scratch_shapes=[pltpu.VMEM((tm, d), BF16)],
        compiler_params=_params(("parallel", "arbitrary"), vmem),
        name="ln_inproj",
    )(x2, g, b, w)


def _mem_kv_kernel(m_ref, g_ref, b_ref, w_ref, o_ref):
    mn = _layer_norm(m_ref[0], g_ref[...], b_ref[...]).astype(BF16)
    o_ref[0] = _dot(mn, w_ref[...]).astype(BF16)


def _mem_kv(mem, g, b, w):
    bsz, m, d = mem.shape
    n = w.shape[1]
    vmem = 2 * (m * d * 4 + d * n * 2 + m * n * 2) + 4 * m * d * 4
    return pl.pallas_call(
        _mem_kv_kernel,
        grid=(bsz,),
        in_specs=[
            pl.BlockSpec((1, m, d), lambda i: (i, 0, 0)),
            pl.BlockSpec((1, d), lambda i: (0, 0)),
            pl.BlockSpec((1, d), lambda i: (0, 0)),
            pl.BlockSpec((d, n), lambda i: (0, 0)),
        ],
        out_specs=pl.BlockSpec((1, m, n), lambda i: (i, 0, 0)),
        out_shape=jax.ShapeDtypeStruct((bsz, m, n), BF16),
        compiler_params=_params(("parallel",), vmem),
        name="mem_kv",
    )(mem, g, b, w)


def _fourier_tables(s, c, p):
    q = s // p
    ki = np.arange(p, dtype=np.int64)
    i = np.arange(p, dtype=np.int64)
    j = np.arange(q, dtype=np.int64)
    ph = (ki[None, :, None] * (q * i[None, None, :] + j[:, None, None])) % s
    ang = 2.0 * np.pi * ph.astype(np.float64) / s
    m1 = np.concatenate([np.cos(ang), -np.sin(ang)], axis=1)
    kj = np.arange(q, dtype=np.int64)
    a2 = 2.0 * np.pi * ((kj[:, None] * j[None, :]) % q).astype(np.float64) / q
    m2 = np.block([[np.cos(a2), np.sin(a2)], [-np.sin(a2), np.cos(a2)]])
    cc = np.arange(c, dtype=np.int64)
    a3 = 2.0 * np.pi * ((cc[:, None] * cc[None, :]) % c).astype(np.float64) / c
    mc = np.concatenate([np.cos(a3), np.sin(a3)], axis=0) / math.sqrt(s * c)
    return (jnp.asarray(m1, dtype=BF16), jnp.asarray(m2, dtype=BF16), jnp.asarray(mc, dtype=BF16))


SUBLANES = 8


def _fourier_pitch(q):
    return 2 * q + SUBLANES if (2 * q // SUBLANES) % 2 == 0 else 2 * q


def _fourier_kernel(x_ref, m1_ref, m2_ref, mc_ref, o_ref, ys_ref, *, p, q):
    grp = FOURIER_UNROLL
    pitch = _fourier_pitch(q)

    def stage1(jg, carry):
        js = [jg * grp + u for u in range(grp)]
        xs = [x_ref[pl.ds(j, p, stride=q), :].astype(BF16) for j in js]
        ys = [_dot(m1_ref[j], xj) for j, xj in zip(js, xs)]
        for j, y in zip(js, ys):
            ys_ref[pl.ds(j, p, stride=pitch), :] = y[:p]
            ys_ref[pl.ds(q + j, p, stride=pitch), :] = y[p:]
        return carry

    lax.fori_loop(0, q // grp, stage1, 0)

    def stage2(kg, carry):
        kis = [kg * grp + u for u in range(grp)]
        ys = [ys_ref[pl.ds(pl.multiple_of(ki * pitch, SUBLANES), 2 * q), :].astype(BF16) for ki in kis]
        zs = [_dot(m2_ref[...], y) for y in ys]
        zcs = [jnp.concatenate([z[:q], z[q:]], axis=1).astype(BF16) for z in zs]
        outs = [_dot(zc, mc_ref[...]) for zc in zcs]
        for ki, o in zip(kis, outs):
            o_ref[pl.ds(ki, q, stride=p), :] = o
        return carry

    lax.fori_loop(0, p // grp, stage2, 0)


def _fourier(proj, bsz, s, col_block0):
    c = FOURIER_GROUP_DIM
    groups = FOURIER_W // c
    p = min(FOURIER_P, s // 8)
    q = s // p
    m1, m2, mc = _fourier_tables(s, c, p)
    vmem = 2 * (s * c * 4 + m1.size * 2 + m2.size * 2 + mc.size * 2 + s * c * 4) + 2 * s * c * 4 + 8 * p * c * 4
    return pl.pallas_call(
        functools.partial(_fourier_kernel, p=p, q=q),
        grid=(bsz, groups),
        in_specs=[
            pl.BlockSpec((s, c), lambda b, g: (b, col_block0 + g)),
            pl.BlockSpec((q, 2 * p, p), lambda b, g: (0, 0, 0)),
            pl.BlockSpec((2 * q, 2 * q), lambda b, g: (0, 0)),
            pl.BlockSpec((2 * c, c), lambda b, g: (0, 0)),
        ],
        out_specs=pl.BlockSpec((s, c), lambda b, g: (b, g)),
        out_shape=jax.ShapeDtypeStruct((bsz * s, FOURIER_W), F32),
        scratch_shapes=[pltpu.VMEM((p * _fourier_pitch(q), c), F32)],
        compiler_params=_params(("parallel", "parallel"), vmem),
        name="fourier_mix",
    )(proj, m1, m2, mc)


def _cumsum_dot(ones_mask, x):
    n = x.shape[1]
    hi = x.astype(BF16)
    r1 = x - hi.astype(F32)
    mid = r1.astype(BF16)
    lo = (r1 - mid.astype(F32)).astype(BF16)
    parts = _dot(ones_mask, jnp.concatenate([hi, mid, lo], axis=1))
    return parts[:, 0:n] + parts[:, n:2 * n] + parts[:, 2 * n:3 * n]


def _hgrn_kernel(*refs, reverse, finalize, n_chunks):
    if finalize:
        q_ref, i_ref, z_ref, lb_ref, of_ref, g_ref, ng_ref, o_ref, st_ref = refs
    else:
        q_ref, i_ref, z_ref, lb_ref, o_ref, st_ref = refs

    @pl.when(pl.program_id(1) == 0)
    def _():
        st_ref[...] = jnp.zeros_like(st_ref)

    ck = HGRN_CHUNK
    hd = HGRN_HEAD_DIM
    row = lax.broadcasted_iota(jnp.int32, (ck, ck), 0)
    col = lax.broadcasted_iota(jnp.int32, (ck, ck), 1)
    tri = (col >= row) if reverse else (col <= row)
    cum = jnp.where(tri, 1.0, 0.0).astype(BF16)
    lb = lb_ref[...]

    order = list(range(n_chunks - 1, -1, -1) if reverse else range(n_chunks))
    heads = [slice(h * hd, (h + 1) * hd) for h in range(HGRN_HEADS)]
    rows = {c: pl.ds(c * ck, ck) for c in order}

    q_dec, k_inv, k_end, decay, vb = {}, {}, {}, {}, {}
    for c in order:
        f = lb + (1.0 - lb) * jax.nn.sigmoid(z_ref[rows[c], :])
        k = 1.0 - f
        a = _cumsum_dot(cum, jnp.log(f))
        a_end = a[0:1, :] if reverse else a[ck - 1:ck, :]
        q_dec[c] = (_silu(q_ref[rows[c], :]) * jnp.exp(a)).astype(BF16)
        k_inv[c] = (k * jnp.exp(-a)).astype(BF16)
        k_end[c] = (k * jnp.exp(a_end - a)).astype(BF16)
        decay[c] = jnp.exp(a_end)
        vb[c] = i_ref[rows[c], :].astype(BF16)

    scores = {(c, h): jnp.where(tri, _dot_nt(q_dec[c][:, sl], k_inv[c][:, sl]), 0.0).astype(BF16)
              for c in order for h, sl in enumerate(heads)}
    update = {(c, h): _dot_tn(vb[c][:, sl], k_end[c][:, sl]) for c in order for h, sl in enumerate(heads)}
    o_intra = {(c, h): _dot(scores[c, h], vb[c][:, sl]) for c in order for h, sl in enumerate(heads)}

    state_in = {}
    for h, sl in enumerate(heads):
        st = st_ref[h]
        for c in order:
            state_in[c, h] = st.astype(BF16)
            st = st * decay[c][:, sl] + update[c, h]
        st_ref[h] = st

    o_inter = {(c, h): _dot_nt(q_dec[c][:, sl], state_in[c, h]) for c in order for h, sl in enumerate(heads)}

    for c in order:
        outs = []
        for h, sl in enumerate(heads):
            o_h = o_intra[c, h] + o_inter[c, h]
            if finalize:
                o_h = o_h + of_ref[rows[c], sl]
                ms = jnp.mean(o_h * o_h, axis=-1, keepdims=True)
                o_h = o_h * lax.rsqrt(ms + LN_EPS) * ng_ref[:, sl]
            outs.append(o_h)
        o = jnp.concatenate(outs, axis=1)
        if finalize:
            o_ref[rows[c], :] = (o * _silu(g_ref[rows[c], :])).astype(o_ref.dtype)
        else:
            o_ref[rows[c], :] = o


def _hgrn_sweep(proj, lb, cols, bsz, s, *, reverse, o_fwd=None, norm_g=None):
    w = HGRN_W
    rb = min(HGRN_ROWS, s)
    nb = s // rb
    finalize = o_fwd is not None

    def rmap(b, n):
        return b * nb + (nb - 1 - n if reverse else n)

    def cmap(cb):
        return lambda b, n: (rmap(b, n), cb)

    in_specs = [pl.BlockSpec((rb, w), cmap(cols[0])), pl.BlockSpec((rb, w), cmap(cols[1])),
                pl.BlockSpec((rb, w), cmap(cols[2])), pl.BlockSpec((1, w), lambda b, n: (0, 0))]
    args = [proj, proj, proj, lb]
    if finalize:
        in_specs += [pl.BlockSpec((rb, w), cmap(0)), pl.BlockSpec((rb, w), cmap(cols[3])),
                     pl.BlockSpec((1, w), lambda b, n: (0, 0))]
        args += [o_fwd, proj, norm_g]
    vmem = 2 * (6 * rb * w * 4) + HGRN_HEADS * HGRN_HEAD_DIM * HGRN_HEAD_DIM * 4 + 24 * HGRN_CHUNK * w * 4
    return pl.pallas_call(
        functools.partial(_hgrn_kernel, reverse=reverse, finalize=finalize, n_chunks=rb // HGRN_CHUNK),
        grid=(bsz, nb),
        in_specs=in_specs,
        out_specs=pl.BlockSpec((rb, w), cmap(0)),
        out_shape=jax.ShapeDtypeStruct((bsz * s, w), BF16 if finalize else F32),
        scratch_shapes=[pltpu.VMEM((HGRN_HEADS, HGRN_HEAD_DIM, HGRN_HEAD_DIM), F32)],
        compiler_params=_params(("parallel", "arbitrary"), vmem),
        name="hgrn_bwd" if reverse else "hgrn_fwd",
    )(*args)


def _merge_kernel(x_ref, lg_ref, lbi_ref, gl_ref, xq_ref, fm_ref, ho_ref, kv_ref, wx_ref, wf_ref, bf_ref,
                  wh_ref, bg_ref, wo_ref, g1_ref, b1_ref, o_ref):
    d = x_ref.shape[1]
    h = _layer_norm(x_ref[...], lg_ref[...], lbi_ref[...])

    kv = kv_ref[0]
    xq = xq_ref[...].astype(BF16)
    key_sl = [slice(hh * XATTN_HEAD_DIM, (hh + 1) * XATTN_HEAD_DIM) for hh in range(XATTN_HEADS)]
    val_sl = [slice(XATTN_W + hh * XATTN_HEAD_DIM, XATTN_W + (hh + 1) * XATTN_HEAD_DIM) for hh in range(XATTN_HEADS)]
    logits = [_dot_nt(xq[:, sl], kv[:, sl]) for sl in key_sl]
    y_f = _dot(fm_ref[...].astype(BF16), wf_ref[...]) + bf_ref[...]
    y_h = _dot(ho_ref[...], wh_ref[...])
    probs = []
    for sc in logits:
        sc = sc * (XATTN_HEAD_DIM ** -0.5)
        sc = sc - jnp.max(sc, axis=-1, keepdims=True)
        e = jnp.exp(sc)
        probs.append((e / jnp.sum(e, axis=-1, keepdims=True)).astype(BF16))
    att = jnp.concatenate([_dot(pr, kv[:, sl]) for pr, sl in zip(probs, val_sl)], axis=1).astype(BF16)

    y_x = _dot(att, wx_ref[...])
    gate = jax.nn.sigmoid(gl_ref[...] + bg_ref[...])
    merged = gate[:, 0:d] * y_f + gate[:, d:2 * d] * y_h + gate[:, 2 * d:3 * d] * y_x
    y = _dot(merged.astype(BF16), wo_ref[...])
    o_ref[...] = _layer_norm(DEEPNORM_ALPHA * h + y, g1_ref[...], b1_ref[...])


def _merge(x2, ln_g, ln_b, proj, gate_cb, xq_cb, fm, ho, kv, w_xo, w_fo, b_fo, w_ho, b_gate, w_out, g1, b1, s):
    t, d = x2.shape
    tm = min(MERGE_TM, s)
    per_b = s // tm
    m, kvw = kv.shape[1], kv.shape[2]
    gw = N_BRANCHES * d

    def full(shape):
        return pl.BlockSpec(shape, lambda i: tuple(0 for _ in shape))

    vmem = (2 * (tm * d * 4 + tm * gw * 4 + tm * XATTN_W * 4 + tm * FOURIER_W * 4 + tm * HGRN_W * 2 + m * kvw * 2
                 + 3 * XATTN_W * d * 2 + d * d * 2 + tm * d * 4) + 10 * tm * d * 4 + 2 * tm * gw * 4)
    return pl.pallas_call(
        _merge_kernel,
        grid=(t // tm,),
        in_specs=[
            pl.BlockSpec((tm, d), lambda i: (i, 0)),
            full((1, d)), full((1, d)),
            pl.BlockSpec((tm, gw), lambda i: (i, gate_cb)),
            pl.BlockSpec((tm, XATTN_W), lambda i: (i, xq_cb)),
            pl.BlockSpec((tm, FOURIER_W), lambda i: (i, 0)),
            pl.BlockSpec((tm, HGRN_W), lambda i: (i, 0)),
            pl.BlockSpec((1, m, kvw), lambda i: (i // per_b, 0, 0)),
            full((XATTN_W, d)), full((FOURIER_W, d)), full((1, d)), full((HGRN_W, d)), full((1, gw)),
            full((d, d)), full((1, d)), full((1, d)),
        ],
        out_specs=pl.BlockSpec((tm, d), lambda i: (i, 0)),
        out_shape=jax.ShapeDtypeStruct((t, d), F32),
        compiler_params=_params(("parallel",), vmem),
        name="merge_out",
    )(x2, ln_g, ln_b, proj, proj, fm, ho, kv, w_xo, w_fo, b_fo, w_ho, b_gate, w_out, g1, b1)


def _route_kernel(h_ref, wr_ref, bias_ref, idx_ref, gw_ref, rank_ref, cnt_ref, carry_ref):
    tm = h_ref.shape[0]
    neg = -jnp.inf

    @pl.when(pl.program_id(0) == 0)
    def _():
        carry_ref[...] = jnp.zeros_like(carry_ref)

    aff = jax.nn.sigmoid(_dot_nt(wr_ref[...], h_ref[...].astype(BF16)))
    sel = aff + bias_ref[...]

    giota = lax.broadcasted_iota(jnp.int32, (GROUP_SIZE, tm), 0)
    scores = []
    for g in range(N_GROUPS):
        slab = sel[g * GROUP_SIZE:(g + 1) * GROUP_SIZE, :]
        m1 = jnp.max(slab, axis=0, keepdims=True)
        first = jnp.min(jnp.where(slab == m1, giota, GROUP_SIZE), axis=0, keepdims=True)
        m2 = jnp.max(jnp.where(giota == first, neg, slab), axis=0, keepdims=True)
        scores.append(m1 + m2)
    gs = jnp.concatenate(scores, axis=0)

    grow = lax.broadcasted_iota(jnp.int32, (N_GROUPS, tm), 0)
    gsel = jnp.zeros((N_GROUPS, tm), jnp.bool_)
    work = gs
    for _ in range(TOPK_GROUPS):
        m = jnp.max(work, axis=0, keepdims=True)
        first = jnp.min(jnp.where(work == m, grow, N_GROUPS), axis=0, keepdims=True)
        hit = grow == first
        gsel = jnp.logical_or(gsel, hit)
        work = jnp.where(hit, neg, work)

    masked = jnp.concatenate(
        [jnp.where(gsel[g:g + 1, :], sel[g * GROUP_SIZE:(g + 1) * GROUP_SIZE, :], neg) for g in range(N_GROUPS)],
        axis=0)

    erow = lax.broadcasted_iota(jnp.int32, (N_EXPERTS, tm), 0)
    ids, ws = [], []
    chosen = jnp.zeros((N_EXPERTS, tm), jnp.bool_)
    for _ in range(TOP_K):
        m = jnp.max(masked, axis=0, keepdims=True)
        first = jnp.min(jnp.where(masked == m, erow, N_EXPERTS), axis=0, keepdims=True)
        hit = erow == first
        ids.append(first)
        ws.append(jnp.sum(jnp.where(hit, aff, 0.0), axis=0, keepdims=True))
        masked = jnp.where(hit, neg, masked)
        chosen = jnp.logical_or(chosen, hit)
    w = jnp.concatenate(ws, axis=0)
    idx_ref[...] = jnp.concatenate(ids, axis=0)
    gw_ref[...] = w / jnp.sum(w, axis=0, keepdims=True) * ROUTED_SCALE

    chosen_f = jnp.where(chosen, 1.0, 0.0)
    srow = lax.broadcasted_iota(jnp.int32, (tm, tm), 0)
    scol = lax.broadcasted_iota(jnp.int32, (tm, tm), 1)
    before = jnp.where(srow < scol, 1.0, 0.0).astype(BF16)
    prefix = _dot(chosen_f.astype(BF16), before) + carry_ref[...]
    ranks = [jnp.sum(jnp.where(erow == ids[k], prefix, 0.0), axis=0, keepdims=True) for k in range(TOP_K)]
    rank_ref[...] = jnp.concatenate(ranks, axis=0).astype(jnp.int32)
    carry_ref[...] = carry_ref[...] + jnp.sum(chosen_f, axis=1, keepdims=True)
    cnt_ref[...] = carry_ref[...]


def _route(h1, w_router_t, bias_col):
    t, d = h1.shape
    tm = min(ROUTE_TM, t)
    vmem = 2 * (tm * d * 4 + N_EXPERTS * d * 2 + 3 * TOP_K * tm * 4) + tm * d * 2 + 16 * N_EXPERTS * tm * 4 + tm * tm * 8
    return pl.pallas_call(
        _route_kernel,
        grid=(t // tm,),
        in_specs=[
            pl.BlockSpec((tm, d), lambda i: (i, 0)),
            pl.BlockSpec((N_EXPERTS, d), lambda i: (0, 0)),
            pl.BlockSpec((N_EXPERTS, 1), lambda i: (0, 0)),
        ],
        out_specs=[pl.BlockSpec((TOP_K, tm), lambda i: (0, i)), pl.BlockSpec((TOP_K, tm), lambda i: (0, i)),
                   pl.BlockSpec((TOP_K, tm), lambda i: (0, i)), pl.BlockSpec((N_EXPERTS, 1), lambda i: (0, 0))],
        out_shape=[jax.ShapeDtypeStruct((TOP_K, t), jnp.int32), jax.ShapeDtypeStruct((TOP_K, t), F32),
                   jax.ShapeDtypeStruct((TOP_K, t), jnp.int32), jax.ShapeDtypeStruct((N_EXPERTS, 1), F32)],
        scratch_shapes=[pltpu.VMEM((N_EXPERTS, 1), F32)],
        compiler_params=_params(("arbitrary",), vmem),
        name="router",
    )(h1, w_router_t, bias_col)


def _dest_kernel(idx_ref, rank_ref, ps_ref, o_ref):
    tm = idx_ref.shape[1]
    erow = lax.broadcasted_iota(jnp.int32, (N_EXPERTS, tm), 0)
    ps = ps_ref[...]
    rows = []
    for k in range(TOP_K):
        base = jnp.sum(jnp.where(erow == idx_ref[k:k + 1, :], ps, 0.0), axis=0, keepdims=True)
        rows.append(base.astype(jnp.int32) + rank_ref[k:k + 1, :])
    o_ref[...] = jnp.concatenate(rows, axis=0)


def _slot_dest(idx_t, rank_t, pstart_col):
    t = idx_t.shape[1]
    tm = min(ROUTE_TM, t)
    vmem = 2 * (3 * TOP_K * tm * 4) + 6 * N_EXPERTS * tm * 4
    return pl.pallas_call(
        _dest_kernel,
        grid=(t // tm,),
        in_specs=[pl.BlockSpec((TOP_K, tm), lambda i: (0, i)), pl.BlockSpec((TOP_K, tm), lambda i: (0, i)),
                  pl.BlockSpec((N_EXPERTS, 1), lambda i: (0, 0))],
        out_specs=pl.BlockSpec((TOP_K, tm), lambda i: (0, i)),
        out_shape=jax.ShapeDtypeStruct((TOP_K, t), jnp.int32),
        compiler_params=_params(("parallel",), vmem),
        name="slot_dest",
    )(idx_t, rank_t, pstart_col)


LANES = 128
U32 = jnp.uint32
HIGH_HALF = np.uint32(0xFFFF0000)
N_DMA_PRIORITIES = 2


def _tile_rows(d):
    return d // (2 * LANES)


def _pack_tiles(tile_ref, value, base=0):
    n, d = value.shape
    sub = _tile_rows(d)
    bits = lax.bitcast_convert_type(value.astype(BF16).astype(F32), U32)
    for c in range(sub):
        low = bits[:, c * LANES:(c + 1) * LANES] >> 16
        high = bits[:, (c + sub) * LANES:(c + sub + 1) * LANES] & HIGH_HALF
        tile_ref[pl.ds(base + c, n, stride=sub), :] = low | high


def _unpack_tiles(tile_ref, n, d, base=0):
    sub = _tile_rows(d)
    words = [tile_ref[pl.ds(base + c, n, stride=sub), :] for c in range(sub)]
    lows = [lax.bitcast_convert_type(u << 16, F32) for u in words]
    highs = [lax.bitcast_convert_type(u & HIGH_HALF, F32) for u in words]
    return jnp.concatenate(lows + highs, axis=1)


def _dispatch_kernel(pend_ref, padded_ref, nu_ref, dest_ref, h_ref, xs_ref, tile_ref, zero_ref, sem, zsem):
    tm, d = h_ref.shape
    sub = _tile_rows(d)
    zrows = zero_ref.shape[0]
    n_blocks = xs_ref.shape[0] // zrows

    @pl.when(pl.program_id(0) == 0)
    def _():
        zero_ref[...] = jnp.zeros_like(zero_ref)

        def zero_block(first_row):
            return pltpu.make_async_copy(zero_ref, xs_ref.at[pl.ds(pl.multiple_of(first_row, 8), zrows), :], zsem)

        def issue(e, carry):
            @pl.when(padded_ref[e] > 0)
            def _():
                zero_block(pend_ref[e] * sub - zrows).start()
            return carry

        def drain(e, carry):
            @pl.when(padded_ref[e] > 0)
            def _():
                zero_block(pend_ref[e] * sub - zrows).wait()
            return carry

        def issue_tail(blk, carry):
            zero_block(blk * zrows).start()
            return carry

        def drain_tail(blk, carry):
            zero_block(blk * zrows).wait()
            return carry

        lax.fori_loop(0, N_EXPERTS, issue, 0)
        lax.fori_loop(nu_ref[0], n_blocks, issue_tail, 0)
        lax.fori_loop(0, N_EXPERTS, drain, 0)
        lax.fori_loop(nu_ref[0], n_blocks, drain_tail, 0)

    i = pl.program_id(0)
    slot = lax.rem(i, 2)
    _pack_tiles(tile_ref.at[slot], h_ref[...])

    def row_copy(tok, dst_row):
        return pltpu.make_async_copy(tile_ref.at[slot, pl.ds(pl.multiple_of(tok * sub, sub), sub), :],
                                     xs_ref.at[pl.ds(pl.multiple_of(dst_row * sub, sub), sub), :], sem.at[slot])

    def issue(tok, carry):
        for k in range(TOP_K):
            row_copy(tok, dest_ref[tok * TOP_K + k]).start(priority=k % N_DMA_PRIORITIES)
        return carry

    lax.fori_loop(0, tm, issue, 0)

    def wait_tile(s):
        for _ in range(TOP_K):
            pltpu.make_async_copy(tile_ref.at[s], xs_ref.at[pl.ds(0, tm * sub), :], sem.at[s]).wait()

    @pl.when(i > 0)
    def _():
        wait_tile(1 - slot)

    @pl.when(i == pl.num_programs(0) - 1)
    def _():
        wait_tile(slot)


def _dispatch(h1, dest_flat, pend, padded, n_used, n_rows_pad):
    t, d = h1.shape
    sub = _tile_rows(d)
    tm = min(DISPATCH_TM, t)
    vmem = 2 * (tm * d * 4) + tm * d * 2 + MOE_BM * d * 2 + 4 * tm * d * 4
    return pl.pallas_call(
        _dispatch_kernel,
        grid_spec=pltpu.PrefetchScalarGridSpec(
            num_scalar_prefetch=3,
            grid=(t // tm,),
            in_specs=[
                pl.BlockSpec((tm * TOP_K,), lambda i, pe, pa, nu: (i,), memory_space=pltpu.SMEM),
                pl.BlockSpec((tm, d), lambda i, pe, pa, nu: (i, 0)),
            ],
            out_specs=pl.BlockSpec(memory_space=pl.ANY),
            scratch_shapes=[pltpu.VMEM((2, tm * sub, LANES), U32), pltpu.VMEM((MOE_BM * sub, LANES), U32),
                            pltpu.SemaphoreType.DMA((2,)), pltpu.SemaphoreType.DMA(())],
        ),
        out_shape=jax.ShapeDtypeStruct((n_rows_pad * sub, LANES), U32),
        compiler_params=_params(("arbitrary",), vmem),
        name="dispatch_rows",
    )(pend, padded, n_used, dest_flat, h1)


def _expert_kernel(ps_ref, pd_ref, nu_ref, wg_ref, wu_ref, wd_ref, x_hbm, y_hbm,
                   xbuf, ybuf, wgb_ref, wub_ref, wdb_ref, xsem, ysem):
    e = pl.program_id(0)
    d = wg_ref.shape[1]
    slots = xbuf.shape[0]
    group = EXPERT_GROUP
    ahead = slots - group
    rows = xbuf.shape[1]
    bm = rows // _tile_rows(d)
    n_blocks = y_hbm.shape[0] // rows
    n_used = nu_ref[0]
    first = ps_ref[e] // bm
    nb = pd_ref[e] // bm

    def x_copy(g, slot):
        return pltpu.make_async_copy(x_hbm.at[pl.ds(pl.multiple_of(g * rows, rows), rows), :], xbuf.at[slot],
                                     xsem.at[slot])

    def y_copy(g, slot):
        return pltpu.make_async_copy(ybuf.at[slot], y_hbm.at[pl.ds(pl.multiple_of(g * rows, rows), rows), :],
                                     ysem.at[slot])

    @pl.when(e == 0)
    def _():
        for g0 in range(ahead):
            @pl.when(g0 < n_used)
            def _():
                x_copy(g0, g0).start()

    @pl.when(nb > 0)
    def _():
        wgb_ref[...] = wg_ref[0].astype(BF16)
        wub_ref[...] = wu_ref[0].astype(BF16)
        wdb_ref[...] = wd_ref[0].astype(BF16)

        def acquire(g):
            slot = lax.rem(g, slots)
            x_copy(g, slot).wait()

            @pl.when(g + ahead < n_used)
            def _():
                x_copy(g + ahead, lax.rem(g + ahead, slots)).start()

            @pl.when(g >= slots)
            def _():
                y_copy(g - slots, slot).wait()

            return slot

        def compute(block_slots):
            xs = [_unpack_tiles(xbuf.at[s], bm, d).astype(BF16) for s in block_slots]
            gates = [_dot(xb, wgb_ref[...]) for xb in xs]
            ups = [_dot(xb, wub_ref[...]) for xb in xs]
            hs = [(_silu(hg) * hu).astype(BF16) for hg, hu in zip(gates, ups)]
            ys = [_dot(hb, wdb_ref[...]) for hb in hs]
            for s, y in zip(block_slots, ys):
                _pack_tiles(ybuf.at[s], y)

        def run_group(g, n):
            block_slots = [acquire(g + u) for u in range(n)]
            compute(block_slots)
            for u, s in enumerate(block_slots):
                y_copy(g + u, s).start()

        def full_group(jj, carry):
            run_group(first + group * jj, group)
            return carry

        lax.fori_loop(0, nb // group, full_group, 0)
        done = nb - lax.rem(nb, group)
        size = group // 2
        while size >= 1:
            @pl.when(lax.rem(nb, 2 * size) >= size)
            def _(size=size, done=done):
                run_group(first + done, size)

            done = done + jnp.where(lax.rem(nb, 2 * size) >= size, size, 0)
            size //= 2

    @pl.when(e == pl.num_programs(0) - 1)
    def _():
        def drain_used(g, carry):
            y_copy(g, lax.rem(g, slots)).wait()
            return carry

        lax.fori_loop(jnp.maximum(n_used - slots, 0), n_used, drain_used, 0)
        ybuf[0] = jnp.zeros_like(ybuf[0])

        def issue(g, carry):
            y_copy(g, 0).start()
            return carry

        def drain(g, carry):
            y_copy(g, 0).wait()
            return carry

        lax.fori_loop(n_used, n_blocks, issue, 0)
        lax.fori_loop(n_used, n_blocks, drain, 0)


def _experts(x_sorted, pstart, padded, n_used, w_gate, w_up, w_down):
    n_exp, d, ff = w_gate.shape
    sub = _tile_rows(d)
    bm = MOE_BM
    vmem = 2 * (3 * d * ff * 4) + 3 * d * ff * 2 + 2 * EXPERT_SLOTS * bm * d * 2 + 6 * bm * ff * 4 + 6 * bm * d * 4
    return pl.pallas_call(
        _expert_kernel,
        grid_spec=pltpu.PrefetchScalarGridSpec(
            num_scalar_prefetch=3,
            grid=(n_exp,),
            in_specs=[
                pl.BlockSpec((1, d, ff), lambda e, ps, pd, nu: (e, 0, 0)),
                pl.BlockSpec((1, d, ff), lambda e, ps, pd, nu: (e, 0, 0)),
                pl.BlockSpec((1, ff, d), lambda e, ps, pd, nu: (e, 0, 0)),
                pl.BlockSpec(memory_space=pl.ANY),
            ],
            out_specs=pl.BlockSpec(memory_space=pl.ANY),
            scratch_shapes=[pltpu.VMEM((EXPERT_SLOTS, bm * sub, LANES), U32),
                            pltpu.VMEM((EXPERT_SLOTS, bm * sub, LANES), U32),
                            pltpu.VMEM((d, ff), BF16), pltpu.VMEM((d, ff), BF16), pltpu.VMEM((ff, d), BF16),
                            pltpu.SemaphoreType.DMA((EXPERT_SLOTS,)), pltpu.SemaphoreType.DMA((EXPERT_SLOTS,))],
        ),
        out_shape=jax.ShapeDtypeStruct(x_sorted.shape, U32),
        compiler_params=_params(("arbitrary",), vmem),
        name="expert_mlp",
    )(pstart, padded, n_used, w_gate, w_up, w_down, x_sorted)


def _final_kernel(dest_ref, dest_next_ref, h_ref, gw_ref, y_ref, wg_ref, wu_ref, wd_ref, g_ref, b_ref, o_ref,
                  rows_a, rows_b, shared_ref, sem):
    i = pl.program_id(0)
    tm, d = h_ref.shape
    sub = _tile_rows(d)
    grp = COMBINE_GROUP

    def issue_tokens(dref, buf, s, tok0):
        for u in range(grp):
            for k in range(TOP_K):
                src_row = dref[(tok0 + u) * TOP_K + k]
                pltpu.make_async_copy(
                    y_ref.at[pl.ds(pl.multiple_of(src_row * sub, sub), sub), :],
                    buf.at[pl.ds(pl.multiple_of((k * tm + tok0 + u) * sub, sub), sub), :],
                    sem.at[s]).start(priority=k % N_DMA_PRIORITIES)

    def wait_tile(buf, s):
        pltpu.make_async_copy(y_ref.at[pl.ds(0, TOP_K * tm * sub), :], buf, sem.at[s]).wait()

    @pl.when(i == 0)
    def _():
        def first(j, carry):
            issue_tokens(dest_ref, rows_a, 0, j * grp)
            return carry

        lax.fori_loop(0, tm // grp, first, 0)

    hb = h_ref[...].astype(BF16)
    shared_ref[...] = _dot((_silu(_dot(hb, wg_ref[...])) * _dot(hb, wu_ref[...])).astype(BF16), wd_ref[...])
    ln_g = g_ref[...]
    ln_b = b_ref[...]

    def reduce_tile(cur, cur_s, nxt, nxt_s):
        wait_tile(cur, cur_s)

        def reduce_group(j, carry):
            tok0 = j * grp if isinstance(j, int) else pl.multiple_of(j * grp, grp)
            issue_tokens(dest_next_ref, nxt, nxt_s, tok0)
            tok = pl.ds(tok0, grp)
            gw = gw_ref[tok, :]
            routed = gw[:, 0:1] * _unpack_tiles(cur, grp, d, base=tok0 * sub)
            for k in range(1, TOP_K):
                routed = routed + gw[:, k:k + 1] * _unpack_tiles(cur, grp, d, base=(k * tm + tok0) * sub)
            o_ref[tok, :] = _layer_norm(DEEPNORM_ALPHA * h_ref[tok, :] + (routed + shared_ref[tok, :]), ln_g, ln_b)
            return carry

        for j in range(tm // grp):
            reduce_group(j, 0)

        @pl.when(i == pl.num_programs(0) - 1)
        def _():
            wait_tile(nxt, nxt_s)

    @pl.when(lax.rem(i, 2) == 0)
    def _():
        reduce_tile(rows_a, 0, rows_b, 1)

    @pl.when(lax.rem(i, 2) == 1)
    def _():
        reduce_tile(rows_b, 1, rows_a, 0)


def _final(h1, dest_flat, gw, y_sorted, w_sg, w_su, w_sd, g2, b2):
    t, d = h1.shape
    ff = w_sg.shape[1]
    sub = _tile_rows(d)
    tm = min(FINAL_TM, t)
    nt = t // tm
    vmem = 2 * (2 * tm * d * 4 + tm * LANES * 4 + 3 * d * ff * 2) + 2 * TOP_K * tm * d * 2 + 8 * tm * d * 4

    def full(shape):
        return pl.BlockSpec(shape, lambda i: tuple(0 for _ in shape))

    return pl.pallas_call(
        _final_kernel,
        grid=(nt,),
        in_specs=[
            pl.BlockSpec((tm * TOP_K,), lambda i: (i,), memory_space=pltpu.SMEM),
            pl.BlockSpec((tm * TOP_K,), lambda i: (jnp.minimum(i + 1, nt - 1),), memory_space=pltpu.SMEM),
            pl.BlockSpec((tm, d), lambda i: (i, 0)),
            pl.BlockSpec((tm, TOP_K), lambda i: (i, 0)),
            pl.BlockSpec(memory_space=pl.ANY),
            full((d, ff)), full((d, ff)), full((ff, d)), full((1, d)), full((1, d)),
        ],
        out_specs=pl.BlockSpec((tm, d), lambda i: (i, 0)),
        out_shape=jax.ShapeDtypeStruct((t, d), F32),
        scratch_shapes=[pltpu.VMEM((TOP_K * tm * sub, LANES), U32), pltpu.VMEM((TOP_K * tm * sub, LANES), U32),
                        pltpu.VMEM((tm, d), F32), pltpu.SemaphoreType.DMA((2,))],
        compiler_params=_params(("arbitrary",), vmem),
        name="combine_final",
    )(dest_flat, dest_flat, h1, gw, y_sorted, w_sg, w_su, w_sd, g2, b2)


def _lower_bound(p):
    return jnp.cumsum(jax.nn.softmax(p.astype(F32), axis=0), axis=0)[0:1]


def _block_plan(counts, n_rows):
    bm = MOE_BM
    counts = counts.reshape(-1).astype(jnp.int32)
    padded = (counts + bm - 1) // bm * bm
    pend = jnp.cumsum(padded).astype(jnp.int32)
    pstart = pend - padded
    n_blocks = n_rows // bm + N_EXPERTS
    n_used = pend[-1:] // bm
    return pstart, pend, padded, n_used.astype(jnp.int32), n_blocks


def kernel(x, mem, ln_in_g, ln_in_b, ln_mem_g, ln_mem_b, hgrn_lb_fwd, hgrn_lb_bwd, w_in, b_gate, hgrn_norm_g, w_mem_kv, w_fourier_o, b_fourier_o, w_hgrn_o, w_xattn_o, w_out, ln1_g, ln1_b, w_router, router_bias, w_exp_gate, w_exp_up, w_exp_down, w_sh_gate, w_sh_up, w_sh_down, ln2_g, ln2_b):
    bsz, s, d = x.shape
    t = bsz * s
    l = 0
    row = lambda v: v.reshape(1, -1).astype(F32)

    n_gate = N_BRANCHES * d
    n_rest = w_in.shape[2] - n_gate
    w_in_r = jnp.concatenate([w_in[l][:, n_rest:], w_in[l][:, :n_rest]], axis=1).astype(BF16)
    gate_blocks = n_gate // INPROJ_TN
    cb_fourier = gate_blocks
    cb_hq, cb_hi, cb_zf, cb_zb, cb_hg, cb_xq = (gate_blocks + 1 + n for n in range(6))

    x2 = x.reshape(t, d)
    proj = _inproj(x2, row(ln_in_g), row(ln_in_b), w_in_r)
    kv = _mem_kv(mem, row(ln_mem_g), row(ln_mem_b), w_mem_kv[l].astype(BF16))

    fm = _fourier(proj, bsz, s, cb_fourier * (INPROJ_TN // FOURIER_GROUP_DIM))

    lb_f = _lower_bound(hgrn_lb_fwd)
    lb_b = _lower_bound(hgrn_lb_bwd)
    o_fwd = _hgrn_sweep(proj, lb_f, (cb_hq, cb_hi, cb_zf), bsz, s, reverse=False)
    ho = _hgrn_sweep(proj, lb_b, (cb_hq, cb_hi, cb_zb, cb_hg), bsz, s, reverse=True,
                     o_fwd=o_fwd, norm_g=row(hgrn_norm_g[l]))

    h1 = _merge(x2, row(ln_in_g), row(ln_in_b), proj, 0, cb_xq, fm, ho, kv,
                w_xattn_o[l].astype(BF16), w_fourier_o[l].astype(BF16), row(b_fourier_o[l]),
                w_hgrn_o[l].astype(BF16), row(b_gate[l]), w_out[l].astype(BF16), row(ln1_g[l]), row(ln1_b[l]), s)

    idx_t, gw_t, rank_t, counts = _route(h1, w_router[l].T.astype(BF16),
                                         router_bias[l].reshape(N_EXPERTS, 1).astype(F32))
    pstart, pend, padded, n_used, n_blocks = _block_plan(counts, t * TOP_K)
    dest_t = _slot_dest(idx_t, rank_t, pstart.astype(F32).reshape(N_EXPERTS, 1))
    dest_flat = dest_t.T.reshape(-1)

    x_sorted = _dispatch(h1, dest_flat, pend, padded, n_used, n_blocks * MOE_BM)
    y_sorted = _experts(x_sorted, pstart, padded, n_used, w_exp_gate[l], w_exp_up[l], w_exp_down[l])
    out = _final(h1, dest_flat, gw_t.T, y_sorted, w_sh_gate[l].astype(BF16), w_sh_up[l].astype(BF16),
                 w_sh_down[l].astype(BF16), row(ln2_g[l]), row(ln2_b[l]))
    return out.reshape(bsz, s, d)
```

```python
import functools
import math

import numpy as np
import jax
import jax.numpy as jnp
from jax import lax
from jax.experimental import pallas as pl
from jax.experimental.pallas import tpu as pltpu

F32 = jnp.float32
BF16 = jnp.bfloat16

LN_EPS = 1e-5
DEPTH = 1
DEEPNORM_ALPHA = (2 * DEPTH) ** 0.25
FOURIER_GROUP_DIM = 128
FOURIER_W = 512
HGRN_HEADS = 4
HGRN_HEAD_DIM = 128
HGRN_W = HGRN_HEADS * HGRN_HEAD_DIM
HGRN_CHUNK = 64
XATTN_HEADS = 4
XATTN_HEAD_DIM = 128
XATTN_W = XATTN_HEADS * XATTN_HEAD_DIM
N_BRANCHES = 3
N_EXPERTS = 256
TOP_K = 8
N_GROUPS = 8
TOPK_GROUPS = 4
GROUP_SIZE = N_EXPERTS // N_GROUPS
ROUTED_SCALE = 2.5

V7X_SCOPED_VMEM_CAP_BYTES = 60000 * 1024

INPROJ_TM = 2048
INPROJ_TN = 512
FOURIER_P = 128
FOURIER_UNROLL = 16
HGRN_ROWS = 512
MERGE_TM = 512
ROUTE_TM = 512
MOE_BM = 256
EXPERT_SLOTS = 8
EXPERT_GROUP = 2
DISPATCH_TM = 256
FINAL_TM = 256
COMBINE_GROUP = 32


def _params(semantics, vmem_bytes):
    return pltpu.CompilerParams(dimension_semantics=semantics,
                                vmem_limit_bytes=int(min(vmem_bytes, V7X_SCOPED_VMEM_CAP_BYTES)))


def _layer_norm(x, g, b):
    mu = jnp.mean(x, axis=-1, keepdims=True)
    xc = x - mu
    var = jnp.mean(xc * xc, axis=-1, keepdims=True)
    return xc * lax.rsqrt(var + LN_EPS) * g + b


def _silu(x):
    return x * jax.nn.sigmoid(x)


def _dot(a, b):
    return jnp.dot(a, b, preferred_element_type=F32)


def _dot_nt(a, b):
    return lax.dot_general(a, b, (((1,), (1,)), ((), ())), preferred_element_type=F32)


def _dot_tn(a, b):
    return lax.dot_general(a, b, (((0,), (0,)), ((), ())), preferred_element_type=F32)


def _inproj_kernel(x_ref, g_ref, b_ref, w_ref, o_ref, hn_ref):
    @pl.when(pl.program_id(1) == 0)
    def _():
        hn_ref[...] = _layer_norm(x_ref[...], g_ref[...], b_ref[...]).astype(BF16)

    o_ref[...] = _dot(hn_ref[...], w_ref[...])


def _inproj(x2, g, b, w):
    t, d = x2.shape
    n = w.shape[1]
    tm, tn = min(INPROJ_TM, t), INPROJ_TN
    vmem = 2 * (tm * d * 4 + d * tn * 2 + tm * tn * 4) + tm * d * 2 + 3 * tm * d * 4
    return pl.pallas_call(
        _inproj_kernel,
        grid=(t // tm, n // tn),
        in_specs=[
            pl.BlockSpec((tm, d), lambda i, j: (i, 0)),
            pl.BlockSpec((1, d), lambda i, j: (0, 0)),
            pl.BlockSpec((1, d), lambda i, j: (0, 0)),
            pl.BlockSpec((d, tn), lambda i, j: (0, j)),
        ],
        out_specs=pl.BlockSpec((tm, tn), lambda i, j: (i, j)),
        out_shape=jax.ShapeDtypeStruct((t, n), F32),
        scratch_shapes=[pltpu.VMEM((tm, d), BF16)],
        compiler_params=_params(("parallel", "arbitrary"), vmem),
        name="ln_inproj",
    )(x2, g, b, w)


def _mem_kv_kernel(m_ref, g_ref, b_ref, w_ref, o_ref):
    mn = _layer_norm(m_ref[0], g_ref[...], b_ref[...]).astype(BF16)
    o_ref[0] = _dot(mn, w_ref[...]).astype(BF16)


def _mem_kv(mem, g, b, w):
    bsz, m, d = mem.shape
    n = w.shape[1]
    vmem = 2 * (m * d * 4 + d * n * 2 + m * n * 2) + 4 * m * d * 4
    return pl.pallas_call(
        _mem_kv_kernel,
        grid=(bsz,),
        in_specs=[
            pl.BlockSpec((1, m, d), lambda i: (i, 0, 0)),
            pl.BlockSpec((1, d), lambda i: (0, 0)),
            pl.BlockSpec((1, d), lambda i: (0, 0)),
            pl.BlockSpec((d, n), lambda i: (0, 0)),
        ],
        out_specs=pl.BlockSpec((1, m, n), lambda i: (i, 0, 0)),
        out_shape=jax.ShapeDtypeStruct((bsz, m, n), BF16),
        compiler_params=_params(("parallel",), vmem),
        name="mem_kv",
    )(mem, g, b, w)


def _fourier_tables(s, c, p):
    q = s // p
    ki = np.arange(p, dtype=np.int64)
    i = np.arange(p, dtype=np.int64)
    j = np.arange(q, dtype=np.int64)
    ph = (ki[None, :, None] * (q * i[None, None, :] + j[:, None, None])) % s
    ang = 2.0 * np.pi * ph.astype(np.float64) / s
    m1 = np.concatenate([np.cos(ang), -np.sin(ang)], axis=1)
    kj = np.arange(q, dtype=np.int64)
    a2 = 2.0 * np.pi * ((kj[:, None] * j[None, :]) % q).astype(np.float64) / q
    m2 = np.block([[np.cos(a2), np.sin(a2)], [-np.sin(a2), np.cos(a2)]])
    cc = np.arange(c, dtype=np.int64)
    a3 = 2.0 * np.pi * ((cc[:, None] * cc[None, :]) % c).astype(np.float64) / c
    mc = np.concatenate([np.cos(a3), np.sin(a3)], axis=0) / math.sqrt(s * c)
    return (jnp.asarray(m1, dtype=BF16), jnp.asarray(m2, dtype=BF16), jnp.asarray(mc, dtype=BF16))


SUBLANES = 8


def _fourier_pitch(q):
    return 2 * q + SUBLANES if (2 * q // SUBLANES) % 2 == 0 else 2 * q


def _fourier_kernel(x_ref, m1_ref, m2_ref, mc_ref, o_ref, ys_ref, *, p, q):
    grp = FOURIER_UNROLL
    pitch = _fourier_pitch(q)

    def stage1(jg, carry):
        js = [jg * grp + u for u in range(grp)]
        xs = [x_ref[pl.ds(j, p, stride=q), :].astype(BF16) for j in js]
        ys = [_dot(m1_ref[j], xj) for j, xj in zip(js, xs)]
        for j, y in zip(js, ys):
            ys_ref[pl.ds(j, p, stride=pitch), :] = y[:p]
            ys_ref[pl.ds(q + j, p, stride=pitch), :] = y[p:]
        return carry

    lax.fori_loop(0, q // grp, stage1, 0)

    def stage2(kg, carry):
        kis = [kg * grp + u for u in range(grp)]
        ys = [ys_ref[pl.ds(pl.multiple_of(ki * pitch, SUBLANES), 2 * q), :].astype(BF16) for ki in kis]
        zs = [_dot(m2_ref[...], y) for y in ys]
        zcs = [jnp.concatenate([z[:q], z[q:]], axis=1).astype(BF16) for z in zs]
        outs = [_dot(zc, mc_ref[...]) for zc in zcs]
        for ki, o in zip(kis, outs):
            o_ref[pl.ds(ki, q, stride=p), :] = o
        return carry

    lax.fori_loop(0, p // grp, stage2, 0)


def _fourier(proj, bsz, s, col_block0):
    c = FOURIER_GROUP_DIM
    groups = FOURIER_W // c
    p = min(FOURIER_P, s // 8)
    q = s // p
    m1, m2, mc = _fourier_tables(s, c, p)
    vmem = 2 * (s * c * 4 + m1.size * 2 + m2.size * 2 + mc.size * 2 + s * c * 4) + 2 * s * c * 4 + 8 * p * c * 4
    return pl.pallas_call(
        functools.partial(_fourier_kernel, p=p, q=q),
        grid=(bsz, groups),
        in_specs=[
            pl.BlockSpec((s, c), lambda b, g: (b, col_block0 + g)),
            pl.BlockSpec((q, 2 * p, p), lambda b, g: (0, 0, 0)),
            pl.BlockSpec((2 * q, 2 * q), lambda b, g: (0, 0)),
            pl.BlockSpec((2 * c, c), lambda b, g: (0, 0)),
        ],
        out_specs=pl.BlockSpec((s, c), lambda b, g: (b, g)),
        out_shape=jax.ShapeDtypeStruct((bsz * s, FOURIER_W), F32),
        scratch_shapes=[pltpu.VMEM((p * _fourier_pitch(q), c), F32)],
        compiler_params=_params(("parallel", "parallel"), vmem),
        name="fourier_mix",
    )(proj, m1, m2, mc)


def _cumsum_dot(ones_mask, x):
    n = x.shape[1]
    hi = x.astype(BF16)
    r1 = x - hi.astype(F32)
    mid = r1.astype(BF16)
    lo = (r1 - mid.astype(F32)).astype(BF16)
    parts = _dot(ones_mask, jnp.concatenate([hi, mid, lo], axis=1))
    return parts[:, 0:n] + parts[:, n:2 * n] + parts[:, 2 * n:3 * n]


def _hgrn_kernel(*refs, reverse, finalize, n_chunks):
    if finalize:
        q_ref, i_ref, z_ref, lb_ref, of_ref, g_ref, ng_ref, o_ref, st_ref = refs
    else:
        q_ref, i_ref, z_ref, lb_ref, o_ref, st_ref = refs

    @pl.when(pl.program_id(1) == 0)
    def _():
        st_ref[...] = jnp.zeros_like(st_ref)

    ck = HGRN_CHUNK
    hd = HGRN_HEAD_DIM
    row = lax.broadcasted_iota(jnp.int32, (ck, ck), 0)
    col = lax.broadcasted_iota(jnp.int32, (ck, ck), 1)
    tri = (col >= row) if reverse else (col <= row)
    cum = jnp.where(tri, 1.0, 0.0).astype(BF16)
    lb = lb_ref[...]

    order = list(range(n_chunks - 1, -1, -1) if reverse else range(n_chunks))
    heads = [slice(h * hd, (h + 1) * hd) for h in range(HGRN_HEADS)]
    rows = {c: pl.ds(c * ck, ck) for c in order}

    q_dec, k_inv, k_end, decay, vb = {}, {}, {}, {}, {}
    for c in order:
        f = lb + (1.0 - lb) * jax.nn.sigmoid(z_ref[rows[c], :])
        k = 1.0 - f
        a = _cumsum_dot(cum, jnp.log(f))
        a_end = a[0:1, :] if reverse else a[ck - 1:ck, :]
        q_dec[c] = (_silu(q_ref[rows[c], :]) * jnp.exp(a)).astype(BF16)
        k_inv[c] = (k * jnp.exp(-a)).astype(BF16)
        k_end[c] = (k * jnp.exp(a_end - a)).astype(BF16)
        decay[c] = jnp.exp(a_end)
        vb[c] = i_ref[rows[c], :].astype(BF16)

    scores = {(c, h): jnp.where(tri, _dot_nt(q_dec[c][:, sl], k_inv[c][:, sl]), 0.0).astype(BF16)
              for c in order for h, sl in enumerate(heads)}
    update = {(c, h): _dot_tn(vb[c][:, sl], k_end[c][:, sl]) for c in order for h, sl in enumerate(heads)}
    o_intra = {(c, h): _dot(scores[c, h], vb[c][:, sl]) for c in order for h, sl in enumerate(heads)}

    state_in = {}
    for h, sl in enumerate(heads):
        st = st_ref[h]
        for c in order:
            state_in[c, h] = st.astype(BF16)
            st = st * decay[c][:, sl] + update[c, h]
        st_ref[h] = st

    o_inter = {(c, h): _dot_nt(q_dec[c][:, sl], state_in[c, h]) for c in order for h, sl in enumerate(heads)}

    for c in order:
        outs = []
        for h, sl in enumerate(heads):
            o_h = o_intra[c, h] + o_inter[c, h]
            if finalize:
                o_h = o_h + of_ref[rows[c], sl]
                ms = jnp.mean(o_h * o_h, axis=-1, keepdims=True)
                o_h = o_h * lax.rsqrt(ms + LN_EPS) * ng_ref[:, sl]
            outs.append(o_h)
        o = jnp.concatenate(outs, axis=1)
        if finalize:
            o_ref[rows[c], :] = (o * _silu(g_ref[rows[c], :])).astype(o_ref.dtype)
        else:
            o_ref[rows[c], :] = o


def _hgrn_sweep(proj, lb, cols, bsz, s, *, reverse, o_fwd=None, norm_g=None):
    w = HGRN_W
    rb = min(HGRN_ROWS, s)
    nb = s // rb
    finalize = o_fwd is not None

    def rmap(b, n):
        return b * nb + (nb - 1 - n if reverse else n)

    def cmap(cb):
        return lambda b, n: (rmap(b, n), cb)

    in_specs = [pl.BlockSpec((rb, w), cmap(cols[0])), pl.BlockSpec((rb, w), cmap(cols[1])),
                pl.BlockSpec((rb, w), cmap(cols[2])), pl.BlockSpec((1, w), lambda b, n: (0, 0))]
    args = [proj, proj, proj, lb]
    if finalize:
        in_specs += [pl.BlockSpec((rb, w), cmap(0)), pl.BlockSpec((rb, w), cmap(cols[3])),
                     pl.BlockSpec((1, w), lambda b, n: (0, 0))]
        args += [o_fwd, proj, norm_g]
    vmem = 2 * (6 * rb * w * 4) + HGRN_HEADS * HGRN_HEAD_DIM * HGRN_HEAD_DIM * 4 + 24 * HGRN_CHUNK * w * 4
    return pl.pallas_call(
        functools.partial(_hgrn_kernel, reverse=reverse, finalize=finalize, n_chunks=rb // HGRN_CHUNK),
        grid=(bsz, nb),
        in_specs=in_specs,
        out_specs=pl.BlockSpec((rb, w), cmap(0)),
        out_shape=jax.ShapeDtypeStruct((bsz * s, w), BF16 if finalize else F32),
        scratch_shapes=[pltpu.VMEM((HGRN_HEADS, HGRN_HEAD_DIM, HGRN_HEAD_DIM), F32)],
        compiler_params=_params(("parallel", "arbitrary"), vmem),
        name="hgrn_bwd" if reverse else "hgrn_fwd",
    )(*args)


def _merge_kernel(x_ref, lg_ref, lbi_ref, gl_ref, xq_ref, fm_ref, ho_ref, kv_ref, wx_ref, wf_ref, bf_ref,
                  wh_ref, bg_ref, wo_ref, g1_ref, b1_ref, o_ref):
    d = x_ref.shape[1]
    h = _layer_norm(x_ref[...], lg_ref[...], lbi_ref[...])

    kv = kv_ref[0]
    xq = xq_ref[...].astype(BF16)
    key_sl = [slice(hh * XATTN_HEAD_DIM, (hh + 1) * XATTN_HEAD_DIM) for hh in range(XATTN_HEADS)]
    val_sl = [slice(XATTN_W + hh * XATTN_HEAD_DIM, XATTN_W + (hh + 1) * XATTN_HEAD_DIM) for hh in range(XATTN_HEADS)]
    logits = [_dot_nt(xq[:, sl], kv[:, sl]) for sl in key_sl]
    y_f = _dot(fm_ref[...].astype(BF16), wf_ref[...]) + bf_ref[...]
    y_h = _dot(ho_ref[...], wh_ref[...])
    probs = []
    for sc in logits:
        sc = sc * (XATTN_HEAD_DIM ** -0.5)
        sc = sc - jnp.max(sc, axis=-1, keepdims=True)
        e = jnp.exp(sc)
        probs.append((e / jnp.sum(e, axis=-1, keepdims=True)).astype(BF16))
    att = jnp.concatenate([_dot(pr, kv[:, sl]) for pr, sl in zip(probs, val_sl)], axis=1).astype(BF16)

    y_x = _dot(att, wx_ref[...])
    gate = jax.nn.sigmoid(gl_ref[...] + bg_ref[...])
    merged = gate[:, 0:d] * y_f + gate[:, d:2 * d] * y_h + gate[:, 2 * d:3 * d] * y_x
    y = _dot(merged.astype(BF16), wo_ref[...])
    o_ref[...] = _layer_norm(DEEPNORM_ALPHA * h + y, g1_ref[...], b1_ref[...])


def _merge(x2, ln_g, ln_b, proj, gate_cb, xq_cb, fm, ho, kv, w_xo, w_fo, b_fo, w_ho, b_gate, w_out, g1, b1, s):
    t, d = x2.shape
    tm = min(MERGE_TM, s)
    per_b = s // tm
    m, kvw = kv.shape[1], kv.shape[2]
    gw = N_BRANCHES * d

    def full(shape):
        return pl.BlockSpec(shape, lambda i: tuple(0 for _ in shape))

    vmem = (2 * (tm * d * 4 + tm * gw * 4 + tm * XATTN_W * 4 + tm * FOURIER_W * 4 + tm * HGRN_W * 2 + m * kvw * 2
                 + 3 * XATTN_W * d * 2 + d * d * 2 + tm * d * 4) + 10 * tm * d * 4 + 2 * tm * gw * 4)
    return pl.pallas_call(
        _merge_kernel,
        grid=(t // tm,),
        in_specs=[
            pl.BlockSpec((tm, d), lambda i: (i, 0)),
            full((1, d)), full((1, d)),
            pl.BlockSpec((tm, gw), lambda i: (i, gate_cb)),
            pl.BlockSpec((tm, XATTN_W), lambda i: (i, xq_cb)),
            pl.BlockSpec((tm, FOURIER_W), lambda i: (i, 0)),
            pl.BlockSpec((tm, HGRN_W), lambda i: (i, 0)),
            pl.BlockSpec((1, m, kvw), lambda i: (i // per_b, 0, 0)),
            full((XATTN_W, d)), full((FOURIER_W, d)), full((1, d)), full((HGRN_W, d)), full((1, gw)),
            full((d, d)), full((1, d)), full((1, d)),
        ],
        out_specs=pl.BlockSpec((tm, d), lambda i: (i, 0)),
        out_shape=jax.ShapeDtypeStruct((t, d), F32),
        compiler_params=_params(("parallel",), vmem),
        name="merge_out",
    )(x2, ln_g, ln_b, proj, proj, fm, ho, kv, w_xo, w_fo, b_fo, w_ho, b_gate, w_out, g1, b1)


def _route_kernel(h_ref, wr_ref, bias_ref, idx_ref, gw_ref, rank_ref, cnt_ref, carry_ref):
    tm = h_ref.shape[0]
    neg = -jnp.inf

    @pl.when(pl.program_id(0) == 0)
    def _():
        carry_ref[...] = jnp.zeros_like(carry_ref)

    aff = jax.nn.sigmoid(_dot_nt(wr_ref[...], h_ref[...].astype(BF16)))
    sel = aff + bias_ref[...]

    giota = lax.broadcasted_iota(jnp.int32, (GROUP_SIZE, tm), 0)
    scores = []
    for g in range(N_GROUPS):
        slab = sel[g * GROUP_SIZE:(g + 1) * GROUP_SIZE, :]
        m1 = jnp.max(slab, axis=0, keepdims=True)
        first = jnp.min(jnp.where(slab == m1, giota, GROUP_SIZE), axis=0, keepdims=True)
        m2 = jnp.max(jnp.where(giota == first, neg, slab), axis=0, keepdims=True)
        scores.append(m1 + m2)
    gs = jnp.concatenate(scores, axis=0)

    grow = lax.broadcasted_iota(jnp.int32, (N_GROUPS, tm), 0)
    gsel = jnp.zeros((N_GROUPS, tm), jnp.bool_)
    work = gs
    for _ in range(TOPK_GROUPS):
        m = jnp.max(work, axis=0, keepdims=True)
        first = jnp.min(jnp.where(work == m, grow, N_GROUPS), axis=0, keepdims=True)
        hit = grow == first
        gsel = jnp.logical_or(gsel, hit)
        work = jnp.where(hit, neg, work)

    masked = jnp.concatenate(
        [jnp.where(gsel[g:g + 1, :], sel[g * GROUP_SIZE:(g + 1) * GROUP_SIZE, :], neg) for g in range(N_GROUPS)],
        axis=0)

    erow = lax.broadcasted_iota(jnp.int32, (N_EXPERTS, tm), 0)
    ids, ws = [], []
    chosen = jnp.zeros((N_EXPERTS, tm), jnp.bool_)
    for _ in range(TOP_K):
        m = jnp.max(masked, axis=0, keepdims=True)
        first = jnp.min(jnp.where(masked == m, erow, N_EXPERTS), axis=0, keepdims=True)
        hit = erow == first
        ids.append(first)
        ws.append(jnp.sum(jnp.where(hit, aff, 0.0), axis=0, keepdims=True))
        masked = jnp.where(hit, neg, masked)
        chosen = jnp.logical_or(chosen, hit)
    w = jnp.concatenate(ws, axis=0)
    idx_ref[...] = jnp.concatenate(ids, axis=0)
    gw_ref[...] = w / jnp.sum(w, axis=0, keepdims=True) * ROUTED_SCALE

    chosen_f = jnp.where(chosen, 1.0, 0.0)
    srow = lax.broadcasted_iota(jnp.int32, (tm, tm), 0)
    scol = lax.broadcasted_iota(jnp.int32, (tm, tm), 1)
    before = jnp.where(srow < scol, 1.0, 0.0).astype(BF16)
    prefix = _dot(chosen_f.astype(BF16), before) + carry_ref[...]
    ranks = [jnp.sum(jnp.where(erow == ids[k], prefix, 0.0), axis=0, keepdims=True) for k in range(TOP_K)]
    rank_ref[...] = jnp.concatenate(ranks, axis=0).astype(jnp.int32)
    carry_ref[...] = carry_ref[...] + jnp.sum(chosen_f, axis=1, keepdims=True)
    cnt_ref[...] = carry_ref[...]


def _route(h1, w_router_t, bias_col):
    t, d = h1.shape
    tm = min(ROUTE_TM, t)
    vmem = 2 * (tm * d * 4 + N_EXPERTS * d * 2 + 3 * TOP_K * tm * 4) + tm * d * 2 + 16 * N_EXPERTS * tm * 4 + tm * tm * 8
    return pl.pallas_call(
        _route_kernel,
        grid=(t // tm,),
        in_specs=[
            pl.BlockSpec((tm, d), lambda i: (i, 0)),
            pl.BlockSpec((N_EXPERTS, d), lambda i: (0, 0)),
            pl.BlockSpec((N_EXPERTS, 1), lambda i: (0, 0)),
        ],
        out_specs=[pl.BlockSpec((TOP_K, tm), lambda i: (0, i)), pl.BlockSpec((TOP_K, tm), lambda i: (0, i)),
                   pl.BlockSpec((TOP_K, tm), lambda i: (0, i)), pl.BlockSpec((N_EXPERTS, 1), lambda i: (0, 0))],
        out_shape=[jax.ShapeDtypeStruct((TOP_K, t), jnp.int32), jax.ShapeDtypeStruct((TOP_K, t), F32),
                   jax.ShapeDtypeStruct((TOP_K, t), jnp.int32), jax.ShapeDtypeStruct((N_EXPERTS, 1), F32)],
        scratch_shapes=[pltpu.VMEM((N_EXPERTS, 1), F32)],
        compiler_params=_params(("arbitrary",), vmem),
        name="router",
    )(h1, w_router_t, bias_col)


def _dest_kernel(idx_ref, rank_ref, ps_ref, o_ref):
    tm = idx_ref.shape[1]
    erow = lax.broadcasted_iota(jnp.int32, (N_EXPERTS, tm), 0)
    ps = ps_ref[...]
    rows = []
    for k in range(TOP_K):
        base = jnp.sum(jnp.where(erow == idx_ref[k:k + 1, :], ps, 0.0), axis=0, keepdims=True)
        rows.append(base.astype(jnp.int32) + rank_ref[k:k + 1, :])
    o_ref[...] = jnp.concatenate(rows, axis=0)


def _slot_dest(idx_t, rank_t, pstart_col):
    t = idx_t.shape[1]
    tm = min(ROUTE_TM, t)
    vmem = 2 * (3 * TOP_K * tm * 4) + 6 * N_EXPERTS * tm * 4
    return pl.pallas_call(
        _dest_kernel,
        grid=(t // tm,),
        in_specs=[pl.BlockSpec((TOP_K, tm), lambda i: (0, i)), pl.BlockSpec((TOP_K, tm), lambda i: (0, i)),
                  pl.BlockSpec((N_EXPERTS, 1), lambda i: (0, 0))],
        out_specs=pl.BlockSpec((TOP_K, tm), lambda i: (0, i)),
        out_shape=jax.ShapeDtypeStruct((TOP_K, t), jnp.int32),
        compiler_params=_params(("parallel",), vmem),
        name="slot_dest",
    )(idx_t, rank_t, pstart_col)


LANES = 128
U32 = jnp.uint32
HIGH_HALF = np.uint32(0xFFFF0000)
N_DMA_PRIORITIES = 2


def _tile_rows(d):
    return d // (2 * LANES)


def _pack_tiles(tile_ref, value, base=0):
    n, d = value.shape
    sub = _tile_rows(d)
    bits = lax.bitcast_convert_type(value.astype(BF16).astype(F32), U32)
    for c in range(sub):
        low = bits[:, c * LANES:(c + 1) * LANES] >> 16
        high = bits[:, (c + sub) * LANES:(c + sub + 1) * LANES] & HIGH_HALF
        tile_ref[pl.ds(base + c, n, stride=sub), :] = low | high


def _unpack_tiles(tile_ref, n, d, base=0):
    sub = _tile_rows(d)
    words = [tile_ref[pl.ds(base + c, n, stride=sub), :] for c in range(sub)]
    lows = [lax.bitcast_convert_type(u << 16, F32) for u in words]
    highs = [lax.bitcast_convert_type(u & HIGH_HALF, F32) for u in words]
    return jnp.concatenate(lows + highs, axis=1)


def _dispatch_kernel(pend_ref, padded_ref, nu_ref, dest_ref, h_ref, xs_ref, tile_ref, zero_ref, sem, zsem):
    tm, d = h_ref.shape
    sub = _tile_rows(d)
    zrows = zero_ref.shape[0]
    n_blocks = xs_ref.shape[0] // zrows

    @pl.when(pl.program_id(0) == 0)
    def _():
        zero_ref[...] = jnp.zeros_like(zero_ref)

        def zero_block(first_row):
            return pltpu.make_async_copy(zero_ref, xs_ref.at[pl.ds(pl.multiple_of(first_row, 8), zrows), :], zsem)

        def issue(e, carry):
            @pl.when(padded_ref[e] > 0)
            def _():
                zero_block(pend_ref[e] * sub - zrows).start()
            return carry

        def drain(e, carry):
            @pl.when(padded_ref[e] > 0)
            def _():
                zero_block(pend_ref[e] * sub - zrows).wait()
            return carry

        def issue_tail(blk, carry):
            zero_block(blk * zrows).start()
            return carry

        def drain_tail(blk, carry):
            zero_block(blk * zrows).wait()
            return carry

        lax.fori_loop(0, N_EXPERTS, issue, 0)
        lax.fori_loop(nu_ref[0], n_blocks, issue_tail, 0)
        lax.fori_loop(0, N_EXPERTS, drain, 0)
        lax.fori_loop(nu_ref[0], n_blocks, drain_tail, 0)

    i = pl.program_id(0)
    slot = lax.rem(i, 2)
    _pack_tiles(tile_ref.at[slot], h_ref[...])

    def row_copy(tok, dst_row):
        return pltpu.make_async_copy(tile_ref.at[slot, pl.ds(pl.multiple_of(tok * sub, sub), sub), :],
                                     xs_ref.at[pl.ds(pl.multiple_of(dst_row * sub, sub), sub), :], sem.at[slot])

    def issue(tok, carry):
        for k in range(TOP_K):
            row_copy(tok, dest_ref[tok * TOP_K + k]).start(priority=k % N_DMA_PRIORITIES)
        return carry

    lax.fori_loop(0, tm, issue, 0)

    def wait_tile(s):
        for _ in range(TOP_K):
            pltpu.make_async_copy(tile_ref.at[s], xs_ref.at[pl.ds(0, tm * sub), :], sem.at[s]).wait()

    @pl.when(i > 0)
    def _():
        wait_tile(1 - slot)

    @pl.when(i == pl.num_programs(0) - 1)
    def _():
        wait_tile(slot)


def _dispatch(h1, dest_flat, pend, padded, n_used, n_rows_pad):
    t, d = h1.shape
    sub = _tile_rows(d)
    tm = min(DISPATCH_TM, t)
    vmem = 2 * (tm * d * 4) + tm * d * 2 + MOE_BM * d * 2 + 4 * tm * d * 4
    return pl.pallas_call(
        _dispatch_kernel,
        grid_spec=pltpu.PrefetchScalarGridSpec(
            num_scalar_prefetch=3,
            grid=(t // tm,),
            in_specs=[
                pl.BlockSpec((tm * TOP_K,), lambda i, pe, pa, nu: (i,), memory_space=pltpu.SMEM),
                pl.BlockSpec((tm, d), lambda i, pe, pa, nu: (i, 0)),
            ],
            out_specs=pl.BlockSpec(memory_space=pl.ANY),
            scratch_shapes=[pltpu.VMEM((2, tm * sub, LANES), U32), pltpu.VMEM((MOE_BM * sub, LANES), U32),
                            pltpu.SemaphoreType.DMA((2,)), pltpu.SemaphoreType.DMA(())],
        ),
        out_shape=jax.ShapeDtypeStruct((n_rows_pad * sub, LANES), U32),
        compiler_params=_params(("arbitrary",), vmem),
        name="dispatch_rows",
    )(pend, padded, n_used, dest_flat, h1)


def _expert_kernel(ps_ref, pd_ref, nu_ref, wg_ref, wu_ref, wd_ref, x_hbm, y_hbm,
                   xbuf, ybuf, wgb_ref, wub_ref, wdb_ref, xsem, ysem):
    e = pl.program_id(0)
    d = wg_ref.shape[1]
    slots = xbuf.shape[0]
    group = EXPERT_GROUP
    ahead = slots - group
    rows = xbuf.shape[1]
    bm = rows // _tile_rows(d)
    n_blocks = y_hbm.shape[0] // rows
    n_used = nu_ref[0]
    first = ps_ref[e] // bm
    nb = pd_ref[e] // bm

    def x_copy(g, slot):
        return pltpu.make_async_copy(x_hbm.at[pl.ds(pl.multiple_of(g * rows, rows), rows), :], xbuf.at[slot],
                                     xsem.at[slot])

    def y_copy(g, slot):
        return pltpu.make_async_copy(ybuf.at[slot], y_hbm.at[pl.ds(pl.multiple_of(g * rows, rows), rows), :],
                                     ysem.at[slot])

    @pl.when(e == 0)
    def _():
        for g0 in range(ahead):
            @pl.when(g0 < n_used)
            def _():
                x_copy(g0, g0).start()

    @pl.when(nb > 0)
    def _():
        wgb_ref[...] = wg_ref[0].astype(BF16)
        wub_ref[...] = wu_ref[0].astype(BF16)
        wdb_ref[...] = wd_ref[0].astype(BF16)

        def acquire(g):
            slot = lax.rem(g, slots)
            x_copy(g, slot).wait()

            @pl.when(g + ahead < n_used)
            def _():
                x_copy(g + ahead, lax.rem(g + ahead, slots)).start()

            @pl.when(g >= slots)
            def _():
                y_copy(g - slots, slot).wait()

            return slot

        def compute(block_slots):
            xs = [_unpack_tiles(xbuf.at[s], bm, d).astype(BF16) for s in block_slots]
            gates = [_dot(xb, wgb_ref[...]) for xb in xs]
            ups = [_dot(xb, wub_ref[...]) for xb in xs]
            hs = [(_silu(hg) * hu).astype(BF16) for hg, hu in zip(gates, ups)]
            ys = [_dot(hb, wdb_ref[...]) for hb in hs]
            for s, y in zip(block_slots, ys):
                _pack_tiles(ybuf.at[s], y)

        def run_group(g, n):
            block_slots = [acquire(g + u) for u in range(n)]
            compute(block_slots)
            for u, s in enumerate(block_slots):
                y_copy(g + u, s).start()

        def full_group(jj, carry):
            run_group(first + group * jj, group)
            return carry

        lax.fori_loop(0, nb // group, full_group, 0)
        done = nb - lax.rem(nb, group)
        size = group // 2
        while size >= 1:
            @pl.when(lax.rem(nb, 2 * size) >= size)
            def _(size=size, done=done):
                run_group(first + done, size)

            done = done + jnp.where(lax.rem(nb, 2 * size) >= size, size, 0)
            size //= 2

    @pl.when(e == pl.num_programs(0) - 1)
    def _():
        def drain_used(g, carry):
            y_copy(g, lax.rem(g, slots)).wait()
            return carry

        lax.fori_loop(jnp.maximum(n_used - slots, 0), n_used, drain_used, 0)
        ybuf[0] = jnp.zeros_like(ybuf[0])

        def issue(g, carry):
            y_copy(g, 0).start()
            return carry

        def drain(g, carry):
            y_copy(g, 0).wait()
            return carry

        lax.fori_loop(n_used, n_blocks, issue, 0)
        lax.fori_loop(n_used, n_blocks, drain, 0)


def _experts(x_sorted, pstart, padded, n_used, w_gate, w_up, w_down):
    n_exp, d, ff = w_gate.shape
    sub = _tile_rows(d)
    bm = MOE_BM
    vmem = 2 * (3 * d * ff * 4) + 3 * d * ff * 2 + 2 * EXPERT_SLOTS * bm * d * 2 + 6 * bm * ff * 4 + 6 * bm * d * 4
    return pl.pallas_call(
        _expert_kernel,
        grid_spec=pltpu.PrefetchScalarGridSpec(
            num_scalar_prefetch=3,
            grid=(n_exp,),
            in_specs=[
                pl.BlockSpec((1, d, ff), lambda e, ps, pd, nu: (e, 0, 0)),
                pl.BlockSpec((1, d, ff), lambda e, ps, pd, nu: (e, 0, 0)),
                pl.BlockSpec((1, ff, d), lambda e, ps, pd, nu: (e, 0, 0)),
                pl.BlockSpec(memory_space=pl.ANY),
            ],
            out_specs=pl.BlockSpec(memory_space=pl.ANY),
            scratch_shapes=[pltpu.VMEM((EXPERT_SLOTS, bm * sub, LANES), U32),
                            pltpu.VMEM((EXPERT_SLOTS, bm * sub, LANES), U32),
                            pltpu.VMEM((d, ff), BF16), pltpu.VMEM((d, ff), BF16), pltpu.VMEM((ff, d), BF16),
                            pltpu.SemaphoreType.DMA((EXPERT_SLOTS,)), pltpu.SemaphoreType.DMA((EXPERT_SLOTS,))],
        ),
        out_shape=jax.ShapeDtypeStruct(x_sorted.shape, U32),
        compiler_params=_params(("arbitrary",), vmem),
        name="expert_mlp",
    )(pstart, padded, n_used, w_gate, w_up, w_down, x_sorted)


def _final_kernel(dest_ref, dest_next_ref, h_ref, gw_ref, y_ref, wg_ref, wu_ref, wd_ref, g_ref, b_ref, o_ref,
                  rows_a, rows_b, shared_ref, sem):
    i = pl.program_id(0)
    tm, d = h_ref.shape
    sub = _tile_rows(d)
    grp = COMBINE_GROUP

    def issue_tokens(dref, buf, s, tok0):
        for u in range(grp):
            for k in range(TOP_K):
                src_row = dref[(tok0 + u) * TOP_K + k]
                pltpu.make_async_copy(
                    y_ref.at[pl.ds(pl.multiple_of(src_row * sub, sub), sub), :],
                    buf.at[pl.ds(pl.multiple_of((k * tm + tok0 + u) * sub, sub), sub), :],
                    sem.at[s]).start(priority=k % N_DMA_PRIORITIES)

    def wait_tile(buf, s):
        pltpu.make_async_copy(y_ref.at[pl.ds(0, TOP_K * tm * sub), :], buf, sem.at[s]).wait()

    @pl.when(i == 0)
    def _():
        def first(j, carry):
            issue_tokens(dest_ref, rows_a, 0, j * grp)
            return carry

        lax.fori_loop(0, tm // grp, first, 0)

    hb = h_ref[...].astype(BF16)
    shared_ref[...] = _dot((_silu(_dot(hb, wg_ref[...])) * _dot(hb, wu_ref[...])).astype(BF16), wd_ref[...])
    ln_g = g_ref[...]
    ln_b = b_ref[...]

    def reduce_tile(cur, cur_s, nxt, nxt_s):
        wait_tile(cur, cur_s)

        def reduce_group(j, carry):
            tok0 = j * grp if isinstance(j, int) else pl.multiple_of(j * grp, grp)
            issue_tokens(dest_next_ref, nxt, nxt_s, tok0)
            tok = pl.ds(tok0, grp)
            gw = gw_ref[tok, :]
            routed = gw[:, 0:1] * _unpack_tiles(cur, grp, d, base=tok0 * sub)
            for k in range(1, TOP_K):
                routed = routed + gw[:, k:k + 1] * _unpack_tiles(cur, grp, d, base=(k * tm + tok0) * sub)
            o_ref[tok, :] = _layer_norm(DEEPNORM_ALPHA * h_ref[tok, :] + (routed + shared_ref[tok, :]), ln_g, ln_b)
            return carry

        for j in range(tm // grp):
            reduce_group(j, 0)

        @pl.when(i == pl.num_programs(0) - 1)
        def _():
            wait_tile(nxt, nxt_s)

    @pl.when(lax.rem(i, 2) == 0)
    def _():
        reduce_tile(rows_a, 0, rows_b, 1)

    @pl.when(lax.rem(i, 2) == 1)
    def _():
        reduce_tile(rows_b, 1, rows_a, 0)


def _final(h1, dest_flat, gw, y_sorted, w_sg, w_su, w_sd, g2, b2):
    t, d = h1.shape
    ff = w_sg.shape[1]
    sub = _tile_rows(d)
    tm = min(FINAL_TM, t)
    nt = t // tm
    vmem = 2 * (2 * tm * d * 4 + tm * LANES * 4 + 3 * d * ff * 2) + 2 * TOP_K * tm * d * 2 + 8 * tm * d * 4

    def full(shape):
        return pl.BlockSpec(shape, lambda i: tuple(0 for _ in shape))

    return pl.pallas_call(
        _final_kernel,
        grid=(nt,),
        in_specs=[
            pl.BlockSpec((tm * TOP_K,), lambda i: (i,), memory_space=pltpu.SMEM),
            pl.BlockSpec((tm * TOP_K,), lambda i: (jnp.minimum(i + 1, nt - 1),), memory_space=pltpu.SMEM),
            pl.BlockSpec((tm, d), lambda i: (i, 0)),
            pl.BlockSpec((tm, TOP_K), lambda i: (i, 0)),
            pl.BlockSpec(memory_space=pl.ANY),
            full((d, ff)), full((d, ff)), full((ff, d)), full((1, d)), full((1, d)),
        ],
        out_specs=pl.BlockSpec((tm, d), lambda i: (i, 0)),
        out_shape=jax.ShapeDtypeStruct((t, d), F32),
        scratch_shapes=[pltpu.VMEM((TOP_K * tm * sub, LANES), U32), pltpu.VMEM((TOP_K * tm * sub, LANES), U32),
                        pltpu.VMEM((tm, d), F32), pltpu.SemaphoreType.DMA((2,))],
        compiler_params=_params(("arbitrary",), vmem),
        name="combine_final",
    )(dest_flat, dest_flat, h1, gw, y_sorted, w_sg, w_su, w_sd, g2, b2)


def _lower_bound(p):
    return jnp.cumsum(jax.nn.softmax(p.astype(F32), axis=0), axis=0)[0:1]


def _block_plan(counts, n_rows):
    bm = MOE_BM
    counts = counts.reshape(-1).astype(jnp.int32)
    padded = (counts + bm - 1) // bm * bm
    pend = jnp.cumsum(padded).astype(jnp.int32)
    pstart = pend - padded
    n_blocks = n_rows // bm + N_EXPERTS
    n_used = pend[-1:] // bm
    return pstart, pend, padded, n_used.astype(jnp.int32), n_blocks


def kernel(x, mem, ln_in_g, ln_in_b, ln_mem_g, ln_mem_b, hgrn_lb_fwd, hgrn_lb_bwd, w_in, b_gate, hgrn_norm_g, w_mem_kv, w_fourier_o, b_fourier_o, w_hgrn_o, w_xattn_o, w_out, ln1_g, ln1_b, w_router, router_bias, w_exp_gate, w_exp_up, w_exp_down, w_sh_gate, w_sh_up, w_sh_down, ln2_g, ln2_b):
    bsz, s, d = x.shape
    t = bsz * s
    l = 0
    row = lambda v: v.reshape(1, -1).astype(F32)

    n_gate = N_BRANCHES * d
    n_rest = w_in.shape[2] - n_gate
    w_in_r = jnp.concatenate([w_in[l][:, n_rest:], w_in[l][:, :n_rest]], axis=1).astype(BF16)
    gate_blocks = n_gate // INPROJ_TN
    cb_fourier = gate_blocks
    cb_hq, cb_hi, cb_zf, cb_zb, cb_hg, cb_xq = (gate_blocks + 1 + n for n in range(6))

    x2 = x.reshape(t, d)
    proj = _inproj(x2, row(ln_in_g), row(ln_in_b), w_in_r)
    kv = _mem_kv(mem, row(ln_mem_g), row(ln_mem_b), w_mem_kv[l].astype(BF16))

    fm = _fourier(proj, bsz, s, cb_fourier * (INPROJ_TN // FOURIER_GROUP_DIM))

    lb_f = _lower_bound(hgrn_lb_fwd)
    lb_b = _lower_bound(hgrn_lb_bwd)
    o_fwd = _hgrn_sweep(proj, lb_f, (cb_hq, cb_hi, cb_zf), bsz, s, reverse=False)
    ho = _hgrn_sweep(proj, lb_b, (cb_hq, cb_hi, cb_zb, cb_hg), bsz, s, reverse=True,
                     o_fwd=o_fwd, norm_g=row(hgrn_norm_g[l]))

    h1 = _merge(x2, row(ln_in_g), row(ln_in_b), proj, 0, cb_xq, fm, ho, kv,
                w_xattn_o[l].astype(BF16), w_fourier_o[l].astype(BF16), row(b_fourier_o[l]),
                w_hgrn_o[l].astype(BF16), row(b_gate[l]), w_out[l].astype(BF16), row(ln1_g[l]), row(ln1_b[l]), s)

    idx_t, gw_t, rank_t, counts = _route(h1, w_router[l].T.astype(BF16),
                                         router_bias[l].reshape(N_EXPERTS, 1).astype(F32))
    pstart, pend, padded, n_used, n_blocks = _block_plan(counts, t * TOP_K)
    dest_t = _slot_dest(idx_t, rank_t, pstart.astype(F32).reshape(N_EXPERTS, 1))
    dest_flat = dest_t.T.reshape(-1)

    x_sorted = _dispatch(h1, dest_flat, pend, padded, n_used, n_blocks * MOE_BM)
    y_sorted = _experts(x_sorted, pstart, padded, n_used, w_exp_gate[l], w_exp_up[l], w_exp_down[l])
    out = _final(h1, dest_flat, gw_t.T, y_sorted, w_sh_gate[l].astype(BF16), w_sh_up[l].astype(BF16),
                 w_sh_down[l].astype(BF16), row(ln2_g[l]), row(ln2_b[l]))
    return out.reshape(bsz, s, d)
```
